```python
import math
import jax
import jax.numpy as jnp
from jax import lax
import numpy as np

D_MODEL = 1024
BATCH = 16
SEQ = 2048
DEPTH = 4

N_MIXERS = 4
GROUP_WIDTH = D_MODEL // N_MIXERS
HEAD_DIM = 64
RET_HEADS = GROUP_WIDTH // HEAD_DIM
RET_CHUNK = 128
S5_CH = 16
S5_GROUPS = GROUP_WIDTH // S5_CH
S5_STATE = 64
S5_DT_MIN = 0.001
S5_DT_MAX = 0.1
SGU_GROUPS = 4
SGU_CHUNK = 128
NSA_HEADS = GROUP_WIDTH // HEAD_DIM
NSA_KV_HEADS = 2
NSA_GQA = NSA_HEADS // NSA_KV_HEADS
NSA_KV_WIDTH = NSA_KV_HEADS * HEAD_DIM
CMP_STRIDE = 16
CMP_BLOCK = 2 * CMP_STRIDE
CMP_HIDDEN = 128
SEL_BLOCK = 64
SEL_TOPK = 8
FORCE_BONUS = 1e3
WIN = 512
SEL_Q_BLOCK = 64
WIN_Q_BLOCK = 128
D_FF = 2816
CONV_W = 3
ROPE_THETA = 10000.0
EPS = 1e-6
NEG = -1e9
IN_COLS_LIST = [GROUP_WIDTH] * 4 + [GROUP_WIDTH] + [GROUP_WIDTH] * 2 + [GROUP_WIDTH] + [NSA_KV_WIDTH] * 6 + [3 * NSA_HEADS]
IN_COLS = 8 * GROUP_WIDTH + 6 * NSA_KV_WIDTH + 3 * NSA_HEADS

kernel_name = 'hybrid_parallel_group_decoder'


def rmsnorm(x, g):
    xf = x.astype(jnp.float32)
    y = xf * lax.rsqrt(jnp.mean(xf * xf, axis=-1, keepdims=True) + EPS)
    return (y * g.astype(jnp.float32)).astype(x.dtype)


def rope_tables(pos):
    inv = ROPE_THETA ** (-jnp.arange(0, HEAD_DIM, 2, dtype=jnp.float32) / HEAD_DIM)
    ang = pos.astype(jnp.float32)[:, None] * inv[None, :]
    return jnp.cos(ang), jnp.sin(ang)


def rope(x, cos, sin):
    half = x.shape[-1] // 2
    x1, x2 = x[..., :half], x[..., half:]
    cos = cos.astype(x.dtype)
    sin = sin.astype(x.dtype)
    return jnp.concatenate([x1 * cos - x2 * sin, x2 * cos + x1 * sin], axis=-1)


def retention(q, k, v, g, norm_g, cos, sin):
    B, S, _ = q.shape
    H, dh, C = RET_HEADS, HEAD_DIM, RET_CHUNK
    NC = S // C
    dt = q.dtype

    def heads(t):
        return t.reshape(B, S, H, dh).transpose(0, 2, 1, 3)

    qh = rope(heads(q), cos, sin) * (dh ** -0.5)
    kh = rope(heads(k), cos, sin)
    vh = heads(v)
    log_gamma = jnp.log1p(-(2.0 ** (-5.0 - jnp.arange(H, dtype=jnp.float32))))
    idx = jnp.arange(C, dtype=jnp.float32)
    diff = idx[:, None] - idx[None, :]
    decay_in = jnp.where(diff >= 0, jnp.exp(log_gamma[:, None, None] * jnp.maximum(diff, 0.0)), 0.0).astype(dt)
    zeta = jnp.exp(log_gamma[:, None] * (C - 1 - idx)[None, :]).astype(dt)
    xi = jnp.exp(log_gamma[:, None] * (idx + 1)[None, :]).astype(dt)
    gamma_chunk = jnp.exp(log_gamma * C)
    qc = qh.reshape(B, H, NC, C, dh)
    kc = kh.reshape(B, H, NC, C, dh)
    vc = vh.reshape(B, H, NC, C, dh)
    scores = jnp.einsum('bhnid,bhnjd->bhnij', qc, kc) * decay_in[:, None]
    o_inner = jnp.einsum('bhnij,bhnje->bhnie', scores, vc)
    U = jnp.einsum('bhnjd,bhnje->nbhde', kc * zeta[:, None, :, None], vc)

    def step(R, u):
        new = (gamma_chunk[None, :, None, None] * R + u).astype(R.dtype)
        return new, R

    _, R_prev = lax.scan(step, jnp.zeros_like(U[0]), U)
    o_cross = jnp.einsum('bhnid,nbhde->bhnie', qc, R_prev) * xi[:, None, :, None]
    o = (o_inner + o_cross).reshape(B, H, S, dh)
    o = rmsnorm(o, norm_g.reshape(H, 1, dh))
    o = o.transpose(0, 2, 1, 3).reshape(B, S, H * dh)
    return (jax.nn.silu(g) * o).astype(dt)


def s5_layer(u, lam_re, lam_im, log_dt, b_re, b_im, c_re, c_im, d, glu_w, glu_b):
    B, S, _ = u.shape
    G, Hc = S5_GROUPS, S5_CH
    f32 = jnp.float32
    ug = u.reshape(B, S, G, Hc)
    dts = jnp.exp(log_dt.astype(f32))[:, None]
    lr = lam_re.astype(f32)
    li = lam_im.astype(f32)
    mag = jnp.exp(lr * dts)
    a_re = mag * jnp.cos(li * dts)
    a_im = mag * jnp.sin(li * dts)
    den = lr * lr + li * li
    f_re = ((a_re - 1.0) * lr + a_im * li) / den
    f_im = (a_im * lr - (a_re - 1.0) * li) / den
    bu_re = jnp.einsum('bsgh,gph->bsgp', ug, b_re).astype(f32)
    bu_im = jnp.einsum('bsgh,gph->bsgp', ug, b_im).astype(f32)
    x_re = f_re * bu_re - f_im * bu_im
    x_im = f_re * bu_im + f_im * bu_re
    a_re_b = jnp.broadcast_to(a_re, x_re.shape)
    a_im_b = jnp.broadcast_to(a_im, x_re.shape)

    def combine(e1, e2):
        a1r, a1i, b1r, b1i = e1
        a2r, a2i, b2r, b2i = e2
        return (a2r * a1r - a2i * a1i, a2r * a1i + a2i * a1r,
                a2r * b1r - a2i * b1i + b2r, a2r * b1i + a2i * b1r + b2i)

    _, _, h_re, h_im = lax.associative_scan(combine, (a_re_b, a_im_b, x_re, x_im), axis=1)
    y = jnp.einsum('bsgp,ghp->bsgh', h_re, c_re.astype(f32)) - jnp.einsum('bsgp,ghp->bsgh', h_im, c_im.astype(f32))
    y = y.reshape(B, S, GROUP_WIDTH) + d.astype(f32) * u.astype(f32)
    y = jax.nn.gelu(y)
    out = y * jax.nn.sigmoid(y @ glu_w.astype(f32) + glu_b.astype(f32))
    return out.astype(u.dtype)


def spatial_gating(u, v, norm_g, w_s, b_s):
    B, S, _ = u.shape
    G, C = SGU_GROUPS, SGU_CHUNK
    dg = GROUP_WIDTH // G
    u = jax.nn.gelu(u)
    v = rmsnorm(jax.nn.gelu(v).reshape(B, S, G, dg), norm_g.reshape(G, dg))
    vc = v.reshape(B, S // C, C, G, dg)
    mask = jnp.tril(jnp.ones((C, C), dtype=bool))
    w = jnp.where(mask, w_s, 0.0).astype(v.dtype)
    s = jnp.einsum('gts,bnsgd->bntgd', w, vc) + b_s.T[None, None, :, :, None]
    return (u * s.reshape(B, S, GROUP_WIDTH)).astype(u.dtype)


def nsa(q, kc, vc, ks, vs, kw, vw, gate_logits, q_g, k_g, cmp_pe, cmp_w1, cmp_w2, cos, sin):
    B, S, _ = q.shape
    KVH, G, dh = NSA_KV_HEADS, NSA_GQA, HEAD_DIM
    f32 = jnp.float32
    pos = jnp.arange(S)
    qh = q.reshape(B, S, KVH, G, dh).transpose(0, 2, 3, 1, 4)
    qh = rope(rmsnorm(qh, q_g), cos, sin) * (dh ** -0.5)

    def kv_heads(t):
        return t.reshape(B, S, KVH, dh).transpose(0, 2, 1, 3)

    n_cmp = S // CMP_STRIDE - 1

    def compress(t, pe, w1, w2):
        halves = kv_heads(t).reshape(B, KVH, S // CMP_STRIDE, CMP_STRIDE, dh)
        blocks = jnp.concatenate([halves[:, :, :-1], halves[:, :, 1:]], axis=3) + pe
        hid = jax.nn.gelu(blocks.reshape(B, KVH, n_cmp, CMP_BLOCK * dh) @ w1)
        return hid @ w2

    cmp_end = jnp.arange(n_cmp) * CMP_STRIDE + CMP_BLOCK - 1
    cos_c, sin_c = rope_tables(cmp_end)
    k_cmp = rope(rmsnorm(compress(kc, cmp_pe[0], cmp_w1[0], cmp_w2[0]), k_g[0]), cos_c, sin_c)
    v_cmp = compress(vc, cmp_pe[1], cmp_w1[1], cmp_w2[1])
    s_c = jnp.einsum('bhgqd,bhnd->bhgqn', qh, k_cmp).astype(f32)
    valid_c = cmp_end[None, :] <= pos[:, None]
    p_c = jax.nn.softmax(jnp.where(valid_c, s_c, NEG), axis=-1) * valid_c
    o_cmp = jnp.einsum('bhgqn,bhnd->bhgqd', p_c.astype(v_cmp.dtype), v_cmp)

    n_sel = S // SEL_BLOCK
    ii = np.arange(n_cmp)[:, None]
    jj = np.arange(n_sel)[None, :]
    overlap = ((ii * CMP_STRIDE < (jj + 1) * SEL_BLOCK) & (ii * CMP_STRIDE + CMP_BLOCK > jj * SEL_BLOCK)).astype(np.float32)
    imp = jnp.einsum('bhgqn,nj->bhqj', p_c, jnp.asarray(overlap))
    blk = jnp.arange(n_sel)[None, :]
    cur = (pos // SEL_BLOCK)[:, None]
    forced = (blk == 0) | (blk == cur) | (blk == cur - 1)
    imp = jnp.where(forced, imp + FORCE_BONUS, imp)
    imp = jnp.where(blk <= cur, imp, NEG)
    k_top = min(SEL_TOPK, n_sel)
    _, sel_idx = lax.top_k(imp, k_top)

    k_sel = rope(rmsnorm(kv_heads(ks), k_g[1]), cos, sin).reshape(B, KVH, n_sel, SEL_BLOCK, dh)
    v_sel = kv_heads(vs).reshape(B, KVH, n_sel, SEL_BLOCK, dh)
    nqb = S // SEL_Q_BLOCK
    q_b = jnp.moveaxis(qh.reshape(B, KVH, G, nqb, SEL_Q_BLOCK, dh), 3, 0)
    idx_b = jnp.moveaxis(sel_idx.reshape(B, KVH, nqb, SEL_Q_BLOCK, k_top), 2, 0)
    pos_b = pos.reshape(nqb, SEL_Q_BLOCK)
    b_i = jnp.arange(B)[:, None, None, None]
    h_i = jnp.arange(KVH)[None, :, None, None]

    def sel_block(args):
        qb, ib, tb = args
        kg = k_sel[b_i, h_i, ib]
        vg = v_sel[b_i, h_i, ib]
        s = jnp.einsum('bhgqd,bhqnkd->bhgqnk', qb, kg).astype(f32)
        kpos = ib[..., None] * SEL_BLOCK + jnp.arange(SEL_BLOCK)
        ok = (kpos <= tb[None, None, :, None, None])[:, :, None]
        s = jnp.where(ok, s, NEG).reshape(B, KVH, G, SEL_Q_BLOCK, k_top * SEL_BLOCK)
        p = jax.nn.softmax(s, axis=-1).reshape(B, KVH, G, SEL_Q_BLOCK, k_top, SEL_BLOCK)
        return jnp.einsum('bhgqnk,bhqnkd->bhgqd', p.astype(vg.dtype), vg)

    o_sel = lax.map(sel_block, (q_b, idx_b, pos_b))
    o_sel = jnp.moveaxis(o_sel, 0, 3).reshape(B, KVH, G, S, dh)

    k_win = rope(rmsnorm(kv_heads(kw), k_g[2]), cos, sin)
    v_win = kv_heads(vw)
    pad = ((0, 0), (0, 0), (WIN, 0), (0, 0))
    kp = jnp.pad(k_win, pad)
    vp = jnp.pad(v_win, pad)
    nqw = S // WIN_Q_BLOCK
    band = jnp.arange(nqw)[:, None] * WIN_Q_BLOCK + jnp.arange(WIN + WIN_Q_BLOCK)[None, :]
    kb = kp[:, :, band]
    vb = vp[:, :, band]
    qw = qh.reshape(B, KVH, G, nqw, WIN_Q_BLOCK, dh)
    s_w = jnp.einsum('bhgnqd,bhnkd->bhgnqk', qw, kb).astype(f32)
    tq = pos.reshape(nqw, WIN_Q_BLOCK)
    kpos = band - WIN
    dist = tq[:, :, None] - kpos[:, None, :]
    ok_w = (dist >= 0) & (dist < WIN) & (kpos[:, None, :] >= 0)
    p_w = jax.nn.softmax(jnp.where(ok_w, s_w, NEG), axis=-1)
    o_win = jnp.einsum('bhgnqk,bhnkd->bhgnqd', p_w.astype(vb.dtype), vb).reshape(B, KVH, G, S, dh)

    gates = jax.nn.sigmoid(gate_logits.reshape(B, S, KVH, G, 3).transpose(0, 2, 3, 1, 4))
    o = gates[..., 0:1] * o_cmp + gates[..., 1:2] * o_sel + gates[..., 2:3] * o_win
    return o.transpose(0, 3, 1, 2, 4).reshape(B, S, GROUP_WIDTH).astype(q.dtype)


def conv_ffn(h, w_up, conv_w, conv_b, w_down):
    a = h @ w_up
    ch = a.shape[-1]
    a = lax.conv_general_dilated(a, conv_w.reshape(CONV_W, 1, ch).astype(a.dtype), window_strides=(1,),
                                 padding=((CONV_W - 1, 0),), dimension_numbers=('NWC', 'WIO', 'NWC'),
                                 feature_group_count=ch) + conv_b
    gate, up = jnp.split(a, 2, axis=-1)
    return (jax.nn.silu(gate) * up) @ w_down


def setup_inputs(seed: int = 0) -> dict:
    key = jax.random.key(seed)
    ks = iter(jax.random.split(key, 40))
    L = DEPTH

    def nrm(shape, scale):
        return jax.random.normal(next(ks), shape, jnp.float32) * scale

    def gain(shape):
        return 1.0 + nrm(shape, 0.05)

    n_idx = jnp.arange(S5_STATE, dtype=jnp.float32)
    x = nrm((BATCH, SEQ, D_MODEL), 1.0)
    c = nrm((BATCH, D_MODEL), 1.0)
    norm1_g = gain((L, D_MODEL))
    norm2_g = gain((L, D_MODEL))
    ada_w = nrm((L, D_MODEL, 6 * D_MODEL), 0.5 * D_MODEL ** -0.5)
    ada_b = nrm((L, 6 * D_MODEL), 0.01)
    w_in = nrm((L, D_MODEL, IN_COLS), D_MODEL ** -0.5)
    ret_norm_g = gain((L, GROUP_WIDTH))
    s5_lambda_re = -0.5 + nrm((L, S5_GROUPS, S5_STATE), 0.01)
    s5_lambda_im = math.pi * n_idx + nrm((L, S5_GROUPS, S5_STATE), 0.01)
    s5_log_dt = jax.random.uniform(next(ks), (L, S5_GROUPS), jnp.float32, math.log(S5_DT_MIN), math.log(S5_DT_MAX))
    s5_b_re = nrm((L, S5_GROUPS, S5_STATE, S5_CH), (2 * S5_CH) ** -0.5)
    s5_b_im = nrm((L, S5_GROUPS, S5_STATE, S5_CH), (2 * S5_CH) ** -0.5)
    s5_c_re = nrm((L, S5_GROUPS, S5_CH, S5_STATE), (2 * S5_STATE) ** -0.5)
    s5_c_im = nrm((L, S5_GROUPS, S5_CH, S5_STATE), (2 * S5_STATE) ** -0.5)
    s5_d = nrm((L, GROUP_WIDTH), 1.0)
    s5_glu_w = nrm((L, GROUP_WIDTH, GROUP_WIDTH), GROUP_WIDTH ** -0.5)
    s5_glu_b = nrm((L, GROUP_WIDTH), 0.01)
    sgu_norm_g = gain((L, GROUP_WIDTH))
    sgu_w = nrm((L, SGU_GROUPS, SGU_CHUNK, SGU_CHUNK), 0.05)
    sgu_b = 1.0 + nrm((L, SGU_GROUPS, SGU_CHUNK), 0.05)
    nsa_q_norm_g = gain((L, HEAD_DIM))
    nsa_k_norm_g = gain((L, 3, HEAD_DIM))
    nsa_cmp_pe = nrm((L, 2, CMP_BLOCK, HEAD_DIM), 0.1)
    nsa_cmp_w1 = nrm((L, 2, CMP_BLOCK * HEAD_DIM, CMP_HIDDEN), (CMP_BLOCK * HEAD_DIM) ** -0.5)
    nsa_cmp_w2 = nrm((L, 2, CMP_HIDDEN, HEAD_DIM), CMP_HIDDEN ** -0.5)
    mix_norm_g = gain((L, D_MODEL))
    w_out = nrm((L, D_MODEL, D_MODEL), D_MODEL ** -0.5)
    ffn_w_up = nrm((L, D_MODEL, 2 * D_FF), D_MODEL ** -0.5)
    ffn_conv_w = nrm((L, CONV_W, 2 * D_FF), CONV_W ** -0.5)
    ffn_conv_b = nrm((L, 2 * D_FF), 0.01)
    ffn_w_down = nrm((L, D_FF, D_MODEL), D_FF ** -0.5)
    return {'x': x, 'c': c, 'norm1_g': norm1_g, 'norm2_g': norm2_g, 'ada_w': ada_w, 'ada_b': ada_b,
            'w_in': w_in, 'ret_norm_g': ret_norm_g,
            's5_lambda_re': s5_lambda_re, 's5_lambda_im': s5_lambda_im, 's5_log_dt': s5_log_dt,
            's5_b_re': s5_b_re, 's5_b_im': s5_b_im, 's5_c_re': s5_c_re, 's5_c_im': s5_c_im,
            's5_d': s5_d, 's5_glu_w': s5_glu_w, 's5_glu_b': s5_glu_b,
            'sgu_norm_g': sgu_norm_g, 'sgu_w': sgu_w, 'sgu_b': sgu_b,
            'nsa_q_norm_g': nsa_q_norm_g, 'nsa_k_norm_g': nsa_k_norm_g, 'nsa_cmp_pe': nsa_cmp_pe,
            'nsa_cmp_w1': nsa_cmp_w1, 'nsa_cmp_w2': nsa_cmp_w2,
            'mix_norm_g': mix_norm_g, 'w_out': w_out,
            'ffn_w_up': ffn_w_up, 'ffn_conv_w': ffn_conv_w, 'ffn_conv_b': ffn_conv_b, 'ffn_w_down': ffn_w_down}


def reference(x, c, norm1_g, norm2_g, ada_w, ada_b, w_in, ret_norm_g,
              s5_lambda_re, s5_lambda_im, s5_log_dt, s5_b_re, s5_b_im, s5_c_re, s5_c_im,
              s5_d, s5_glu_w, s5_glu_b, sgu_norm_g, sgu_w, sgu_b,
              nsa_q_norm_g, nsa_k_norm_g, nsa_cmp_pe, nsa_cmp_w1, nsa_cmp_w2,
              mix_norm_g, w_out, ffn_w_up, ffn_conv_w, ffn_conv_b, ffn_w_down):
    B, S, D = x.shape
    cos, sin = rope_tables(jnp.arange(S))
    splits = [int(v) for v in np.cumsum(IN_COLS_LIST)[:-1]]
    c_act = jax.nn.silu(c)
    for l in range(DEPTH):
        mod = c_act @ ada_w[l] + ada_b[l]
        sh1, sc1, g1, sh2, sc2, g2 = [m[:, None, :] for m in jnp.split(mod, 6, axis=-1)]
        h = rmsnorm(x, norm1_g[l]) * (1.0 + sc1) + sh1
        proj = h @ w_in[l]
        (rq, rk, rv, rg, su, gu, gv, nq, nkc, nvc, nks, nvs, nkw, nvw, ngate) = jnp.split(proj, splits, axis=-1)
        y_ret = retention(rq, rk, rv, rg, ret_norm_g[l], cos, sin)
        y_s5 = s5_layer(su, s5_lambda_re[l], s5_lambda_im[l], s5_log_dt[l], s5_b_re[l], s5_b_im[l],
                        s5_c_re[l], s5_c_im[l], s5_d[l], s5_glu_w[l], s5_glu_b[l])
        y_sgu = spatial_gating(gu, gv, sgu_norm_g[l], sgu_w[l], sgu_b[l])
        y_nsa = nsa(nq, nkc, nvc, nks, nvs, nkw, nvw, ngate, nsa_q_norm_g[l], nsa_k_norm_g[l],
                    nsa_cmp_pe[l], nsa_cmp_w1[l], nsa_cmp_w2[l], cos, sin)
        y = jnp.concatenate([y_ret, y_s5, y_sgu, y_nsa], axis=-1).reshape(B, S, N_MIXERS, GROUP_WIDTH)
        y = rmsnorm(y, mix_norm_g[l].reshape(N_MIXERS, GROUP_WIDTH)).reshape(B, S, D)
        x = x + g1 * (y @ w_out[l])
        h = rmsnorm(x, norm2_g[l]) * (1.0 + sc2) + sh2
        x = x + g2 * conv_ffn(h, ffn_w_up[l], ffn_conv_w[l], ffn_conv_b[l], ffn_w_down[l])
    return x
```

```python
import functools
import math

import jax
import jax.numpy as jnp
import numpy as np
from jax import lax
from jax.experimental import pallas as pl
from jax.experimental.pallas import tpu as pltpu

F32 = jnp.float32
BF16 = jnp.bfloat16

HEAD_DIM = 64
GROUP_WIDTH = 256
RET_CHUNK = 128
S5_GROUPS = 16
S5_CH = 16
S5_STATE = 64
S5_WIDTH = S5_GROUPS * S5_STATE
SGU_GROUPS = 4
SGU_CHUNK = 128
NSA_KV_WIDTH = 128
CMP_STRIDE = 16
CMP_BLOCK = 32
CMP_HIDDEN = 128
SEL_BLOCK = 64
SEL_TOPK = 8
FORCE_BONUS = 1e3
WIN = 512
CONV_W = 3
ROPE_THETA = 10000.0
EPS = 1e-6
NEG = -1e9
QK_SCALE = HEAD_DIM ** -0.5

ATT_TILE = 128
VMEM_LIMIT = 56 * 1024 * 1024


def _cparams(sem):
    return pltpu.CompilerParams(dimension_semantics=sem, vmem_limit_bytes=VMEM_LIMIT)


def _dot(a, b):
    return jnp.dot(a.astype(BF16), b.astype(BF16), preferred_element_type=F32)


def _dot_nt(a, b):
    return lax.dot_general(a.astype(BF16), b.astype(BF16), (((1,), (1,)), ((), ())),
                           preferred_element_type=F32)


def _dot_split(a, b_bf16):
    hi = a.astype(BF16)
    lo = (a - hi.astype(F32)).astype(BF16)
    return (jnp.dot(hi, b_bf16, preferred_element_type=F32)
            + jnp.dot(lo, b_bf16, preferred_element_type=F32))


def _sigmoid(x):
    return 1.0 / (1.0 + jnp.exp(-x))


def _silu(x):
    return x * _sigmoid(x)


def _gelu(x):
    c = math.sqrt(2.0 / math.pi)
    return 0.5 * x * (1.0 + jnp.tanh(c * (x + 0.044715 * (x * x * x))))


def _group_rms(x, ones_bd):
    ms = _dot_split(x * x, ones_bd) * (1.0 / HEAD_DIM)
    return lax.rsqrt(ms + EPS)


def _rot_half(x):
    n = x.shape[-1]
    lane = lax.broadcasted_iota(jnp.int32, x.shape, 1)
    first = (lane % HEAD_DIM) < (HEAD_DIM // 2)
    return jnp.where(first, -pltpu.roll(x, n - HEAD_DIM // 2, 1), pltpu.roll(x, HEAD_DIM // 2, 1))


def _rope(x, cos, sin):
    return x * cos + _rot_half(x) * sin


def _lane_group(shape, width):
    return lax.broadcasted_iota(jnp.int32, shape, len(shape) - 1) // width


def _mod_kernel(c_ref, w_ref, b_ref, o_ref):
    o_ref[...] = _dot(_silu(c_ref[...]), w_ref[...]) + b_ref[...]


def _mod_call(c, ada_w, ada_b):
    L, D, D6 = ada_w.shape
    B = c.shape[0]
    nj = D6 // D
    return pl.pallas_call(
        _mod_kernel,
        grid=(L, nj),
        in_specs=[pl.BlockSpec((B, D), lambda l, j: (0, 0)),
                  pl.BlockSpec((None, D, D), lambda l, j: (l, 0, j)),
                  pl.BlockSpec((None, 1, D), lambda l, j: (l, 0, j))],
        out_specs=pl.BlockSpec((None, B, D), lambda l, j: (l, 0, j)),
        out_shape=jax.ShapeDtypeStruct((L, B, D6), F32),
        compiler_params=_cparams(("arbitrary", "arbitrary")),
        name="adaln_mod",
    )(c, ada_w, ada_b.reshape(L, 1, D6))


C_RET = 0
C_S5 = 1024
C_SGU = 1280
C_NQ = 1792
C_KVC = 2048
C_KV = 2304
C_GATE = 2816
N_PROJ = 3584


def _proj_columns():
    cols = list(range(0, 1792))
    nq0 = 1792
    for g in range(2):
        for kvh in range(2):
            base = nq0 + (kvh * 2 + g) * HEAD_DIM
            cols += list(range(base, base + HEAD_DIM))
    cols += list(range(2048, 2816))
    for g in range(2):
        for br in range(3):
            for kvh in range(2):
                cols += [2816 + (kvh * 2 + g) * 3 + br] * HEAD_DIM
    assert len(cols) == N_PROJ
    return np.asarray(cols, dtype=np.int32)


def _inproj_kernel(x_ref, g_ref, sc_ref, sh_ref, w_ref, cos_ref, sin_ref, ones_ref, qg_ref, kg_ref,
                   ret_ref, s5_ref, sgu_ref, nq_ref, kvc_ref, kv_ref, gate_ref):
    x = x_ref[...]
    ms = jnp.mean(x * x, axis=-1, keepdims=True)
    h = x * lax.rsqrt(ms + EPS) * g_ref[...]
    h = h * (1.0 + sc_ref[...]) + sh_ref[...]
    hb = h.astype(BF16)

    def proj(c0, n):
        return jnp.dot(hb, w_ref[:, c0:c0 + n], preferred_element_type=F32)

    cos = cos_ref[...]
    sin = sin_ref[...]
    cos_h, sin_h = cos[:, :128], sin[:, :128]
    ones256 = ones_ref[...]
    ones128 = ones256[:128, :128]

    ret_ref[:, 0:256] = _rope(proj(C_RET, 256), cos, sin) * QK_SCALE
    ret_ref[:, 256:512] = _rope(proj(C_RET + 256, 256), cos, sin)
    ret_ref[:, 512:1024] = proj(C_RET + 512, 512)
    s5_ref[...] = proj(C_S5, 256)
    sgu_ref[...] = proj(C_SGU, 512)
    q = proj(C_NQ, 256)
    qn = q * _group_rms(q, ones256) * qg_ref[...]
    nq_ref[...] = _rope(qn, cos, sin) * QK_SCALE
    kvc_ref[...] = proj(C_KVC, 256)
    kg = kg_ref[...]
    for i in range(2):
        k = proj(C_KV + i * 256, 128)
        kn = k * _group_rms(k, ones128) * kg[i + 1:i + 2, :]
        kv_ref[:, i * 256:i * 256 + 128] = _rope(kn, cos_h, sin_h)
        kv_ref[:, i * 256 + 128:i * 256 + 256] = proj(C_KV + i * 256 + 128, 128)
    gate_ref[...] = _sigmoid(proj(C_GATE, 768))


def _inproj_call(x, norm_g, sc, sh, w, cos, sin, ones256, q_gain, k_gain, ts):
    B, S, D = x.shape
    row = lambda n: pl.BlockSpec((None, ts, n), lambda b, s: (b, s, 0))
    const = lambda shape: pl.BlockSpec(shape, lambda b, s: (0,) * len(shape))
    vec = pl.BlockSpec((None, 1, D), lambda b, s: (b, 0, 0))
    return pl.pallas_call(
        _inproj_kernel,
        grid=(B, S // ts),
        in_specs=[row(D), const((1, D)), vec, vec, const((D, N_PROJ)),
                  pl.BlockSpec((ts, 256), lambda b, s: (s, 0)),
                  pl.BlockSpec((ts, 256), lambda b, s: (s, 0)),
                  const((256, 256)), const((1, 256)), const((3, 128))],
        out_specs=[row(1024), pl.BlockSpec((ts, 256), lambda b, s: (s, b)), row(512), row(256),
                   row(256), row(512), row(768)],
        out_shape=[jax.ShapeDtypeStruct((B, S, 1024), F32),
                   jax.ShapeDtypeStruct((S, B * 256), F32),
                   jax.ShapeDtypeStruct((B, S, 512), F32),
                   jax.ShapeDtypeStruct((B, S, 256), F32),
                   jax.ShapeDtypeStruct((B, S, 256), F32),
                   jax.ShapeDtypeStruct((B, S, 512), F32),
                   jax.ShapeDtypeStruct((B, S, 768), F32)],
        compiler_params=_cparams(("arbitrary", "arbitrary")),
        name="in_proj",
    )(x, norm_g, sc, sh, w, cos, sin, ones256, q_gain, k_gain)


def _retention_kernel(blk_ref, dec_ref, zeta_ref, xi_ref, gbd_ref, bd_ref, ones_ref, ng_ref,
                      o_ref, r_ref):
    @pl.when(pl.program_id(1) == 0)
    def _():
        r_ref[...] = jnp.zeros_like(r_ref)

    q = blk_ref[:, 0:256]
    k = blk_ref[:, 256:512]
    v = blk_ref[:, 512:768]
    g = blk_ref[:, 768:1024]
    kb = k.astype(BF16)
    vb = v.astype(BF16)
    head = _lane_group(q.shape, HEAD_DIM)
    o = jnp.zeros(q.shape, F32)
    for h in range(GROUP_WIDTH // HEAD_DIM):
        qh = jnp.where(head == h, q, 0.0)
        s = _dot_nt(qh, kb) * dec_ref[h]
        o = o + jnp.where(head == h, _dot(s, vb), 0.0)
    r_prev = r_ref[...]
    o = o + _dot(q, r_prev) * xi_ref[...]
    kz_t = (k * zeta_ref[...]).T
    u = _dot(kz_t, vb)
    r_ref[...] = r_prev * gbd_ref[...] + u * bd_ref[...]
    o = o * _group_rms(o, ones_ref[...]) * ng_ref[...]
    o_ref[...] = _silu(g) * o


def _retention_call(ret, dec, zeta, xi, gbd, bd, ones256, norm_g):
    B, S, _ = ret.shape
    C = RET_CHUNK
    const = lambda shape: pl.BlockSpec(shape, lambda b, n: (0,) * len(shape))
    return pl.pallas_call(
        _retention_kernel,
        grid=(B, S // C),
        in_specs=[pl.BlockSpec((None, C, 1024), lambda b, n: (b, n, 0)),
                  const((4, C, C)), const((C, 256)), const((C, 256)),
                  const((256, 256)), const((256, 256)), const((256, 256)), const((1, 256))],
        out_specs=pl.BlockSpec((None, C, 256), lambda b, n: (b, n, 0)),
        out_shape=jax.ShapeDtypeStruct((B, S, 256), F32),
        scratch_shapes=[pltpu.VMEM((256, 256), F32)],
        compiler_params=_cparams(("arbitrary", "arbitrary")),
        name="retention",
    )(ret, dec, zeta, xi, gbd, bd, ones256, norm_g)


def _retention_consts():
    H, C = GROUP_WIDTH // HEAD_DIM, RET_CHUNK
    log_gamma = np.log1p(-(2.0 ** (-5.0 - np.arange(H, dtype=np.float64))))
    idx = np.arange(C, dtype=np.float64)
    diff = idx[:, None] - idx[None, :]
    dec = np.where(diff >= 0, np.exp(log_gamma[:, None, None] * np.maximum(diff, 0.0)), 0.0)
    zeta = np.exp(log_gamma[:, None] * (C - 1 - idx)[None, :])
    xi = np.exp(log_gamma[:, None] * (idx + 1)[None, :])
    gamma_chunk = np.exp(log_gamma * C)
    zeta256 = np.repeat(zeta.T, HEAD_DIM, axis=1)
    xi256 = np.repeat(xi.T, HEAD_DIM, axis=1)
    hid = np.arange(GROUP_WIDTH) // HEAD_DIM
    bd = (hid[:, None] == hid[None, :]).astype(np.float64)
    gbd = bd * gamma_chunk[hid][:, None]
    f = lambda a: jnp.asarray(a, dtype=F32)
    return f(dec), f(zeta256), f(xi256), f(gbd), f(bd)


def _sgu_kernel(blk_ref, w_ref, bias_ref, ones_ref, ng_ref, o_ref):
    u = _gelu(blk_ref[:, 0:256])
    v = _gelu(blk_ref[:, 256:512])
    vn = v * _group_rms(v, ones_ref[...]) * ng_ref[...]
    vb = vn.astype(BF16)
    grp = _lane_group(u.shape, HEAD_DIM)
    C = w_ref.shape[-1]
    tril = (lax.broadcasted_iota(jnp.int32, (C, C), 0) >= lax.broadcasted_iota(jnp.int32, (C, C), 1))
    s = bias_ref[...]
    for gi in range(SGU_GROUPS):
        wg = jnp.where(tril, w_ref[gi], 0.0)
        s = s + jnp.where(grp == gi, _dot(wg, vb), 0.0)
    o_ref[...] = u * s


def _sgu_call(sgu, w_s, bias256, ones256, norm_g):
    B, S, _ = sgu.shape
    C = SGU_CHUNK
    const = lambda shape: pl.BlockSpec(shape, lambda b, n: (0,) * len(shape))
    return pl.pallas_call(
        _sgu_kernel,
        grid=(B, S // C),
        in_specs=[pl.BlockSpec((None, C, 512), lambda b, n: (b, n, 0)),
                  const((SGU_GROUPS, C, C)), const((C, 256)), const((256, 256)), const((1, 256))],
        out_specs=pl.BlockSpec((None, C, 256), lambda b, n: (b, n, 0)),
        out_shape=jax.ShapeDtypeStruct((B, S, 256), F32),
        compiler_params=_cparams(("arbitrary", "arbitrary")),
        name="spatial_gating",
    )(sgu, w_s, bias256, ones256, norm_g)


def _s5_prep_kernel(lr_ref, li_ref, ldt_ref, bre_ref, bim_ref, a_ref, bx_ref):
    lr = lr_ref[...]
    li = li_ref[...]
    dts = jnp.exp(ldt_ref[...])
    mag = jnp.exp(lr * dts)
    a_re = mag * jnp.cos(li * dts)
    a_im = mag * jnp.sin(li * dts)
    den = lr * lr + li * li
    f_re = ((a_re - 1.0) * lr + a_im * li) / den
    f_im = (a_im * lr - (a_re - 1.0) * li) / den
    a_ref[0:1, :] = a_re
    a_ref[1:2, :] = a_im
    bre = bre_ref[...]
    bim = bim_ref[...]
    bx_ref[:, 0:S5_WIDTH] = (f_re * bre - f_im * bim).astype(BF16)
    bx_ref[:, S5_WIDTH:2 * S5_WIDTH] = (f_re * bim + f_im * bre).astype(BF16)


def _s5_prep_call(lam_re, lam_im, log_dt, b_re, b_im):
    L = lam_re.shape[0]
    W = S5_WIDTH
    lr = lam_re.reshape(L, 1, W)
    li = lam_im.reshape(L, 1, W)
    ldt = jnp.repeat(log_dt, S5_STATE, axis=-1).reshape(L, 1, W)
    eye = jnp.eye(S5_GROUPS, dtype=F32)

    def dense(b):
        return jnp.einsum('lgph,gk->lghkp', b, eye).reshape(L, S5_GROUPS * S5_CH, W)

    vec = pl.BlockSpec((None, 1, W), lambda l: (l, 0, 0))
    mat = pl.BlockSpec((None, GROUP_WIDTH, W), lambda l: (l, 0, 0))
    return pl.pallas_call(
        _s5_prep_kernel,
        grid=(L,),
        in_specs=[vec, vec, vec, mat, mat],
        out_specs=[pl.BlockSpec((None, 2, W), lambda l: (l, 0, 0)),
                   pl.BlockSpec((None, GROUP_WIDTH, 2 * W), lambda l: (l, 0, 0))],
        out_shape=[jax.ShapeDtypeStruct((L, 2, W), F32),
                   jax.ShapeDtypeStruct((L, GROUP_WIDTH, 2 * W), BF16)],
        compiler_params=_cparams(("arbitrary",)),
        name="s5_discretise",
    )(lr, li, ldt, dense(b_re), dense(b_im))


def _s5_kernel(u_ref, a_ref, bx_ref, cre_ref, cim_ref, d_ref, gw_ref, gb_ref, o_ref,
               hre_ref, him_ref, xre_ref, xim_ref, *, nb, tc):
    @pl.when(pl.program_id(0) == 0)
    def _():
        hre_ref[...] = jnp.zeros_like(hre_ref)
        him_ref[...] = jnp.zeros_like(him_ref)

    u = u_ref[...]
    ub = u.astype(BF16)
    xre_ref[...] = jnp.dot(ub, bx_ref[:, 0:S5_WIDTH], preferred_element_type=F32)
    xim_ref[...] = jnp.dot(ub, bx_ref[:, S5_WIDTH:2 * S5_WIDTH], preferred_element_type=F32)
    a_re = jnp.broadcast_to(a_ref[0:1, :], (nb, S5_WIDTH))
    a_im = jnp.broadcast_to(a_ref[1:2, :], (nb, S5_WIDTH))

    def step(t, carry):
        h_re, h_im = carry
        r0 = pl.multiple_of(t * nb, nb)
        n_re = a_re * h_re - a_im * h_im + xre_ref[pl.ds(r0, nb), :]
        n_im = a_re * h_im + a_im * h_re + xim_ref[pl.ds(r0, nb), :]
        xre_ref[pl.ds(r0, nb), :] = n_re
        xim_ref[pl.ds(r0, nb), :] = n_im
        return n_re, n_im

    h_re, h_im = lax.fori_loop(0, tc, step, (hre_ref[...], him_ref[...]))
    hre_ref[...] = h_re
    him_ref[...] = h_im
    y = _dot(xre_ref[...], cre_ref[...]) - _dot(xim_ref[...], cim_ref[...])
    y = _gelu(y + d_ref[...] * u)
    o_ref[...] = y * _sigmoid(_dot(y, gw_ref[...]) + gb_ref[...])


def _s5_call(u2d, a, bx, cre, cim, d, glu_w, glu_b, nb, tc):
    rows = u2d.shape[0]
    W = S5_WIDTH
    const = lambda shape: pl.BlockSpec(shape, lambda i: (0,) * len(shape))
    return pl.pallas_call(
        functools.partial(_s5_kernel, nb=nb, tc=tc),
        grid=(rows // (tc * nb),),
        in_specs=[pl.BlockSpec((tc * nb, 256), lambda i: (i, 0)),
                  const((2, W)), const((256, 2 * W)), const((W, 256)), const((W, 256)),
                  const((1, 256)), const((256, 256)), const((1, 256))],
        out_specs=pl.BlockSpec((tc * nb, 256), lambda i: (i, 0)),
        out_shape=jax.ShapeDtypeStruct((rows, 256), F32),
        scratch_shapes=[pltpu.VMEM((nb, W), F32), pltpu.VMEM((nb, W), F32),
                        pltpu.VMEM((tc * nb, W), F32), pltpu.VMEM((tc * nb, W), F32)],
        compiler_params=_cparams(("arbitrary",)),
        name="s5_scan",
    )(u2d, a, bx, cre, cim, d, glu_w, glu_b)


def _compress_kernel(hm_ref, pet_ref, peb_ref, w1t_ref, w1b_ref, w2_ref, ones_ref, kg_ref,
                     cos_ref, sin_ref, k_ref, v_ref):
    hm = hm_ref[...]
    a = _dot(hm + pet_ref[...], w1t_ref[...])
    b = _dot(hm + peb_ref[...], w1b_ref[...])
    n = a.shape[0]
    pre = a + pltpu.roll(b, n - 1, 0)
    cmp = _dot(_gelu(pre), w2_ref[...])
    k = cmp[:, 0:128]
    kn = k * _group_rms(k, ones_ref[...]) * kg_ref[...]
    k_ref[...] = _rope(kn, cos_ref[...], sin_ref[...])
    v_ref[...] = cmp[:, 128:256]


def _compress_call(hm, pet, peb, w1t, w1b, w2, ones128, k_gain, cos_c, sin_c):
    B, n, width = hm.shape
    const = lambda shape: pl.BlockSpec(shape, lambda b: (0,) * len(shape))
    out = pl.BlockSpec((None, n, 128), lambda b: (b, 0, 0))
    return pl.pallas_call(
        _compress_kernel,
        grid=(B,),
        in_specs=[pl.BlockSpec((None, n, width), lambda b: (b, 0, 0)),
                  const((1, width)), const((1, width)), const((width, 512)), const((width, 512)),
                  const((512, 256)), const((128, 128)), const((1, 128)),
                  const((n, 128)), const((n, 128))],
        out_specs=[out, out],
        out_shape=[jax.ShapeDtypeStruct((B, n, 128), F32)] * 2,
        compiler_params=_cparams(("arbitrary",)),
        name="nsa_compress",
    )(hm, pet, peb, w1t, w1b, w2, ones128, k_gain, cos_c, sin_c)


def _nsa_kernel(nq_ref, gate_ref, kc_ref, vc_ref, kv_ref, ovt_ref, o_ref,
                m_ref, l_ref, acc_ref, *, n_cmp_rows):
    T = ATT_TILE
    qi = pl.program_id(1)
    t0 = qi * T
    pos_c = t0 + lax.broadcasted_iota(jnp.int32, (T, 1), 0)
    pos_r = t0 + lax.broadcasted_iota(jnp.int32, (1, T), 1)
    kvh_lane = _lane_group((T, 128), HEAD_DIM)
    combos = [(g, kvh) for g in range(2) for kvh in range(2)]
    q_masked = []
    for g, kvh in combos:
        qg = nq_ref[:, g * 128:(g + 1) * 128]
        q_masked.append(jnp.where(kvh_lane == kvh, qg, 0.0).astype(BF16))

    kc = kc_ref[...].astype(BF16)
    vc = vc_ref[...].astype(BF16)
    cmp_end = lax.broadcasted_iota(jnp.int32, (1, n_cmp_rows), 1) * CMP_STRIDE + (CMP_BLOCK - 1)
    valid = cmp_end <= pos_c
    o_cmp = [jnp.zeros((T, 128), F32) for _ in range(2)]
    p_sum = [jnp.zeros((T, n_cmp_rows), F32) for _ in range(2)]
    for ci, (g, kvh) in enumerate(combos):
        s = jnp.where(valid, _dot_nt(q_masked[ci], kc), NEG)
        e = jnp.exp(s - jnp.max(s, axis=-1, keepdims=True))
        p = jnp.where(valid, e / jnp.sum(e, axis=-1, keepdims=True), 0.0)
        o_cmp[g] = o_cmp[g] + jnp.where(kvh_lane == kvh, _dot(p, vc), 0.0)
        p_sum[kvh] = p_sum[kvh] + p

    n_sel = ovt_ref.shape[0]
    blk = lax.broadcasted_iota(jnp.int32, (n_sel, T), 0)
    cur = pos_r // SEL_BLOCK
    forced = (blk == 0) | (blk == cur) | (blk == cur - 1)
    causal = blk <= cur
    ovt = ovt_ref[...]
    sel_rows = []
    for kvh in range(2):
        ph = p_sum[kvh].astype(BF16)
        pl_ = (p_sum[kvh] - ph.astype(F32)).astype(BF16)
        imp = (lax.dot_general(ovt, ph, (((1,), (1,)), ((), ())), preferred_element_type=F32)
               + lax.dot_general(ovt, pl_, (((1,), (1,)), ((), ())), preferred_element_type=F32))
        imp = jnp.where(forced, imp + FORCE_BONUS, imp)
        imp = jnp.where(causal, imp, NEG)
        cnt = jnp.zeros((n_sel, T), F32)
        for j in range(n_sel):
            row = imp[j:j + 1, :]
            ahead = (row > imp) | ((row == imp) & (blk > j))
            cnt = cnt + jnp.where(ahead, 1.0, 0.0)
        sel_t = jnp.where((cnt < SEL_TOPK) & causal, 1.0, 0.0)
        sel_t = jnp.concatenate([sel_t, jnp.zeros((128 - n_sel, T), F32)], axis=0)
        sel_rows.append(sel_t.T.astype(BF16))

    def attend(lo, hi, k_col, v_col, allowed_fn):
        m_ref[...] = jnp.full(m_ref.shape, NEG, F32)
        l_ref[...] = jnp.zeros_like(l_ref)
        acc_ref[...] = jnp.zeros_like(acc_ref)

        def body(j, carry):
            r0 = pl.multiple_of(j * T, T)
            kt = kv_ref[pl.ds(r0, T), k_col:k_col + 128].astype(BF16)
            vt = kv_ref[pl.ds(r0, T), v_col:v_col + 128].astype(BF16)
            kpos = j * T + lax.broadcasted_iota(jnp.int32, (1, T), 1)
            for ci, (g, kvh) in enumerate(combos):
                ok = allowed_fn(j, kpos, kvh)
                s = jnp.where(ok, _dot_nt(q_masked[ci], kt), NEG)
                m_prev = m_ref[ci]
                m_new = jnp.maximum(m_prev, jnp.max(s, axis=-1, keepdims=True))
                p = jnp.where(ok, jnp.exp(s - m_new), 0.0)
                alpha = jnp.exp(m_prev - m_new)
                l_ref[ci] = alpha * l_ref[ci] + jnp.sum(p, axis=-1, keepdims=True)
                acc_ref[ci] = alpha * acc_ref[ci] + _dot(p, vt)
                m_ref[ci] = m_new
            return carry

        lax.fori_loop(lo, hi, body, 0)
        outs = [jnp.zeros((T, 128), F32) for _ in range(2)]
        for ci, (g, kvh) in enumerate(combos):
            outs[g] = outs[g] + jnp.where(kvh_lane == kvh, acc_ref[ci] / l_ref[ci], 0.0)
        return outs

    def sel_allowed(j, kpos, kvh):
        r = lax.broadcasted_iota(jnp.int32, (128, T), 0)
        c = lax.broadcasted_iota(jnp.int32, (128, T), 1)
        expand = jnp.where(r == 2 * j + c // SEL_BLOCK, 1.0, 0.0).astype(BF16)
        picked = jnp.dot(sel_rows[kvh], expand, preferred_element_type=F32)
        return (picked > 0.5) & (kpos <= pos_c)

    def win_allowed(j, kpos, kvh):
        dist = pos_c - kpos
        return (dist >= 0) & (dist < WIN)

    o_sel = attend(0, qi + 1, 0, 128, sel_allowed)
    o_win = attend(jnp.maximum(qi - WIN // T, 0), qi + 1, 256, 384, win_allowed)

    for g in range(2):
        gc = gate_ref[:, (g * 3 + 0) * 128:(g * 3 + 1) * 128]
        gs = gate_ref[:, (g * 3 + 1) * 128:(g * 3 + 2) * 128]
        gw = gate_ref[:, (g * 3 + 2) * 128:(g * 3 + 3) * 128]
        o_ref[:, g * 128:(g + 1) * 128] = gc * o_cmp[g] + gs * o_sel[g] + gw * o_win[g]


def _nsa_call(nq, gates, kcmp, vcmp, kv, ovt):
    B, S, _ = nq.shape
    T = ATT_TILE
    n = kcmp.shape[1]
    n_sel = ovt.shape[0]
    return pl.pallas_call(
        functools.partial(_nsa_kernel, n_cmp_rows=n),
        grid=(B, S // T),
        in_specs=[pl.BlockSpec((None, T, 256), lambda b, i: (b, i, 0)),
                  pl.BlockSpec((None, T, 768), lambda b, i: (b, i, 0)),
                  pl.BlockSpec((None, n, 128), lambda b, i: (b, 0, 0)),
                  pl.BlockSpec((None, n, 128), lambda b, i: (b, 0, 0)),
                  pl.BlockSpec((None, S, 512), lambda b, i: (b, 0, 0)),
                  pl.BlockSpec((n_sel, n), lambda b, i: (0, 0))],
        out_specs=pl.BlockSpec((None, T, 256), lambda b, i: (b, i, 0)),
        out_shape=jax.ShapeDtypeStruct((B, S, 256), F32),
        scratch_shapes=[pltpu.VMEM((4, T, 128), F32), pltpu.VMEM((4, T, 128), F32),
                        pltpu.VMEM((4, T, 128), F32)],
        compiler_params=_cparams(("arbitrary", "arbitrary")),
        name="nsa_attention",
    )(nq, gates, kcmp, vcmp, kv, ovt)


def _outproj_kernel(x_ref, yr_ref, ys_ref, yg_ref, yn_ref, mg_ref, w_ref, g1_ref, o_ref):
    acc = jnp.zeros(x_ref.shape, F32)
    for i, y_ref in enumerate((yr_ref, ys_ref, yg_ref, yn_ref)):
        y = y_ref[...]
        ms = jnp.mean(y * y, axis=-1, keepdims=True)
        yn = y * lax.rsqrt(ms + EPS) * mg_ref[i:i + 1, :]
        acc = acc + jnp.dot(yn.astype(BF16), w_ref[i * 256:(i + 1) * 256, :],
                            preferred_element_type=F32)
    o_ref[...] = x_ref[...] + g1_ref[...] * acc


def _outproj_call(x, y_ret, y_s5, y_sgu, y_nsa, mix_g, w_out, g1, ts):
    B, S, D = x.shape
    row = lambda n: pl.BlockSpec((None, ts, n), lambda b, s: (b, s, 0))
    return pl.pallas_call(
        _outproj_kernel,
        grid=(B, S // ts),
        in_specs=[row(D), row(256), pl.BlockSpec((ts, 256), lambda b, s: (s, b)), row(256), row(256),
                  pl.BlockSpec((4, 256), lambda b, s: (0, 0)),
                  pl.BlockSpec((D, D), lambda b, s: (0, 0)),
                  pl.BlockSpec((None, 1, D), lambda b, s: (b, 0, 0))],
        out_specs=row(D),
        out_shape=jax.ShapeDtypeStruct((B, S, D), F32),
        compiler_params=_cparams(("arbitrary", "arbitrary")),
        name="out_proj",
    )(x, y_ret, y_s5, y_sgu, y_nsa, mix_g, w_out, g1)


def _ffn_kernel(x_ref, g_ref, sc_ref, sh_ref, g2_ref, wup_ref, cw_ref, cb_ref, wdn_ref, o_ref,
                tail_ref, *, d_ff, tf):
    @pl.when(pl.program_id(1) == 0)
    def _():
        tail_ref[...] = jnp.zeros_like(tail_ref)

    x = x_ref[...]
    ts = x.shape[0]
    ms = jnp.mean(x * x, axis=-1, keepdims=True)
    h = x * lax.rsqrt(ms + EPS) * g_ref[...]
    hb = (h * (1.0 + sc_ref[...]) + sh_ref[...]).astype(BF16)
    rowid = lax.broadcasted_iota(jnp.int32, (ts, tf), 0)

    def conv(c0):
        a = jnp.dot(hb, wup_ref[:, c0:c0 + tf], preferred_element_type=F32)
        t6 = tail_ref[6:7, c0:c0 + tf]
        t7 = tail_ref[7:8, c0:c0 + tf]
        a1 = jnp.where(rowid == 0, t7, pltpu.roll(a, 1, 0))
        a2 = jnp.where(rowid == 0, t6, jnp.where(rowid == 1, t7, pltpu.roll(a, 2, 0)))
        tail_ref[:, c0:c0 + tf] = a[ts - 8:ts, :]
        return (cw_ref[2:3, c0:c0 + tf] * a + cw_ref[1:2, c0:c0 + tf] * a1
                + cw_ref[0:1, c0:c0 + tf] * a2 + cb_ref[:, c0:c0 + tf])

    acc = jnp.zeros(x.shape, F32)
    for c in range(d_ff // tf):
        gate = conv(c * tf)
        up = conv(d_ff + c * tf)
        z = (_silu(gate) * up).astype(BF16)
        acc = acc + jnp.dot(z, wdn_ref[c * tf:(c + 1) * tf, :], preferred_element_type=F32)
    o_ref[...] = x + g2_ref[...] * acc


def _ffn_call(x, norm_g, sc, sh, g2, w_up, conv_w, conv_b, w_down, ts, tf):
    B, S, D = x.shape
    d_ff = w_down.shape[0]
    row = pl.BlockSpec((None, ts, D), lambda b, s: (b, s, 0))
    vec = pl.BlockSpec((None, 1, D), lambda b, s: (b, 0, 0))
    const = lambda shape: pl.BlockSpec(shape, lambda b, s: (0,) * len(shape),
                                       pipeline_mode=pl.Buffered(1))
    return pl.pallas_call(
        functools.partial(_ffn_kernel, d_ff=d_ff, tf=tf),
        grid=(B, S // ts),
        in_specs=[row, const((1, D)), vec, vec, vec, const((D, 2 * d_ff)),
                  const((CONV_W, 2 * d_ff)), const((1, 2 * d_ff)), const((d_ff, D))],
        out_specs=row,
        out_shape=jax.ShapeDtypeStruct((B, S, D), F32),
        scratch_shapes=[pltpu.VMEM((8, 2 * d_ff), F32)],
        compiler_params=_cparams(("arbitrary", "arbitrary")),
        name="conv_ffn",
    )(x, norm_g, sc, sh, g2, w_up, conv_w, conv_b, w_down)


def _rope_tables(pos):
    inv = ROPE_THETA ** (-jnp.arange(0, HEAD_DIM, 2, dtype=F32) / HEAD_DIM)
    ang = pos.astype(F32)[:, None] * inv[None, :]
    return jnp.cos(ang), jnp.sin(ang)


def _tile_rope(cos, sin, width):
    reps = width // (HEAD_DIM // 2)
    return jnp.tile(cos, (1, reps)), jnp.tile(sin, (1, reps))


def _block_ones(n):
    hid = np.arange(n) // HEAD_DIM
    return jnp.asarray(hid[:, None] == hid[None, :], dtype=BF16)


def _compress_weights(pe, w1, w2):
    half = CMP_STRIDE * HEAD_DIM
    eye2 = jnp.eye(2, dtype=F32)

    def big(w):
        w = w.reshape(2, CMP_STRIDE, HEAD_DIM, CMP_HIDDEN)
        full = jnp.einsum('ktdj,ka,hb->tkhdabj', w, eye2, eye2)
        return full.reshape(CMP_STRIDE * 4 * HEAD_DIM, 4 * CMP_HIDDEN)

    w1t = big(w1[:, :half])
    w1b = big(w1[:, half:])

    def pe_row(p):
        return jnp.broadcast_to(p[:, :, None, :], (2, CMP_STRIDE, 2, HEAD_DIM)).transpose(
            1, 0, 2, 3).reshape(1, CMP_STRIDE * 4 * HEAD_DIM)

    pet = pe_row(pe[:, :CMP_STRIDE])
    peb = pe_row(pe[:, CMP_STRIDE:])
    w2big = jnp.einsum('kjd,ka,hb->khjabd', w2, eye2, eye2).reshape(4 * CMP_HIDDEN, 4 * HEAD_DIM)
    return pet, peb, w1t.astype(BF16), w1b.astype(BF16), w2big.astype(BF16)


def _nsa_perm():
    idx = np.zeros(GROUP_WIDTH, dtype=np.int32)
    for g in range(2):
        for kvh in range(2):
            for d in range(HEAD_DIM):
                idx[g * 128 + kvh * HEAD_DIM + d] = (kvh * 2 + g) * HEAD_DIM + d
    return idx


def kernel(x, c, norm1_g, norm2_g, ada_w, ada_b, w_in, ret_norm_g, s5_lambda_re, s5_lambda_im, s5_log_dt, s5_b_re, s5_b_im, s5_c_re, s5_c_im, s5_d, s5_glu_w, s5_glu_b, sgu_norm_g, sgu_w, sgu_b, nsa_q_norm_g, nsa_k_norm_g, nsa_cmp_pe, nsa_cmp_w1, nsa_cmp_w2, mix_norm_g, w_out, ffn_w_up, ffn_conv_w, ffn_conv_b, ffn_w_down):
    B, S, D = x.shape
    L = w_in.shape[0]
    ts = min(256, S)
    s5_tc = min(64, S)

    cos, sin = _rope_tables(jnp.arange(S))
    cos256, sin256 = _tile_rope(cos, sin, 256)
    n_cmp_rows = S // CMP_STRIDE
    cos_c, sin_c = _rope_tables(jnp.arange(n_cmp_rows) * CMP_STRIDE + CMP_BLOCK - 1)
    cos_c, sin_c = _tile_rope(cos_c, sin_c, 128)
    ones256 = _block_ones(256)
    ones128 = _block_ones(128)
    dec, zeta256, xi256, gbd, bd = _retention_consts()
    n_sel = S // SEL_BLOCK
    ii = np.arange(n_cmp_rows)[None, :]
    jj = np.arange(n_sel)[:, None]
    overlap_t = ((ii * CMP_STRIDE < (jj + 1) * SEL_BLOCK) & (ii * CMP_STRIDE + CMP_BLOCK > jj * SEL_BLOCK)
                 & (ii < n_cmp_rows - 1))
    ovt = jnp.asarray(overlap_t, dtype=BF16)
    proj_cols = _proj_columns()
    nsa_perm = _nsa_perm()

    mod = _mod_call(c, ada_w, ada_b)
    s5_a, s5_bx = _s5_prep_call(s5_lambda_re, s5_lambda_im, s5_log_dt, s5_b_re, s5_b_im)
    eye_g = jnp.eye(S5_GROUPS, dtype=F32)

    for l in range(L):
        sh1, sc1, g1, sh2, sc2, g2 = [mod[l, :, i * D:(i + 1) * D].reshape(B, 1, D) for i in range(6)]
        w = w_in[l][:, proj_cols].astype(BF16)
        q_gain = jnp.tile(nsa_q_norm_g[l], 4).reshape(1, 256)
        k_gain = jnp.tile(nsa_k_norm_g[l], (1, 2))
        ret, s5u, sgu, nq, kvc, kv, gates = _inproj_call(
            x, norm1_g[l].reshape(1, D), sc1, sh1, w, cos256, sin256, ones256, q_gain, k_gain, ts)

        y_ret = _retention_call(ret, dec, zeta256, xi256, gbd, bd, ones256,
                                ret_norm_g[l].reshape(1, 256))

        cre = jnp.einsum('ghp,gk->gpkh', s5_c_re[l], eye_g).reshape(S5_WIDTH, GROUP_WIDTH)
        cim = jnp.einsum('ghp,gk->gpkh', s5_c_im[l], eye_g).reshape(S5_WIDTH, GROUP_WIDTH)
        y_s5 = _s5_call(s5u.reshape(S * B, 256), s5_a[l], s5_bx[l], cre.astype(BF16), cim.astype(BF16),
                        s5_d[l].reshape(1, 256), s5_glu_w[l].astype(BF16), s5_glu_b[l].reshape(1, 256),
                        B, s5_tc).reshape(S, B * 256)

        bias256 = jnp.repeat(sgu_b[l].T, HEAD_DIM, axis=1)
        y_sgu = _sgu_call(sgu, sgu_w[l], bias256, ones256, sgu_norm_g[l].reshape(1, 256))

        pet, peb, w1t, w1b, w2big = _compress_weights(nsa_cmp_pe[l], nsa_cmp_w1[l], nsa_cmp_w2[l])
        kcmp, vcmp = _compress_call(kvc.reshape(B, n_cmp_rows, CMP_STRIDE * 256), pet, peb, w1t, w1b,
                                    w2big, ones128, k_gain[0:1], cos_c, sin_c)
        y_nsa = _nsa_call(nq, gates, kcmp, vcmp, kv, ovt)

        mix_g = mix_norm_g[l].reshape(4, 256)
        mix_g = mix_g.at[3].set(mix_g[3][nsa_perm])
        wo = w_out[l]
        wo = jnp.concatenate([wo[:768], wo[768:][nsa_perm]], axis=0).astype(BF16)
        x = _outproj_call(x, y_ret, y_s5, y_sgu, y_nsa, mix_g, wo, g1, ts)

        x = _ffn_call(x, norm2_g[l].reshape(1, D), sc2, sh2, g2, ffn_w_up[l].astype(BF16),
                      ffn_conv_w[l], ffn_conv_b[l].reshape(1, -1), ffn_w_down[l].astype(BF16),
                      ts, 256)
    return x
```

```python
import functools
import math

import jax
import jax.numpy as jnp
import numpy as np
from jax import lax
from jax.experimental import pallas as pl
from jax.experimental.pallas import tpu as pltpu

F32 = jnp.float32
BF16 = jnp.bfloat16

HEAD_DIM = 64
GROUP_WIDTH = 256
RET_CHUNK = 128
S5_GROUPS = 16
S5_CH = 16
S5_STATE = 64
S5_WIDTH = S5_GROUPS * S5_STATE
SGU_GROUPS = 4
SGU_CHUNK = 128
NSA_KV_WIDTH = 128
CMP_STRIDE = 16
CMP_BLOCK = 32
CMP_HIDDEN = 128
SEL_BLOCK = 64
SEL_TOPK = 8
FORCE_BONUS = 1e3
WIN = 512
CONV_W = 3
ROPE_THETA = 10000.0
EPS = 1e-6
NEG = -1e9
QK_SCALE = HEAD_DIM ** -0.5

ATT_TILE = 128
VMEM_LIMIT = 56 * 1024 * 1024


def _cparams(sem):
    return pltpu.CompilerParams(dimension_semantics=sem, vmem_limit_bytes=VMEM_LIMIT)


def _dot(a, b):
    return jnp.dot(a.astype(BF16), b.astype(BF16), preferred_element_type=F32)


def _dot_nt(a, b):
    return lax.dot_general(a.astype(BF16), b.astype(BF16), (((1,), (1,)), ((), ())),
                           preferred_element_type=F32)


def _dot_split(a, b_bf16):
    hi = a.astype(BF16)
    lo = (a - hi.astype(F32)).astype(BF16)
    return (jnp.dot(hi, b_bf16, preferred_element_type=F32)
            + jnp.dot(lo, b_bf16, preferred_element_type=F32))


def _sigmoid(x):
    return 1.0 / (1.0 + jnp.exp(-x))


def _silu(x):
    return x * _sigmoid(x)


def _gelu(x):
    c = math.sqrt(2.0 / math.pi)
    return 0.5 * x * (1.0 + jnp.tanh(c * (x + 0.044715 * (x * x * x))))


def _group_rms(x, ones_bd):
    ms = _dot_split(x * x, ones_bd) * (1.0 / HEAD_DIM)
    return lax.rsqrt(ms + EPS)


def _rot_half(x):
    n = x.shape[-1]
    lane = lax.broadcasted_iota(jnp.int32, x.shape, 1)
    first = (lane % HEAD_DIM) < (HEAD_DIM // 2)
    return jnp.where(first, -pltpu.roll(x, n - HEAD_DIM // 2, 1), pltpu.roll(x, HEAD_DIM // 2, 1))


def _rope(x, cos, sin):
    return x * cos + _rot_half(x) * sin


def _lane_group(shape, width):
    return lax.broadcasted_iota(jnp.int32, shape, len(shape) - 1) // width


def _mod_kernel(c_ref, w_ref, b_ref, o_ref):
    o_ref[...] = _dot(_silu(c_ref[...]), w_ref[...]) + b_ref[...]


def _mod_call(c, ada_w, ada_b):
    L, D, D6 = ada_w.shape
    B = c.shape[0]
    nj = D6 // D
    return pl.pallas_call(
        _mod_kernel,
        grid=(L, nj),
        in_specs=[pl.BlockSpec((B, D), lambda l, j: (0, 0)),
                  pl.BlockSpec((None, D, D), lambda l, j: (l, 0, j)),
                  pl.BlockSpec((None, 1, D), lambda l, j: (l, 0, j))],
        out_specs=pl.BlockSpec((None, B, D), lambda l, j: (l, 0, j)),
        out_shape=jax.ShapeDtypeStruct((L, B, D6), F32),
        compiler_params=_cparams(("arbitrary", "arbitrary")),
        name="adaln_mod",
    )(c, ada_w, ada_b.reshape(L, 1, D6))


C_RET = 0
C_S5 = 1024
C_SGU = 1280
C_NQ = 1792
C_KVC = 2048
C_KV = 2304
C_GATE = 2816
N_PROJ = 3584


def _proj_columns():
    cols = list(range(0, 1792))
    nq0 = 1792
    for g in range(2):
        for kvh in range(2):
            base = nq0 + (kvh * 2 + g) * HEAD_DIM
            cols += list(range(base, base + HEAD_DIM))
    cols += list(range(2048, 2816))
    for g in range(2):
        for br in range(3):
            for kvh in range(2):
                cols += [2816 + (kvh * 2 + g) * 3 + br] * HEAD_DIM
    assert len(cols) == N_PROJ
    return np.asarray(cols, dtype=np.int32)


def _inproj_kernel(x_ref, g_ref, sc_ref, sh_ref, w_ref, cos_ref, sin_ref, ones_ref, qg_ref, kg_ref,
                   ret_ref, s5_ref, sgu_ref, nq_ref, kvc_ref, kv_ref, gate_ref):
    x = x_ref[...]
    ms = jnp.mean(x * x, axis=-1, keepdims=True)
    h = x * lax.rsqrt(ms + EPS) * g_ref[...]
    h = h * (1.0 + sc_ref[...]) + sh_ref[...]
    hb = h.astype(BF16)

    def proj(c0, n):
        return jnp.dot(hb, w_ref[:, c0:c0 + n], preferred_element_type=F32)

    cos = cos_ref[...]
    sin = sin_ref[...]
    cos_h, sin_h = cos[:, :128], sin[:, :128]
    ones256 = ones_ref[...]
    ones128 = ones256[:128, :128]

    ret_ref[:, 0:256] = _rope(proj(C_RET, 256), cos, sin) * QK_SCALE
    ret_ref[:, 256:512] = _rope(proj(C_RET + 256, 256), cos, sin)
    ret_ref[:, 512:1024] = proj(C_RET + 512, 512)
    s5_ref[...] = proj(C_S5, 256)
    sgu_ref[...] = proj(C_SGU, 512)
    q = proj(C_NQ, 256)
    qn = q * _group_rms(q, ones256) * qg_ref[...]
    nq_ref[...] = _rope(qn, cos, sin) * QK_SCALE
    kvc_ref[...] = proj(C_KVC, 256)
    kg = kg_ref[...]
    for i in range(2):
        k = proj(C_KV + i * 256, 128)
        kn = k * _group_rms(k, ones128) * kg[i + 1:i + 2, :]
        kv_ref[:, i * 256:i * 256 + 128] = _rope(kn, cos_h, sin_h)
        kv_ref[:, i * 256 + 128:i * 256 + 256] = proj(C_KV + i * 256 + 128, 128)
    gate_ref[...] = _sigmoid(proj(C_GATE, 768))


def _inproj_call(x, norm_g, sc, sh, w, cos, sin, ones256, q_gain, k_gain, ts):
    B, S, D = x.shape
    row = lambda n: pl.BlockSpec((None, ts, n), lambda b, s: (b, s, 0))
    const = lambda shape: pl.BlockSpec(shape, lambda b, s: (0,) * len(shape))
    vec = pl.BlockSpec((None, 1, D), lambda b, s: (b, 0, 0))
    return pl.pallas_call(
        _inproj_kernel,
        grid=(B, S // ts),
        in_specs=[row(D), const((1, D)), vec, vec, const((D, N_PROJ)),
                  pl.BlockSpec((ts, 256), lambda b, s: (s, 0)),
                  pl.BlockSpec((ts, 256), lambda b, s: (s, 0)),
                  const((256, 256)), const((1, 256)), const((3, 128))],
        out_specs=[row(1024), pl.BlockSpec((ts, 256), lambda b, s: (s, b)), row(512), row(256),
                   row(256), row(512), row(768)],
        out_shape=[jax.ShapeDtypeStruct((B, S, 1024), F32),
                   jax.ShapeDtypeStruct((S, B * 256), F32),
                   jax.ShapeDtypeStruct((B, S, 512), F32),
                   jax.ShapeDtypeStruct((B, S, 256), F32),
                   jax.ShapeDtypeStruct((B, S, 256), F32),
                   jax.ShapeDtypeStruct((B, S, 512), F32),
                   jax.ShapeDtypeStruct((B, S, 768), F32)],
        compiler_params=_cparams(("arbitrary", "arbitrary")),
        name="in_proj",
    )(x, norm_g, sc, sh, w, cos, sin, ones256, q_gain, k_gain)


def _retention_kernel(blk_ref, dec_ref, zeta_ref, xi_ref, gbd_ref, bd_ref, ones_ref, ng_ref,
                      o_ref, r_ref):
    @pl.when(pl.program_id(1) == 0)
    def _():
        r_ref[...] = jnp.zeros_like(r_ref)

    q = blk_ref[:, 0:256]
    k = blk_ref[:, 256:512]
    v = blk_ref[:, 512:768]
    g = blk_ref[:, 768:1024]
    kb = k.astype(BF16)
    vb = v.astype(BF16)
    head = _lane_group(q.shape, HEAD_DIM)
    o = jnp.zeros(q.shape, F32)
    for h in range(GROUP_WIDTH // HEAD_DIM):
        qh = jnp.where(head == h, q, 0.0)
        s = _dot_nt(qh, kb) * dec_ref[h]
        o = o + jnp.where(head == h, _dot(s, vb), 0.0)
    r_prev = r_ref[...]
    o = o + _dot(q, r_prev) * xi_ref[...]
    kz_t = (k * zeta_ref[...]).T
    u = _dot(kz_t, vb)
    r_ref[...] = r_prev * gbd_ref[...] + u * bd_ref[...]
    o = o * _group_rms(o, ones_ref[...]) * ng_ref[...]
    o_ref[...] = _silu(g) * o


def _retention_call(ret, dec, zeta, xi, gbd, bd, ones256, norm_g):
    B, S, _ = ret.shape
    C = RET_CHUNK
    const = lambda shape: pl.BlockSpec(shape, lambda b, n: (0,) * len(shape))
    return pl.pallas_call(
        _retention_kernel,
        grid=(B, S // C),
        in_specs=[pl.BlockSpec((None, C, 1024), lambda b, n: (b, n, 0)),
                  const((4, C, C)), const((C, 256)), const((C, 256)),
                  const((256, 256)), const((256, 256)), const((256, 256)), const((1, 256))],
        out_specs=pl.BlockSpec((None, C, 256), lambda b, n: (b, n, 0)),
        out_shape=jax.ShapeDtypeStruct((B, S, 256), F32),
        scratch_shapes=[pltpu.VMEM((256, 256), F32)],
        compiler_params=_cparams(("arbitrary", "arbitrary")),
        name="retention",
    )(ret, dec, zeta, xi, gbd, bd, ones256, norm_g)


def _retention_consts():
    H, C = GROUP_WIDTH // HEAD_DIM, RET_CHUNK
    log_gamma = np.log1p(-(2.0 ** (-5.0 - np.arange(H, dtype=np.float64))))
    idx = np.arange(C, dtype=np.float64)
    diff = idx[:, None] - idx[None, :]
    dec = np.where(diff >= 0, np.exp(log_gamma[:, None, None] * np.maximum(diff, 0.0)), 0.0)
    zeta = np.exp(log_gamma[:, None] * (C - 1 - idx)[None, :])
    xi = np.exp(log_gamma[:, None] * (idx + 1)[None, :])
    gamma_chunk = np.exp(log_gamma * C)
    zeta256 = np.repeat(zeta.T, HEAD_DIM, axis=1)
    xi256 = np.repeat(xi.T, HEAD_DIM, axis=1)
    hid = np.arange(GROUP_WIDTH) // HEAD_DIM
    bd = (hid[:, None] == hid[None, :]).astype(np.float64)
    gbd = bd * gamma_chunk[hid][:, None]
    f = lambda a: jnp.asarray(a, dtype=F32)
    return f(dec), f(zeta256), f(xi256), f(gbd), f(bd)


def _sgu_kernel(blk_ref, w_ref, bias_ref, ones_ref, ng_ref, o_ref):
    u = _gelu(blk_ref[:, 0:256])
    v = _gelu(blk_ref[:, 256:512])
    vn = v * _group_rms(v, ones_ref[...]) * ng_ref[...]
    vb = vn.astype(BF16)
    grp = _lane_group(u.shape, HEAD_DIM)
    C = w_ref.shape[-1]
    tril = (lax.broadcasted_iota(jnp.int32, (C, C), 0) >= lax.broadcasted_iota(jnp.int32, (C, C), 1))
    s = bias_ref[...]
    for gi in range(SGU_GROUPS):
        wg = jnp.where(tril, w_ref[gi], 0.0)
        s = s + jnp.where(grp == gi, _dot(wg, vb), 0.0)
    o_ref[...] = u * s


def _sgu_call(sgu, w_s, bias256, ones256, norm_g):
    B, S, _ = sgu.shape
    C = SGU_CHUNK
    const = lambda shape: pl.BlockSpec(shape, lambda b, n: (0,) * len(shape))
    return pl.pallas_call(
        _sgu_kernel,
        grid=(B, S // C),
        in_specs=[pl.BlockSpec((None, C, 512), lambda b, n: (b, n, 0)),
                  const((SGU_GROUPS, C, C)), const((C, 256)), const((256, 256)), const((1, 256))],
        out_specs=pl.BlockSpec((None, C, 256), lambda b, n: (b, n, 0)),
        out_shape=jax.ShapeDtypeStruct((B, S, 256), F32),
        compiler_params=_cparams(("arbitrary", "arbitrary")),
        name="spatial_gating",
    )(sgu, w_s, bias256, ones256, norm_g)


def _s5_prep_kernel(lr_ref, li_ref, ldt_ref, bre_ref, bim_ref, a_ref, bx_ref):
    lr = lr_ref[...]
    li = li_ref[...]
    dts = jnp.exp(ldt_ref[...])
    mag = jnp.exp(lr * dts)
    a_re = mag * jnp.cos(li * dts)
    a_im = mag * jnp.sin(li * dts)
    den = lr * lr + li * li
    f_re = ((a_re - 1.0) * lr + a_im * li) / den
    f_im = (a_im * lr - (a_re - 1.0) * li) / den
    a_ref[0:1, :] = a_re
    a_ref[1:2, :] = a_im
    bre = bre_ref[...]
    bim = bim_ref[...]
    bx_ref[:, 0:S5_WIDTH] = (f_re * bre - f_im * bim).astype(BF16)
    bx_ref[:, S5_WIDTH:2 * S5_WIDTH] = (f_re * bim + f_im * bre).astype(BF16)


def _s5_prep_call(lam_re, lam_im, log_dt, b_re, b_im):
    L = lam_re.shape[0]
    W = S5_WIDTH
    lr = lam_re.reshape(L, 1, W)
    li = lam_im.reshape(L, 1, W)
    ldt = jnp.repeat(log_dt, S5_STATE, axis=-1).reshape(L, 1, W)
    eye = jnp.eye(S5_GROUPS, dtype=F32)

    def dense(b):
        return jnp.einsum('lgph,gk->lghkp', b, eye).reshape(L, S5_GROUPS * S5_CH, W)

    vec = pl.BlockSpec((None, 1, W), lambda l: (l, 0, 0))
    mat = pl.BlockSpec((None, GROUP_WIDTH, W), lambda l: (l, 0, 0))
    return pl.pallas_call(
        _s5_prep_kernel,
        grid=(L,),
        in_specs=[vec, vec, vec, mat, mat],
        out_specs=[pl.BlockSpec((None, 2, W), lambda l: (l, 0, 0)),
                   pl.BlockSpec((None, GROUP_WIDTH, 2 * W), lambda l: (l, 0, 0))],
        out_shape=[jax.ShapeDtypeStruct((L, 2, W), F32),
                   jax.ShapeDtypeStruct((L, GROUP_WIDTH, 2 * W), BF16)],
        compiler_params=_cparams(("arbitrary",)),
        name="s5_discretise",
    )(lr, li, ldt, dense(b_re), dense(b_im))


def _s5_kernel(u_ref, a_ref, bx_ref, cre_ref, cim_ref, d_ref, gw_ref, gb_ref, o_ref,
               hre_ref, him_ref, xre_ref, xim_ref, *, nb, tc):
    @pl.when(pl.program_id(0) == 0)
    def _():
        hre_ref[...] = jnp.zeros_like(hre_ref)
        him_ref[...] = jnp.zeros_like(him_ref)

    u = u_ref[...]
    ub = u.astype(BF16)
    xre_ref[...] = jnp.dot(ub, bx_ref[:, 0:S5_WIDTH], preferred_element_type=F32)
    xim_ref[...] = jnp.dot(ub, bx_ref[:, S5_WIDTH:2 * S5_WIDTH], preferred_element_type=F32)
    a_re = jnp.broadcast_to(a_ref[0:1, :], (nb, S5_WIDTH))
    a_im = jnp.broadcast_to(a_ref[1:2, :], (nb, S5_WIDTH))

    def step(t, carry):
        h_re, h_im = carry
        r0 = pl.multiple_of(t * nb, nb)
        n_re = a_re * h_re - a_im * h_im + xre_ref[pl.ds(r0, nb), :]
        n_im = a_re * h_im + a_im * h_re + xim_ref[pl.ds(r0, nb), :]
        xre_ref[pl.ds(r0, nb), :] = n_re
        xim_ref[pl.ds(r0, nb), :] = n_im
        return n_re, n_im

    h_re, h_im = lax.fori_loop(0, tc, step, (hre_ref[...], him_ref[...]))
    hre_ref[...] = h_re
    him_ref[...] = h_im
    y = _dot(xre_ref[...], cre_ref[...]) - _dot(xim_ref[...], cim_ref[...])
    y = _gelu(y + d_ref[...] * u)
    o_ref[...] = y * _sigmoid(_dot(y, gw_ref[...]) + gb_ref[...])


def _s5_call(u2d, a, bx, cre, cim, d, glu_w, glu_b, nb, tc):
    rows = u2d.shape[0]
    W = S5_WIDTH
    const = lambda shape: pl.BlockSpec(shape, lambda i: (0,) * len(shape))
    return pl.pallas_call(
        functools.partial(_s5_kernel, nb=nb, tc=tc),
        grid=(rows // (tc * nb),),
        in_specs=[pl.BlockSpec((tc * nb, 256), lambda i: (i, 0)),
                  const((2, W)), const((256, 2 * W)), const((W, 256)), const((W, 256)),
                  const((1, 256)), const((256, 256)), const((1, 256))],
        out_specs=pl.BlockSpec((tc * nb, 256), lambda i: (i, 0)),
        out_shape=jax.ShapeDtypeStruct((rows, 256), F32),
        scratch_shapes=[pltpu.VMEM((nb, W), F32), pltpu.VMEM((nb, W), F32),
                        pltpu.VMEM((tc * nb, W), F32), pltpu.VMEM((tc * nb, W), F32)],
        compiler_params=_cparams(("arbitrary",)),
        name="s5_scan",
    )(u2d, a, bx, cre, cim, d, glu_w, glu_b)


def _compress_kernel(hm_ref, pet_ref, peb_ref, w1t_ref, w1b_ref, w2_ref, ones_ref, kg_ref,
                     cos_ref, sin_ref, k_ref, v_ref):
    hm = hm_ref[...]
    a = _dot(hm + pet_ref[...], w1t_ref[...])
    b = _dot(hm + peb_ref[...], w1b_ref[...])
    n = a.shape[0]
    pre = a + pltpu.roll(b, n - 1, 0)
    cmp = _dot(_gelu(pre), w2_ref[...])
    k = cmp[:, 0:128]
    kn = k * _group_rms(k, ones_ref[...]) * kg_ref[...]
    k_ref[...] = _rope(kn, cos_ref[...], sin_ref[...])
    v_ref[...] = cmp[:, 128:256]


def _compress_call(hm, pet, peb, w1t, w1b, w2, ones128, k_gain, cos_c, sin_c):
    B, n, width = hm.shape
    const = lambda shape: pl.BlockSpec(shape, lambda b: (0,) * len(shape))
    out = pl.BlockSpec((None, n, 128), lambda b: (b, 0, 0))
    return pl.pallas_call(
        _compress_kernel,
        grid=(B,),
        in_specs=[pl.BlockSpec((None, n, width), lambda b: (b, 0, 0)),
                  const((1, width)), const((1, width)), const((width, 512)), const((width, 512)),
                  const((512, 256)), const((128, 128)), const((1, 128)),
                  const((n, 128)), const((n, 128))],
        out_specs=[out, out],
        out_shape=[jax.ShapeDtypeStruct((B, n, 128), F32)] * 2,
        compiler_params=_cparams(("arbitrary",)),
        name="nsa_compress",
    )(hm, pet, peb, w1t, w1b, w2, ones128, k_gain, cos_c, sin_c)


def _nsa_kernel(nq_ref, gate_ref, kc_ref, vc_ref, kv_ref, ovt_ref, o_ref,
                s_ref, m_ref, l_ref, acc_ref, *, n_cmp_rows, win_keys):
    T = ATT_TILE
    qi = pl.program_id(1)
    t0 = qi * T
    pos_c = t0 + lax.broadcasted_iota(jnp.int32, (T, 1), 0)
    pos_r = t0 + lax.broadcasted_iota(jnp.int32, (1, T), 1)
    kvh_lane = _lane_group((T, 128), HEAD_DIM)
    combos = [(g, kvh) for g in range(2) for kvh in range(2)]
    rows = lambda ci: slice(ci * T, (ci + 1) * T)
    q_all = jnp.concatenate(
        [jnp.where(kvh_lane == kvh, nq_ref[:, g * 128:(g + 1) * 128], 0.0).astype(BF16)
         for g, kvh in combos], axis=0)

    def head_lanes(x):
        return [jnp.where(kvh_lane == 0, x[rows(2 * g)], x[rows(2 * g + 1)]) for g in range(2)]

    cmp_end = lax.broadcasted_iota(jnp.int32, (1, n_cmp_rows), 1) * CMP_STRIDE + (CMP_BLOCK - 1)
    valid = cmp_end <= pos_c
    s_all = _dot_nt(q_all, kc_ref[...])
    p_list = []
    for ci in range(4):
        s = jnp.where(valid, s_all[rows(ci)], NEG)
        e = jnp.exp(s - jnp.max(s, axis=-1, keepdims=True))
        inv = 1.0 / jnp.sum(e, axis=-1, keepdims=True)
        p_list.append(jnp.where(valid, e * inv, 0.0))
    o_cmp = head_lanes(_dot(jnp.concatenate(p_list, axis=0), vc_ref[...]))

    n_sel = ovt_ref.shape[0]
    blk = lax.broadcasted_iota(jnp.int32, (n_sel, T), 0)
    cur = pos_r // SEL_BLOCK
    forced = (blk == 0) | (blk == cur) | (blk == cur - 1)
    causal_blk = blk <= cur
    ovt = ovt_ref[...]
    sel_rows = []
    for kvh in range(2):
        p_kvh = p_list[kvh] + p_list[2 + kvh]
        ph = p_kvh.astype(BF16)
        plo = (p_kvh - ph.astype(F32)).astype(BF16)
        imp = (lax.dot_general(ovt, ph, (((1,), (1,)), ((), ())), preferred_element_type=F32)
               + lax.dot_general(ovt, plo, (((1,), (1,)), ((), ())), preferred_element_type=F32))
        imp = jnp.where(forced, imp + FORCE_BONUS, imp)
        imp = jnp.where(causal_blk, imp, NEG)
        cnt = jnp.zeros((n_sel, T), F32)
        for j in range(n_sel):
            row = imp[j:j + 1, :]
            ahead = (row > imp) | ((row == imp) & (blk > j))
            cnt = cnt + jnp.where(ahead, 1.0, 0.0)
        sel_t = jnp.where((cnt < SEL_TOPK) & causal_blk, 1.0, 0.0)
        sel_t = jnp.concatenate([sel_t, jnp.zeros((128 - n_sel, T), F32)], axis=0)
        sel_rows.append(sel_t.T.astype(BF16))

    tri = (lax.broadcasted_iota(jnp.int32, (T, T), 0) >= lax.broadcasted_iota(jnp.int32, (T, T), 1))
    blk_r = lax.broadcasted_iota(jnp.int32, (128, T), 0)
    blk_c = lax.broadcasted_iota(jnp.int32, (128, T), 1) // SEL_BLOCK
    m_ref[...] = jnp.full(m_ref.shape, NEG, F32)

    def scores(j, diagonal):
        r0 = pl.multiple_of(j * T, T)
        s = _dot_nt(q_all, kv_ref[pl.ds(r0, T), 0:128])
        expand = jnp.where(blk_r == 2 * j + blk_c, 1.0, 0.0).astype(BF16)
        for kvh in range(2):
            ok = jnp.dot(sel_rows[kvh], expand, preferred_element_type=F32) > 0.5
            if diagonal:
                ok = ok & tri
            for g in range(2):
                ci = 2 * g + kvh
                sm = jnp.where(ok, s[rows(ci)], NEG)
                s_ref[j, rows(ci), :] = sm
                m_ref[rows(ci), :] = jnp.maximum(m_ref[rows(ci), :], sm)

    def pass1(j, carry):
        scores(j, False)
        return carry

    lax.fori_loop(0, qi, pass1, 0)
    scores(qi, True)
    m_ref[...] = jnp.broadcast_to(jnp.max(m_ref[...], axis=-1, keepdims=True), m_ref.shape)
    l_ref[...] = jnp.zeros_like(l_ref)
    acc_ref[...] = jnp.zeros_like(acc_ref)

    def pass2(j, carry):
        r0 = pl.multiple_of(j * T, T)
        p = jnp.exp(s_ref[j] - m_ref[...])
        l_ref[...] += p
        acc_ref[...] += _dot(p, kv_ref[pl.ds(r0, T), 128:256])
        return carry

    lax.fori_loop(0, qi + 1, pass2, 0)
    inv_l = 1.0 / jnp.sum(l_ref[...], axis=-1, keepdims=True)
    o_sel = head_lanes(acc_ref[...] * inv_l)

    w0 = pl.multiple_of(jnp.maximum(t0 + T - win_keys, 0), T)
    dist = pos_c - (w0 + lax.broadcasted_iota(jnp.int32, (1, win_keys), 1))
    ok_w = (dist >= 0) & (dist < WIN)
    s_w = _dot_nt(q_all, kv_ref[pl.ds(w0, win_keys), 256:384])
    vw = kv_ref[pl.ds(w0, win_keys), 384:512].astype(BF16)
    o_w = []
    for ci in range(4):
        s = jnp.where(ok_w, s_w[rows(ci)], NEG)
        e = jnp.exp(s - jnp.max(s, axis=-1, keepdims=True))
        inv = 1.0 / jnp.sum(e, axis=-1, keepdims=True)
        o_w.append(_dot(e, vw) * inv)
    o_win = head_lanes(jnp.concatenate(o_w, axis=0))

    for g in range(2):
        gc = gate_ref[:, (g * 3 + 0) * 128:(g * 3 + 1) * 128]
        gs = gate_ref[:, (g * 3 + 1) * 128:(g * 3 + 2) * 128]
        gw = gate_ref[:, (g * 3 + 2) * 128:(g * 3 + 3) * 128]
        o_ref[:, g * 128:(g + 1) * 128] = gc * o_cmp[g] + gs * o_sel[g] + gw * o_win[g]


def _nsa_call(nq, gates, kcmp, vcmp, kv, ovt):
    B, S, _ = nq.shape
    T = ATT_TILE
    n = kcmp.shape[1]
    n_sel = ovt.shape[0]
    win_keys = min(WIN + T, S)
    return pl.pallas_call(
        functools.partial(_nsa_kernel, n_cmp_rows=n, win_keys=win_keys),
        grid=(B, S // T),
        in_specs=[pl.BlockSpec((None, T, 256), lambda b, i: (b, i, 0)),
                  pl.BlockSpec((None, T, 768), lambda b, i: (b, i, 0)),
                  pl.BlockSpec((None, n, 128), lambda b, i: (b, 0, 0)),
                  pl.BlockSpec((None, n, 128), lambda b, i: (b, 0, 0)),
                  pl.BlockSpec((None, S, 512), lambda b, i: (b, 0, 0)),
                  pl.BlockSpec((n_sel, n), lambda b, i: (0, 0))],
        out_specs=pl.BlockSpec((None, T, 256), lambda b, i: (b, i, 0)),
        out_shape=jax.ShapeDtypeStruct((B, S, 256), F32),
        scratch_shapes=[pltpu.VMEM((S // T, 4 * T, T), F32), pltpu.VMEM((4 * T, T), F32),
                        pltpu.VMEM((4 * T, T), F32), pltpu.VMEM((4 * T, T), F32)],
        compiler_params=_cparams(("arbitrary", "arbitrary")),
        name="nsa_attention",
    )(nq, gates, kcmp, vcmp, kv, ovt)


def _outproj_kernel(x_ref, yr_ref, ys_ref, yg_ref, yn_ref, mg_ref, w_ref, g1_ref, o_ref):
    acc = jnp.zeros(x_ref.shape, F32)
    for i, y_ref in enumerate((yr_ref, ys_ref, yg_ref, yn_ref)):
        y = y_ref[...]
        ms = jnp.mean(y * y, axis=-1, keepdims=True)
        yn = y * lax.rsqrt(ms + EPS) * mg_ref[i:i + 1, :]
        acc = acc + jnp.dot(yn.astype(BF16), w_ref[i * 256:(i + 1) * 256, :],
                            preferred_element_type=F32)
    o_ref[...] = x_ref[...] + g1_ref[...] * acc


def _outproj_call(x, y_ret, y_s5, y_sgu, y_nsa, mix_g, w_out, g1, ts):
    B, S, D = x.shape
    row = lambda n: pl.BlockSpec((None, ts, n), lambda b, s: (b, s, 0))
    return pl.pallas_call(
        _outproj_kernel,
        grid=(B, S // ts),
        in_specs=[row(D), row(256), pl.BlockSpec((ts, 256), lambda b, s: (s, b)), row(256), row(256),
                  pl.BlockSpec((4, 256), lambda b, s: (0, 0)),
                  pl.BlockSpec((D, D), lambda b, s: (0, 0)),
                  pl.BlockSpec((None, 1, D), lambda b, s: (b, 0, 0))],
        out_specs=row(D),
        out_shape=jax.ShapeDtypeStruct((B, S, D), F32),
        compiler_params=_cparams(("arbitrary", "arbitrary")),
        name="out_proj",
    )(x, y_ret, y_s5, y_sgu, y_nsa, mix_g, w_out, g1)


def _ffn_kernel(x_ref, g_ref, sc_ref, sh_ref, g2_ref, wup_ref, cw_ref, cb_ref, wdn_ref, o_ref,
                tail_ref, *, d_ff, tf):
    @pl.when(pl.program_id(1) == 0)
    def _():
        tail_ref[...] = jnp.zeros_like(tail_ref)

    x = x_ref[...]
    ts = x.shape[0]
    ms = jnp.mean(x * x, axis=-1, keepdims=True)
    h = x * lax.rsqrt(ms + EPS) * g_ref[...]
    hb = (h * (1.0 + sc_ref[...]) + sh_ref[...]).astype(BF16)
    rowid = lax.broadcasted_iota(jnp.int32, (ts, tf), 0)

    def conv(c0):
        a = jnp.dot(hb, wup_ref[:, c0:c0 + tf], preferred_element_type=F32)
        t6 = tail_ref[6:7, c0:c0 + tf]
        t7 = tail_ref[7:8, c0:c0 + tf]
        a1 = jnp.where(rowid == 0, t7, pltpu.roll(a, 1, 0))
        a2 = jnp.where(rowid == 0, t6, jnp.where(rowid == 1, t7, pltpu.roll(a, 2, 0)))
        tail_ref[:, c0:c0 + tf] = a[ts - 8:ts, :]
        return (cw_ref[2:3, c0:c0 + tf] * a + cw_ref[1:2, c0:c0 + tf] * a1
                + cw_ref[0:1, c0:c0 + tf] * a2 + cb_ref[:, c0:c0 + tf])

    acc = jnp.zeros(x.shape, F32)
    for c in range(d_ff // tf):
        gate = conv(c * tf)
        up = conv(d_ff + c * tf)
        z = (_silu(gate) * up).astype(BF16)
        acc = acc + jnp.dot(z, wdn_ref[c * tf:(c + 1) * tf, :], preferred_element_type=F32)
    o_ref[...] = x + g2_ref[...] * acc


def _ffn_call(x, norm_g, sc, sh, g2, w_up, conv_w, conv_b, w_down, ts, tf):
    B, S, D = x.shape
    d_ff = w_down.shape[0]
    row = pl.BlockSpec((None, ts, D), lambda b, s: (b, s, 0))
    vec = pl.BlockSpec((None, 1, D), lambda b, s: (b, 0, 0))
    const = lambda shape: pl.BlockSpec(shape, lambda b, s: (0,) * len(shape),
                                       pipeline_mode=pl.Buffered(1))
    return pl.pallas_call(
        functools.partial(_ffn_kernel, d_ff=d_ff, tf=tf),
        grid=(B, S // ts),
        in_specs=[row, const((1, D)), vec, vec, vec, const((D, 2 * d_ff)),
                  const((CONV_W, 2 * d_ff)), const((1, 2 * d_ff)), const((d_ff, D))],
        out_specs=row,
        out_shape=jax.ShapeDtypeStruct((B, S, D), F32),
        scratch_shapes=[pltpu.VMEM((8, 2 * d_ff), F32)],
        compiler_params=_cparams(("arbitrary", "arbitrary")),
        name="conv_ffn",
    )(x, norm_g, sc, sh, g2, w_up, conv_w, conv_b, w_down)


def _rope_tables(pos):
    inv = ROPE_THETA ** (-jnp.arange(0, HEAD_DIM, 2, dtype=F32) / HEAD_DIM)
    ang = pos.astype(F32)[:, None] * inv[None, :]
    return jnp.cos(ang), jnp.sin(ang)


def _tile_rope(cos, sin, width):
    reps = width // (HEAD_DIM // 2)
    return jnp.tile(cos, (1, reps)), jnp.tile(sin, (1, reps))


def _block_ones(n):
    hid = np.arange(n) // HEAD_DIM
    return jnp.asarray(hid[:, None] == hid[None, :], dtype=BF16)


def _compress_weights(pe, w1, w2):
    half = CMP_STRIDE * HEAD_DIM
    eye2 = jnp.eye(2, dtype=F32)

    def big(w):
        w = w.reshape(2, CMP_STRIDE, HEAD_DIM, CMP_HIDDEN)
        full = jnp.einsum('ktdj,ka,hb->tkhdabj', w, eye2, eye2)
        return full.reshape(CMP_STRIDE * 4 * HEAD_DIM, 4 * CMP_HIDDEN)

    w1t = big(w1[:, :half])
    w1b = big(w1[:, half:])

    def pe_row(p):
        return jnp.broadcast_to(p[:, :, None, :], (2, CMP_STRIDE, 2, HEAD_DIM)).transpose(
            1, 0, 2, 3).reshape(1, CMP_STRIDE * 4 * HEAD_DIM)

    pet = pe_row(pe[:, :CMP_STRIDE])
    peb = pe_row(pe[:, CMP_STRIDE:])
    w2big = jnp.einsum('kjd,ka,hb->khjabd', w2, eye2, eye2).reshape(4 * CMP_HIDDEN, 4 * HEAD_DIM)
    return pet, peb, w1t.astype(BF16), w1b.astype(BF16), w2big.astype(BF16)


def _nsa_perm():
    idx = np.zeros(GROUP_WIDTH, dtype=np.int32)
    for g in range(2):
        for kvh in range(2):
            for d in range(HEAD_DIM):
                idx[g * 128 + kvh * HEAD_DIM + d] = (kvh * 2 + g) * HEAD_DIM + d
    return idx


def kernel(x, c, norm1_g, norm2_g, ada_w, ada_b, w_in, ret_norm_g, s5_lambda_re, s5_lambda_im, s5_log_dt, s5_b_re, s5_b_im, s5_c_re, s5_c_im, s5_d, s5_glu_w, s5_glu_b, sgu_norm_g, sgu_w, sgu_b, nsa_q_norm_g, nsa_k_norm_g, nsa_cmp_pe, nsa_cmp_w1, nsa_cmp_w2, mix_norm_g, w_out, ffn_w_up, ffn_conv_w, ffn_conv_b, ffn_w_down):
    B, S, D = x.shape
    L = w_in.shape[0]
    ts = min(256, S)
    s5_tc = min(64, S)

    cos, sin = _rope_tables(jnp.arange(S))
    cos256, sin256 = _tile_rope(cos, sin, 256)
    n_cmp_rows = S // CMP_STRIDE
    cos_c, sin_c = _rope_tables(jnp.arange(n_cmp_rows) * CMP_STRIDE + CMP_BLOCK - 1)
    cos_c, sin_c = _tile_rope(cos_c, sin_c, 128)
    ones256 = _block_ones(256)
    ones128 = _block_ones(128)
    dec, zeta256, xi256, gbd, bd = _retention_consts()
    n_sel = S // SEL_BLOCK
    ii = np.arange(n_cmp_rows)[None, :]
    jj = np.arange(n_sel)[:, None]
    overlap_t = ((ii * CMP_STRIDE < (jj + 1) * SEL_BLOCK) & (ii * CMP_STRIDE + CMP_BLOCK > jj * SEL_BLOCK)
                 & (ii < n_cmp_rows - 1))
    ovt = jnp.asarray(overlap_t, dtype=BF16)
    proj_cols = _proj_columns()
    nsa_perm = _nsa_perm()

    mod = _mod_call(c, ada_w, ada_b)
    s5_a, s5_bx = _s5_prep_call(s5_lambda_re, s5_lambda_im, s5_log_dt, s5_b_re, s5_b_im)
    eye_g = jnp.eye(S5_GROUPS, dtype=F32)

    for l in range(L):
        sh1, sc1, g1, sh2, sc2, g2 = [mod[l, :, i * D:(i + 1) * D].reshape(B, 1, D) for i in range(6)]
        w = w_in[l][:, proj_cols].astype(BF16)
        q_gain = jnp.tile(nsa_q_norm_g[l], 4).reshape(1, 256)
        k_gain = jnp.tile(nsa_k_norm_g[l], (1, 2))
        ret, s5u, sgu, nq, kvc, kv, gates = _inproj_call(
            x, norm1_g[l].reshape(1, D), sc1, sh1, w, cos256, sin256, ones256, q_gain, k_gain, ts)

        y_ret = _retention_call(ret, dec, zeta256, xi256, gbd, bd, ones256,
                                ret_norm_g[l].reshape(1, 256))

        cre = jnp.einsum('ghp,gk->gpkh', s5_c_re[l], eye_g).reshape(S5_WIDTH, GROUP_WIDTH)
        cim = jnp.einsum('ghp,gk->gpkh', s5_c_im[l], eye_g).reshape(S5_WIDTH, GROUP_WIDTH)
        y_s5 = _s5_call(s5u.reshape(S * B, 256), s5_a[l], s5_bx[l], cre.astype(BF16), cim.astype(BF16),
                        s5_d[l].reshape(1, 256), s5_glu_w[l].astype(BF16), s5_glu_b[l].reshape(1, 256),
                        B, s5_tc).reshape(S, B * 256)

        bias256 = jnp.repeat(sgu_b[l].T, HEAD_DIM, axis=1)
        y_sgu = _sgu_call(sgu, sgu_w[l], bias256, ones256, sgu_norm_g[l].reshape(1, 256))

        pet, peb, w1t, w1b, w2big = _compress_weights(nsa_cmp_pe[l], nsa_cmp_w1[l], nsa_cmp_w2[l])
        kcmp, vcmp = _compress_call(kvc.reshape(B, n_cmp_rows, CMP_STRIDE * 256), pet, peb, w1t, w1b,
                                    w2big, ones128, k_gain[0:1], cos_c, sin_c)
        y_nsa = _nsa_call(nq, gates, kcmp, vcmp, kv, ovt)

        mix_g = mix_norm_g[l].reshape(4, 256)
        mix_g = mix_g.at[3].set(mix_g[3][nsa_perm])
        wo = w_out[l]
        wo = jnp.concatenate([wo[:768], wo[768:][nsa_perm]], axis=0).astype(BF16)
        x = _outproj_call(x, y_ret, y_s5, y_sgu, y_nsa, mix_g, wo, g1, ts)

        x = _ffn_call(x, norm2_g[l].reshape(1, D), sc2, sh2, g2, ffn_w_up[l].astype(BF16),
                      ffn_conv_w[l], ffn_conv_b[l].reshape(1, -1), ffn_w_down[l].astype(BF16),
                      ts, 256)
    return x
```

```python
import functools
import math

import jax
import jax.numpy as jnp
import numpy as np
from jax import lax
from jax.experimental import pallas as pl
from jax.experimental.pallas import tpu as pltpu

F32 = jnp.float32
BF16 = jnp.bfloat16

HEAD_DIM = 64
GROUP_WIDTH = 256
RET_CHUNK = 128
S5_GROUPS = 16
S5_CH = 16
S5_STATE = 64
S5_WIDTH = S5_GROUPS * S5_STATE
SGU_GROUPS = 4
SGU_CHUNK = 128
NSA_KV_WIDTH = 128
CMP_STRIDE = 16
CMP_BLOCK = 32
CMP_HIDDEN = 128
SEL_BLOCK = 64
SEL_TOPK = 8
FORCE_BONUS = 1e3
WIN = 512
CONV_W = 3
ROPE_THETA = 10000.0
EPS = 1e-6
NEG = -1e9
QK_SCALE = HEAD_DIM ** -0.5

ATT_TILE = 256
VMEM_LIMIT = 56 * 1024 * 1024


def _cparams(sem):
    return pltpu.CompilerParams(dimension_semantics=sem, vmem_limit_bytes=VMEM_LIMIT)


def _dot(a, b):
    return jnp.dot(a.astype(BF16), b.astype(BF16), preferred_element_type=F32)


def _dot_nt(a, b):
    return lax.dot_general(a.astype(BF16), b.astype(BF16), (((1,), (1,)), ((), ())),
                           preferred_element_type=F32)


def _dot_split(a, b_bf16):
    hi = a.astype(BF16)
    lo = (a - hi.astype(F32)).astype(BF16)
    return (jnp.dot(hi, b_bf16, preferred_element_type=F32)
            + jnp.dot(lo, b_bf16, preferred_element_type=F32))


def _sigmoid(x):
    return 1.0 / (1.0 + jnp.exp(-x))


def _silu(x):
    return x * _sigmoid(x)


def _gelu(x):
    c = math.sqrt(2.0 / math.pi)
    return 0.5 * x * (1.0 + jnp.tanh(c * (x + 0.044715 * (x * x * x))))


def _group_rms(x, ones_bd):
    ms = _dot_split(x * x, ones_bd) * (1.0 / HEAD_DIM)
    return lax.rsqrt(ms + EPS)


def _rot_half(x):
    n = x.shape[-1]
    lane = lax.broadcasted_iota(jnp.int32, x.shape, 1)
    first = (lane % HEAD_DIM) < (HEAD_DIM // 2)
    return jnp.where(first, -pltpu.roll(x, n - HEAD_DIM // 2, 1), pltpu.roll(x, HEAD_DIM // 2, 1))


def _rope(x, cos, sin):
    return x * cos + _rot_half(x) * sin


def _lane_group(shape, width):
    return lax.broadcasted_iota(jnp.int32, shape, len(shape) - 1) // width


def _mod_kernel(c_ref, w_ref, b_ref, o_ref):
    o_ref[...] = _dot(_silu(c_ref[...]), w_ref[...]) + b_ref[...]


def _mod_call(c, ada_w, ada_b):
    L, D, D6 = ada_w.shape
    B = c.shape[0]
    nj = D6 // D
    return pl.pallas_call(
        _mod_kernel,
        grid=(L, nj),
        in_specs=[pl.BlockSpec((B, D), lambda l, j: (0, 0)),
                  pl.BlockSpec((None, D, D), lambda l, j: (l, 0, j)),
                  pl.BlockSpec((None, 1, D), lambda l, j: (l, 0, j))],
        out_specs=pl.BlockSpec((None, B, D), lambda l, j: (l, 0, j)),
        out_shape=jax.ShapeDtypeStruct((L, B, D6), F32),
        compiler_params=_cparams(("arbitrary", "arbitrary")),
        name="adaln_mod",
    )(c, ada_w, ada_b.reshape(L, 1, D6))


C_RET = 0
C_S5 = 1024
C_SGU = 1280
C_NQ = 1792
C_KVC = 2048
C_KV = 2304
C_GATE = 2816
N_PROJ = 3584


def _proj_columns():
    cols = list(range(0, 1792))
    nq0 = 1792
    for g in range(2):
        for kvh in range(2):
            base = nq0 + (kvh * 2 + g) * HEAD_DIM
            cols += list(range(base, base + HEAD_DIM))
    cols += list(range(2048, 2816))
    for g in range(2):
        for br in range(3):
            for kvh in range(2):
                cols += [2816 + (kvh * 2 + g) * 3 + br] * HEAD_DIM
    assert len(cols) == N_PROJ
    return np.asarray(cols, dtype=np.int32)


def _inproj_kernel(x_ref, g_ref, sc_ref, sh_ref, w_ref, cos_ref, sin_ref, ones_ref, qg_ref, kg_ref,
                   ret_ref, s5_ref, sgu_ref, nq_ref, kvc_ref, kv_ref, gate_ref):
    x = x_ref[...]
    ms = jnp.mean(x * x, axis=-1, keepdims=True)
    h = x * lax.rsqrt(ms + EPS) * g_ref[...]
    h = h * (1.0 + sc_ref[...]) + sh_ref[...]
    hb = h.astype(BF16)

    def proj(c0, n):
        return jnp.dot(hb, w_ref[:, c0:c0 + n], preferred_element_type=F32)

    cos = cos_ref[...]
    sin = sin_ref[...]
    cos_h, sin_h = cos[:, :128], sin[:, :128]
    ones256 = ones_ref[...]
    ones128 = ones256[:128, :128]

    ret_ref[:, 0:256] = _rope(proj(C_RET, 256), cos, sin) * QK_SCALE
    ret_ref[:, 256:512] = _rope(proj(C_RET + 256, 256), cos, sin)
    ret_ref[:, 512:1024] = proj(C_RET + 512, 512)
    s5_ref[...] = proj(C_S5, 256)
    sgu_ref[...] = proj(C_SGU, 512)
    q = proj(C_NQ, 256)
    qn = q * _group_rms(q, ones256) * qg_ref[...]
    nq_ref[...] = _rope(qn, cos, sin) * QK_SCALE
    kvc_ref[...] = proj(C_KVC, 256)
    kg = kg_ref[...]
    for i in range(2):
        k = proj(C_KV + i * 256, 128)
        kn = k * _group_rms(k, ones128) * kg[i + 1:i + 2, :]
        kv_ref[:, i * 256:i * 256 + 128] = _rope(kn, cos_h, sin_h)
        kv_ref[:, i * 256 + 128:i * 256 + 256] = proj(C_KV + i * 256 + 128, 128)
    gate_ref[...] = _sigmoid(proj(C_GATE, 768))


def _inproj_call(x, norm_g, sc, sh, w, cos, sin, ones256, q_gain, k_gain, ts):
    B, S, D = x.shape
    row = lambda n: pl.BlockSpec((None, ts, n), lambda b, s: (b, s, 0))
    const = lambda shape: pl.BlockSpec(shape, lambda b, s: (0,) * len(shape),
                                       pipeline_mode=pl.Buffered(1))
    vec = pl.BlockSpec((None, 1, D), lambda b, s: (b, 0, 0))
    return pl.pallas_call(
        _inproj_kernel,
        grid=(B, S // ts),
        in_specs=[row(D), const((1, D)), vec, vec, const((D, N_PROJ)),
                  pl.BlockSpec((ts, 256), lambda b, s: (s, 0)),
                  pl.BlockSpec((ts, 256), lambda b, s: (s, 0)),
                  const((256, 256)), const((1, 256)), const((3, 128))],
        out_specs=[row(1024), pl.BlockSpec((ts, 256), lambda b, s: (s, b)), row(512), row(256),
                   row(256), row(512), row(768)],
        out_shape=[jax.ShapeDtypeStruct((B, S, 1024), F32),
                   jax.ShapeDtypeStruct((S, B * 256), F32),
                   jax.ShapeDtypeStruct((B, S, 512), F32),
                   jax.ShapeDtypeStruct((B, S, 256), F32),
                   jax.ShapeDtypeStruct((B, S, 256), F32),
                   jax.ShapeDtypeStruct((B, S, 512), F32),
                   jax.ShapeDtypeStruct((B, S, 768), F32)],
        compiler_params=_cparams(("arbitrary", "arbitrary")),
        name="in_proj",
    )(x, norm_g, sc, sh, w, cos, sin, ones256, q_gain, k_gain)


def _retention_kernel(blk_ref, dec_ref, zeta_ref, xi_ref, gbd_ref, bd_ref, ones_ref, ng_ref,
                      o_ref, r_ref):
    @pl.when(pl.program_id(1) == 0)
    def _():
        r_ref[...] = jnp.zeros_like(r_ref)

    C = RET_CHUNK
    head = _lane_group((C, GROUP_WIDTH), HEAD_DIM)
    for ci in range(blk_ref.shape[0] // C):
        rows = slice(ci * C, (ci + 1) * C)
        q = blk_ref[rows, 0:256]
        k = blk_ref[rows, 256:512]
        g = blk_ref[rows, 768:1024]
        kb = k.astype(BF16)
        vb = blk_ref[rows, 512:768].astype(BF16)
        o = jnp.zeros(q.shape, F32)
        for h in range(GROUP_WIDTH // HEAD_DIM):
            qh = jnp.where(head == h, q, 0.0)
            s = _dot_nt(qh, kb) * dec_ref[h]
            o = o + jnp.where(head == h, _dot(s, vb), 0.0)
        r_prev = r_ref[...]
        o = o + _dot(q, r_prev) * xi_ref[...]
        kz_t = (k * zeta_ref[...]).T
        u = _dot(kz_t, vb)
        r_ref[...] = r_prev * gbd_ref[...] + u * bd_ref[...]
        o = o * _group_rms(o, ones_ref[...]) * ng_ref[...]
        o_ref[rows, :] = _silu(g) * o


def _retention_call(ret, dec, zeta, xi, gbd, bd, ones256, norm_g, rows):
    B, S, _ = ret.shape
    C = RET_CHUNK
    const = lambda shape: pl.BlockSpec(shape, lambda b, n: (0,) * len(shape))
    return pl.pallas_call(
        _retention_kernel,
        grid=(B, S // rows),
        in_specs=[pl.BlockSpec((None, rows, 1024), lambda b, n: (b, n, 0)),
                  const((4, C, C)), const((C, 256)), const((C, 256)),
                  const((256, 256)), const((256, 256)), const((256, 256)), const((1, 256))],
        out_specs=pl.BlockSpec((None, rows, 256), lambda b, n: (b, n, 0)),
        out_shape=jax.ShapeDtypeStruct((B, S, 256), F32),
        scratch_shapes=[pltpu.VMEM((256, 256), F32)],
        compiler_params=_cparams(("arbitrary", "arbitrary")),
        name="retention",
    )(ret, dec, zeta, xi, gbd, bd, ones256, norm_g)


def _retention_consts():
    H, C = GROUP_WIDTH // HEAD_DIM, RET_CHUNK
    log_gamma = np.log1p(-(2.0 ** (-5.0 - np.arange(H, dtype=np.float64))))
    idx = np.arange(C, dtype=np.float64)
    diff = idx[:, None] - idx[None, :]
    dec = np.where(diff >= 0, np.exp(log_gamma[:, None, None] * np.maximum(diff, 0.0)), 0.0)
    zeta = np.exp(log_gamma[:, None] * (C - 1 - idx)[None, :])
    xi = np.exp(log_gamma[:, None] * (idx + 1)[None, :])
    gamma_chunk = np.exp(log_gamma * C)
    zeta256 = np.repeat(zeta.T, HEAD_DIM, axis=1)
    xi256 = np.repeat(xi.T, HEAD_DIM, axis=1)
    hid = np.arange(GROUP_WIDTH) // HEAD_DIM
    bd = (hid[:, None] == hid[None, :]).astype(np.float64)
    gbd = bd * gamma_chunk[hid][:, None]
    f = lambda a: jnp.asarray(a, dtype=F32)
    return f(dec), f(zeta256), f(xi256), f(gbd), f(bd)


def _sgu_kernel(blk_ref, w_ref, bias_ref, ones_ref, ng_ref, o_ref):
    C = w_ref.shape[-1]
    grp = _lane_group((C, GROUP_WIDTH), HEAD_DIM)
    tril = (lax.broadcasted_iota(jnp.int32, (C, C), 0) >= lax.broadcasted_iota(jnp.int32, (C, C), 1))
    w_tril = [jnp.where(tril, w_ref[gi], 0.0).astype(BF16) for gi in range(SGU_GROUPS)]
    for ci in range(blk_ref.shape[0] // C):
        rows = slice(ci * C, (ci + 1) * C)
        u = _gelu(blk_ref[rows, 0:256])
        v = _gelu(blk_ref[rows, 256:512])
        vn = v * _group_rms(v, ones_ref[...]) * ng_ref[...]
        vb = vn.astype(BF16)
        s = bias_ref[...]
        for gi in range(SGU_GROUPS):
            s = s + jnp.where(grp == gi, jnp.dot(w_tril[gi], vb, preferred_element_type=F32), 0.0)
        o_ref[rows, :] = u * s


def _sgu_call(sgu, w_s, bias256, ones256, norm_g, rows):
    B, S, _ = sgu.shape
    C = SGU_CHUNK
    const = lambda shape: pl.BlockSpec(shape, lambda b, n: (0,) * len(shape))
    return pl.pallas_call(
        _sgu_kernel,
        grid=(B, S // rows),
        in_specs=[pl.BlockSpec((None, rows, 512), lambda b, n: (b, n, 0)),
                  const((SGU_GROUPS, C, C)), const((C, 256)), const((256, 256)), const((1, 256))],
        out_specs=pl.BlockSpec((None, rows, 256), lambda b, n: (b, n, 0)),
        out_shape=jax.ShapeDtypeStruct((B, S, 256), F32),
        compiler_params=_cparams(("arbitrary", "arbitrary")),
        name="spatial_gating",
    )(sgu, w_s, bias256, ones256, norm_g)


def _s5_prep_kernel(lr_ref, li_ref, ldt_ref, bre_ref, bim_ref, a_ref, bx_ref):
    lr = lr_ref[...]
    li = li_ref[...]
    dts = jnp.exp(ldt_ref[...])
    mag = jnp.exp(lr * dts)
    a_re = mag * jnp.cos(li * dts)
    a_im = mag * jnp.sin(li * dts)
    den = lr * lr + li * li
    f_re = ((a_re - 1.0) * lr + a_im * li) / den
    f_im = (a_im * lr - (a_re - 1.0) * li) / den
    a_ref[0:1, :] = a_re
    a_ref[1:2, :] = a_im
    bre = bre_ref[...]
    bim = bim_ref[...]
    bx_ref[:, 0:S5_WIDTH] = (f_re * bre - f_im * bim).astype(BF16)
    bx_ref[:, S5_WIDTH:2 * S5_WIDTH] = (f_re * bim + f_im * bre).astype(BF16)


def _s5_prep_call(lam_re, lam_im, log_dt, b_re, b_im):
    L = lam_re.shape[0]
    W = S5_WIDTH
    lr = lam_re.reshape(L, 1, W)
    li = lam_im.reshape(L, 1, W)
    ldt = jnp.repeat(log_dt, S5_STATE, axis=-1).reshape(L, 1, W)
    eye = jnp.eye(S5_GROUPS, dtype=F32)

    def dense(b):
        return jnp.einsum('lgph,gk->lghkp', b, eye).reshape(L, S5_GROUPS * S5_CH, W)

    vec = pl.BlockSpec((None, 1, W), lambda l: (l, 0, 0))
    mat = pl.BlockSpec((None, GROUP_WIDTH, W), lambda l: (l, 0, 0))
    return pl.pallas_call(
        _s5_prep_kernel,
        grid=(L,),
        in_specs=[vec, vec, vec, mat, mat],
        out_specs=[pl.BlockSpec((None, 2, W), lambda l: (l, 0, 0)),
                   pl.BlockSpec((None, GROUP_WIDTH, 2 * W), lambda l: (l, 0, 0))],
        out_shape=[jax.ShapeDtypeStruct((L, 2, W), F32),
                   jax.ShapeDtypeStruct((L, GROUP_WIDTH, 2 * W), BF16)],
        compiler_params=_cparams(("arbitrary",)),
        name="s5_discretise",
    )(lr, li, ldt, dense(b_re), dense(b_im))


def _s5_kernel(u_ref, a_ref, bx_ref, cre_ref, cim_ref, d_ref, gw_ref, gb_ref, o_ref,
               hre_ref, him_ref, xre_ref, xim_ref, *, nb, tc):
    @pl.when(pl.program_id(0) == 0)
    def _():
        hre_ref[...] = jnp.zeros_like(hre_ref)
        him_ref[...] = jnp.zeros_like(him_ref)

    u = u_ref[...]
    ub = u.astype(BF16)
    xre_ref[...] = jnp.dot(ub, bx_ref[:, 0:S5_WIDTH], preferred_element_type=F32)
    xim_ref[...] = jnp.dot(ub, bx_ref[:, S5_WIDTH:2 * S5_WIDTH], preferred_element_type=F32)
    a_re = jnp.broadcast_to(a_ref[0:1, :], (nb, S5_WIDTH))
    a_im = jnp.broadcast_to(a_ref[1:2, :], (nb, S5_WIDTH))

    def step(t, carry):
        h_re, h_im = carry
        r0 = pl.multiple_of(t * nb, nb)
        n_re = a_re * h_re - a_im * h_im + xre_ref[pl.ds(r0, nb), :]
        n_im = a_re * h_im + a_im * h_re + xim_ref[pl.ds(r0, nb), :]
        xre_ref[pl.ds(r0, nb), :] = n_re
        xim_ref[pl.ds(r0, nb), :] = n_im
        return n_re, n_im

    h_re, h_im = lax.fori_loop(0, tc, step, (hre_ref[...], him_ref[...]))
    hre_ref[...] = h_re
    him_ref[...] = h_im
    y = _dot(xre_ref[...], cre_ref[...]) - _dot(xim_ref[...], cim_ref[...])
    y = _gelu(y + d_ref[...] * u)
    o_ref[...] = y * _sigmoid(_dot(y, gw_ref[...]) + gb_ref[...])


def _s5_call(u2d, a, bx, cre, cim, d, glu_w, glu_b, nb, tc):
    rows = u2d.shape[0]
    W = S5_WIDTH
    const = lambda shape: pl.BlockSpec(shape, lambda i: (0,) * len(shape))
    return pl.pallas_call(
        functools.partial(_s5_kernel, nb=nb, tc=tc),
        grid=(rows // (tc * nb),),
        in_specs=[pl.BlockSpec((tc * nb, 256), lambda i: (i, 0)),
                  const((2, W)), const((256, 2 * W)), const((W, 256)), const((W, 256)),
                  const((1, 256)), const((256, 256)), const((1, 256))],
        out_specs=pl.BlockSpec((tc * nb, 256), lambda i: (i, 0)),
        out_shape=jax.ShapeDtypeStruct((rows, 256), F32),
        scratch_shapes=[pltpu.VMEM((nb, W), F32), pltpu.VMEM((nb, W), F32),
                        pltpu.VMEM((tc * nb, W), F32), pltpu.VMEM((tc * nb, W), F32)],
        compiler_params=_cparams(("arbitrary",)),
        name="s5_scan",
    )(u2d, a, bx, cre, cim, d, glu_w, glu_b)


def _compress_kernel(hm_ref, pet_ref, peb_ref, w1t_ref, w1b_ref, w2_ref, ones_ref, kg_ref,
                     cos_ref, sin_ref, k_ref, v_ref):
    hm = hm_ref[...]
    a = _dot(hm + pet_ref[...], w1t_ref[...])
    b = _dot(hm + peb_ref[...], w1b_ref[...])
    n = a.shape[0]
    pre = a + pltpu.roll(b, n - 1, 0)
    cmp = _dot(_gelu(pre), w2_ref[...])
    k = cmp[:, 0:128]
    kn = k * _group_rms(k, ones_ref[...]) * kg_ref[...]
    k_ref[...] = _rope(kn, cos_ref[...], sin_ref[...])
    v_ref[...] = cmp[:, 128:256]


def _compress_call(hm, pet, peb, w1t, w1b, w2, ones128, k_gain, cos_c, sin_c):
    B, n, width = hm.shape
    const = lambda shape: pl.BlockSpec(shape, lambda b: (0,) * len(shape))
    out = pl.BlockSpec((None, n, 128), lambda b: (b, 0, 0))
    return pl.pallas_call(
        _compress_kernel,
        grid=(B,),
        in_specs=[pl.BlockSpec((None, n, width), lambda b: (b, 0, 0)),
                  const((1, width)), const((1, width)), const((width, 512)), const((width, 512)),
                  const((512, 256)), const((128, 128)), const((1, 128)),
                  const((n, 128)), const((n, 128))],
        out_specs=[out, out],
        out_shape=[jax.ShapeDtypeStruct((B, n, 128), F32)] * 2,
        compiler_params=_cparams(("arbitrary",)),
        name="nsa_compress",
    )(hm, pet, peb, w1t, w1b, w2, ones128, k_gain, cos_c, sin_c)


def _nsa_kernel(nq_ref, gate_ref, kc_ref, vc_ref, kv_ref, ovt_ref, o_ref,
                s_ref, m_ref, l_ref, acc_ref, *, n_cmp_rows, win_keys):
    T = ATT_TILE
    qi = pl.program_id(1)
    t0 = qi * T
    pos_c = t0 + lax.broadcasted_iota(jnp.int32, (T, 1), 0)
    pos_r = t0 + lax.broadcasted_iota(jnp.int32, (1, T), 1)
    kvh_lane = _lane_group((T, 128), HEAD_DIM)
    combos = [(g, kvh) for g in range(2) for kvh in range(2)]
    rows = lambda ci: slice(ci * T, (ci + 1) * T)
    q_all = jnp.concatenate(
        [jnp.where(kvh_lane == kvh, nq_ref[:, g * 128:(g + 1) * 128], 0.0).astype(BF16)
         for g, kvh in combos], axis=0)

    def head_lanes(x):
        return [jnp.where(kvh_lane == 0, x[rows(2 * g)], x[rows(2 * g + 1)]) for g in range(2)]

    cmp_end = lax.broadcasted_iota(jnp.int32, (1, n_cmp_rows), 1) * CMP_STRIDE + (CMP_BLOCK - 1)
    valid = cmp_end <= pos_c
    s_all = _dot_nt(q_all, kc_ref[...])
    p_list = []
    for ci in range(4):
        s = jnp.where(valid, s_all[rows(ci)], NEG)
        e = jnp.exp(s - jnp.max(s, axis=-1, keepdims=True))
        inv = 1.0 / jnp.sum(e, axis=-1, keepdims=True)
        p_list.append(jnp.where(valid, e * inv, 0.0))
    o_cmp = head_lanes(_dot(jnp.concatenate(p_list, axis=0), vc_ref[...]))

    n_sel = ovt_ref.shape[0]
    blk = lax.broadcasted_iota(jnp.int32, (n_sel, T), 0)
    cur = pos_r // SEL_BLOCK
    forced = (blk == 0) | (blk == cur) | (blk == cur - 1)
    causal_blk = blk <= cur
    ovt = ovt_ref[...]
    sel_rows = []
    for kvh in range(2):
        p_kvh = p_list[kvh] + p_list[2 + kvh]
        ph = p_kvh.astype(BF16)
        plo = (p_kvh - ph.astype(F32)).astype(BF16)
        imp = (lax.dot_general(ovt, ph, (((1,), (1,)), ((), ())), preferred_element_type=F32)
               + lax.dot_general(ovt, plo, (((1,), (1,)), ((), ())), preferred_element_type=F32))
        imp = jnp.where(forced, imp + FORCE_BONUS, imp)
        imp = jnp.where(causal_blk, imp, NEG)
        cnt = jnp.zeros((n_sel, T), F32)
        for j in range(n_sel):
            row = imp[j:j + 1, :]
            ahead = (row > imp) | ((row == imp) & (blk > j))
            cnt = cnt + jnp.where(ahead, 1.0, 0.0)
        sel_t = jnp.where((cnt < SEL_TOPK) & causal_blk, 1.0, 0.0)
        sel_t = jnp.concatenate([sel_t, jnp.zeros((128 - n_sel, T), F32)], axis=0)
        sel_rows.append(sel_t.T.astype(BF16))

    tri = (lax.broadcasted_iota(jnp.int32, (T, T), 0) >= lax.broadcasted_iota(jnp.int32, (T, T), 1))
    blk_r = lax.broadcasted_iota(jnp.int32, (128, T), 0)
    blk_c = lax.broadcasted_iota(jnp.int32, (128, T), 1) // SEL_BLOCK
    m_ref[...] = jnp.full(m_ref.shape, NEG, F32)

    def scores(j, diagonal):
        r0 = pl.multiple_of(j * T, T)
        s = _dot_nt(q_all, kv_ref[pl.ds(r0, T), 0:128])
        expand = jnp.where(blk_r == (T // SEL_BLOCK) * j + blk_c, 1.0, 0.0).astype(BF16)
        for kvh in range(2):
            ok = jnp.dot(sel_rows[kvh], expand, preferred_element_type=F32) > 0.5
            if diagonal:
                ok = ok & tri
            for g in range(2):
                ci = 2 * g + kvh
                sm = jnp.where(ok, s[rows(ci)], NEG)
                s_ref[j, rows(ci), :] = sm
                m_ref[rows(ci), :] = jnp.maximum(m_ref[rows(ci), :], sm)

    def pass1(j, carry):
        scores(j, False)
        return carry

    lax.fori_loop(0, qi, pass1, 0)
    scores(qi, True)
    m_ref[...] = jnp.broadcast_to(jnp.max(m_ref[...], axis=-1, keepdims=True), m_ref.shape)
    l_ref[...] = jnp.zeros_like(l_ref)
    acc_ref[...] = jnp.zeros_like(acc_ref)

    def pass2(j, carry):
        r0 = pl.multiple_of(j * T, T)
        p = jnp.exp(s_ref[j] - m_ref[...])
        l_ref[...] += p
        acc_ref[...] += _dot(p, kv_ref[pl.ds(r0, T), 128:256])
        return carry

    lax.fori_loop(0, qi + 1, pass2, 0)
    inv_l = 1.0 / jnp.sum(l_ref[...], axis=-1, keepdims=True)
    o_sel = head_lanes(acc_ref[...] * inv_l)

    w0 = pl.multiple_of(jnp.maximum(t0 + T - win_keys, 0), T)
    dist = pos_c - (w0 + lax.broadcasted_iota(jnp.int32, (1, win_keys), 1))
    ok_w = (dist >= 0) & (dist < WIN)
    s_w = _dot_nt(q_all, kv_ref[pl.ds(w0, win_keys), 256:384])
    vw = kv_ref[pl.ds(w0, win_keys), 384:512].astype(BF16)
    o_w = []
    for ci in range(4):
        s = jnp.where(ok_w, s_w[rows(ci)], NEG)
        e = jnp.exp(s - jnp.max(s, axis=-1, keepdims=True))
        inv = 1.0 / jnp.sum(e, axis=-1, keepdims=True)
        o_w.append(_dot(e, vw) * inv)
    o_win = head_lanes(jnp.concatenate(o_w, axis=0))

    for g in range(2):
        gc = gate_ref[:, (g * 3 + 0) * 128:(g * 3 + 1) * 128]
        gs = gate_ref[:, (g * 3 + 1) * 128:(g * 3 + 2) * 128]
        gw = gate_ref[:, (g * 3 + 2) * 128:(g * 3 + 3) * 128]
        o_ref[:, g * 128:(g + 1) * 128] = gc * o_cmp[g] + gs * o_sel[g] + gw * o_win[g]


def _nsa_call(nq, gates, kcmp, vcmp, kv, ovt):
    B, S, _ = nq.shape
    T = ATT_TILE
    n = kcmp.shape[1]
    n_sel = ovt.shape[0]
    win_keys = min(WIN + T, S)
    return pl.pallas_call(
        functools.partial(_nsa_kernel, n_cmp_rows=n, win_keys=win_keys),
        grid=(B, S // T),
        in_specs=[pl.BlockSpec((None, T, 256), lambda b, i: (b, i, 0)),
                  pl.BlockSpec((None, T, 768), lambda b, i: (b, i, 0)),
                  pl.BlockSpec((None, n, 128), lambda b, i: (b, 0, 0)),
                  pl.BlockSpec((None, n, 128), lambda b, i: (b, 0, 0)),
                  pl.BlockSpec((None, S, 512), lambda b, i: (b, 0, 0)),
                  pl.BlockSpec((n_sel, n), lambda b, i: (0, 0))],
        out_specs=pl.BlockSpec((None, T, 256), lambda b, i: (b, i, 0)),
        out_shape=jax.ShapeDtypeStruct((B, S, 256), F32),
        scratch_shapes=[pltpu.VMEM((S // T, 4 * T, T), F32), pltpu.VMEM((4 * T, T), F32),
                        pltpu.VMEM((4 * T, T), F32), pltpu.VMEM((4 * T, 128), F32)],
        compiler_params=_cparams(("arbitrary", "arbitrary")),
        name="nsa_attention",
    )(nq, gates, kcmp, vcmp, kv, ovt)


def _outproj_kernel(x_ref, yr_ref, ys_ref, yg_ref, yn_ref, mg_ref, w_ref, g1_ref, o_ref):
    acc = jnp.zeros(x_ref.shape, F32)
    for i, y_ref in enumerate((yr_ref, ys_ref, yg_ref, yn_ref)):
        y = y_ref[...]
        ms = jnp.mean(y * y, axis=-1, keepdims=True)
        yn = y * lax.rsqrt(ms + EPS) * mg_ref[i:i + 1, :]
        acc = acc + jnp.dot(yn.astype(BF16), w_ref[i * 256:(i + 1) * 256, :],
                            preferred_element_type=F32)
    o_ref[...] = x_ref[...] + g1_ref[...] * acc


def _outproj_call(x, y_ret, y_s5, y_sgu, y_nsa, mix_g, w_out, g1, ts):
    B, S, D = x.shape
    row = lambda n: pl.BlockSpec((None, ts, n), lambda b, s: (b, s, 0))
    return pl.pallas_call(
        _outproj_kernel,
        grid=(B, S // ts),
        in_specs=[row(D), row(256), pl.BlockSpec((ts, 256), lambda b, s: (s, b)), row(256), row(256),
                  pl.BlockSpec((4, 256), lambda b, s: (0, 0)),
                  pl.BlockSpec((D, D), lambda b, s: (0, 0), pipeline_mode=pl.Buffered(1)),
                  pl.BlockSpec((None, 1, D), lambda b, s: (b, 0, 0))],
        out_specs=row(D),
        out_shape=jax.ShapeDtypeStruct((B, S, D), F32),
        compiler_params=_cparams(("arbitrary", "arbitrary")),
        name="out_proj",
    )(x, y_ret, y_s5, y_sgu, y_nsa, mix_g, w_out, g1)


def _ffn_kernel(x_ref, g_ref, sc_ref, sh_ref, g2_ref, wup_ref, cw_ref, cb_ref, wdn_ref, o_ref,
                tail_ref, *, d_ff, tf):
    @pl.when(pl.program_id(1) == 0)
    def _():
        tail_ref[...] = jnp.zeros_like(tail_ref)

    x = x_ref[...]
    ts = x.shape[0]
    ms = jnp.mean(x * x, axis=-1, keepdims=True)
    h = x * lax.rsqrt(ms + EPS) * g_ref[...]
    hb = (h * (1.0 + sc_ref[...]) + sh_ref[...]).astype(BF16)
    row8 = lax.broadcasted_iota(jnp.int32, (8, tf), 0)

    def conv(c0):
        a = jnp.dot(hb, wup_ref[:, c0:c0 + tf], preferred_element_type=F32)
        t6 = tail_ref[6:7, c0:c0 + tf]
        t7 = tail_ref[7:8, c0:c0 + tf]
        a1 = pltpu.roll(a, 1, 0)
        a2 = pltpu.roll(a, 2, 0)
        a1 = jnp.concatenate([jnp.where(row8 == 0, t7, a1[0:8]), a1[8:]], axis=0)
        a2 = jnp.concatenate([jnp.where(row8 == 0, t6, jnp.where(row8 == 1, t7, a2[0:8])), a2[8:]],
                             axis=0)
        tail_ref[:, c0:c0 + tf] = a[ts - 8:ts, :]
        return (cw_ref[2:3, c0:c0 + tf] * a + cw_ref[1:2, c0:c0 + tf] * a1
                + cw_ref[0:1, c0:c0 + tf] * a2 + cb_ref[:, c0:c0 + tf])

    acc = jnp.zeros(x.shape, F32)
    for c in range(d_ff // tf):
        gate = conv(c * tf)
        up = conv(d_ff + c * tf)
        z = (_silu(gate) * up).astype(BF16)
        acc = acc + jnp.dot(z, wdn_ref[c * tf:(c + 1) * tf, :], preferred_element_type=F32)
    o_ref[...] = x + g2_ref[...] * acc


def _ffn_call(x, norm_g, sc, sh, g2, w_up, conv_w, conv_b, w_down, ts, tf):
    B, S, D = x.shape
    d_ff = w_down.shape[0]
    row = pl.BlockSpec((None, ts, D), lambda b, s: (b, s, 0))
    vec = pl.BlockSpec((None, 1, D), lambda b, s: (b, 0, 0))
    const = lambda shape: pl.BlockSpec(shape, lambda b, s: (0,) * len(shape),
                                       pipeline_mode=pl.Buffered(1))
    return pl.pallas_call(
        functools.partial(_ffn_kernel, d_ff=d_ff, tf=tf),
        grid=(B, S // ts),
        in_specs=[row, const((1, D)), vec, vec, vec, const((D, 2 * d_ff)),
                  const((CONV_W, 2 * d_ff)), const((1, 2 * d_ff)), const((d_ff, D))],
        out_specs=row,
        out_shape=jax.ShapeDtypeStruct((B, S, D), F32),
        scratch_shapes=[pltpu.VMEM((8, 2 * d_ff), F32)],
        compiler_params=_cparams(("arbitrary", "arbitrary")),
        name="conv_ffn",
    )(x, norm_g, sc, sh, g2, w_up, conv_w, conv_b, w_down)


def _rope_tables(pos):
    inv = ROPE_THETA ** (-jnp.arange(0, HEAD_DIM, 2, dtype=F32) / HEAD_DIM)
    ang = pos.astype(F32)[:, None] * inv[None, :]
    return jnp.cos(ang), jnp.sin(ang)


def _tile_rope(cos, sin, width):
    reps = width // (HEAD_DIM // 2)
    return jnp.tile(cos, (1, reps)), jnp.tile(sin, (1, reps))


def _block_ones(n):
    hid = np.arange(n) // HEAD_DIM
    return jnp.asarray(hid[:, None] == hid[None, :], dtype=BF16)


def _compress_weights(pe, w1, w2):
    half = CMP_STRIDE * HEAD_DIM
    eye2 = jnp.eye(2, dtype=F32)

    def big(w):
        w = w.reshape(2, CMP_STRIDE, HEAD_DIM, CMP_HIDDEN)
        full = jnp.einsum('ktdj,ka,hb->tkhdabj', w, eye2, eye2)
        return full.reshape(CMP_STRIDE * 4 * HEAD_DIM, 4 * CMP_HIDDEN)

    w1t = big(w1[:, :half])
    w1b = big(w1[:, half:])

    def pe_row(p):
        return jnp.broadcast_to(p[:, :, None, :], (2, CMP_STRIDE, 2, HEAD_DIM)).transpose(
            1, 0, 2, 3).reshape(1, CMP_STRIDE * 4 * HEAD_DIM)

    pet = pe_row(pe[:, :CMP_STRIDE])
    peb = pe_row(pe[:, CMP_STRIDE:])
    w2big = jnp.einsum('kjd,ka,hb->khjabd', w2, eye2, eye2).reshape(4 * CMP_HIDDEN, 4 * HEAD_DIM)
    return pet, peb, w1t.astype(BF16), w1b.astype(BF16), w2big.astype(BF16)


def _nsa_perm():
    idx = np.zeros(GROUP_WIDTH, dtype=np.int32)
    for g in range(2):
        for kvh in range(2):
            for d in range(HEAD_DIM):
                idx[g * 128 + kvh * HEAD_DIM + d] = (kvh * 2 + g) * HEAD_DIM + d
    return idx


def kernel(x, c, norm1_g, norm2_g, ada_w, ada_b, w_in, ret_norm_g, s5_lambda_re, s5_lambda_im, s5_log_dt, s5_b_re, s5_b_im, s5_c_re, s5_c_im, s5_d, s5_glu_w, s5_glu_b, sgu_norm_g, sgu_w, sgu_b, nsa_q_norm_g, nsa_k_norm_g, nsa_cmp_pe, nsa_cmp_w1, nsa_cmp_w2, mix_norm_g, w_out, ffn_w_up, ffn_conv_w, ffn_conv_b, ffn_w_down):
    B, S, D = x.shape
    L = w_in.shape[0]
    ts = min(512, S)
    s5_tc = min(64, S)

    cos, sin = _rope_tables(jnp.arange(S))
    cos256, sin256 = _tile_rope(cos, sin, 256)
    n_cmp_rows = S // CMP_STRIDE
    cos_c, sin_c = _rope_tables(jnp.arange(n_cmp_rows) * CMP_STRIDE + CMP_BLOCK - 1)
    cos_c, sin_c = _tile_rope(cos_c, sin_c, 128)
    ones256 = _block_ones(256)
    ones128 = _block_ones(128)
    dec, zeta256, xi256, gbd, bd = _retention_consts()
    n_sel = S // SEL_BLOCK
    ii = np.arange(n_cmp_rows)[None, :]
    jj = np.arange(n_sel)[:, None]
    overlap_t = ((ii * CMP_STRIDE < (jj + 1) * SEL_BLOCK) & (ii * CMP_STRIDE + CMP_BLOCK > jj * SEL_BLOCK)
                 & (ii < n_cmp_rows - 1))
    ovt = jnp.asarray(overlap_t, dtype=BF16)
    proj_cols = _proj_columns()
    nsa_perm = _nsa_perm()

    mod = _mod_call(c, ada_w, ada_b)
    s5_a, s5_bx = _s5_prep_call(s5_lambda_re, s5_lambda_im, s5_log_dt, s5_b_re, s5_b_im)
    eye_g = jnp.eye(S5_GROUPS, dtype=F32)

    for l in range(L):
        sh1, sc1, g1, sh2, sc2, g2 = [mod[l, :, i * D:(i + 1) * D].reshape(B, 1, D) for i in range(6)]
        w = w_in[l][:, proj_cols].astype(BF16)
        q_gain = jnp.tile(nsa_q_norm_g[l], 4).reshape(1, 256)
        k_gain = jnp.tile(nsa_k_norm_g[l], (1, 2))
        ret, s5u, sgu, nq, kvc, kv, gates = _inproj_call(
            x, norm1_g[l].reshape(1, D), sc1, sh1, w, cos256, sin256, ones256, q_gain, k_gain, ts)

        y_ret = _retention_call(ret, dec, zeta256, xi256, gbd, bd, ones256,
                                ret_norm_g[l].reshape(1, 256), ts)

        cre = jnp.einsum('ghp,gk->gpkh', s5_c_re[l], eye_g).reshape(S5_WIDTH, GROUP_WIDTH)
        cim = jnp.einsum('ghp,gk->gpkh', s5_c_im[l], eye_g).reshape(S5_WIDTH, GROUP_WIDTH)
        y_s5 = _s5_call(s5u.reshape(S * B, 256), s5_a[l], s5_bx[l], cre.astype(BF16), cim.astype(BF16),
                        s5_d[l].reshape(1, 256), s5_glu_w[l].astype(BF16), s5_glu_b[l].reshape(1, 256),
                        B, s5_tc).reshape(S, B * 256)

        bias256 = jnp.repeat(sgu_b[l].T, HEAD_DIM, axis=1)
        y_sgu = _sgu_call(sgu, sgu_w[l], bias256, ones256, sgu_norm_g[l].reshape(1, 256), ts)

        pet, peb, w1t, w1b, w2big = _compress_weights(nsa_cmp_pe[l], nsa_cmp_w1[l], nsa_cmp_w2[l])
        kcmp, vcmp = _compress_call(kvc.reshape(B, n_cmp_rows, CMP_STRIDE * 256), pet, peb, w1t, w1b,
                                    w2big, ones128, k_gain[0:1], cos_c, sin_c)
        y_nsa = _nsa_call(nq, gates, kcmp, vcmp, kv, ovt)

        mix_g = mix_norm_g[l].reshape(4, 256)
        mix_g = mix_g.at[3].set(mix_g[3][nsa_perm])
        wo = w_out[l]
        wo = jnp.concatenate([wo[:768], wo[768:][nsa_perm]], axis=0).astype(BF16)
        x = _outproj_call(x, y_ret, y_s5, y_sgu, y_nsa, mix_g, wo, g1, ts)

        x = _ffn_call(x, norm2_g[l].reshape(1, D), sc2, sh2, g2, ffn_w_up[l].astype(BF16),
                      ffn_conv_w[l], ffn_conv_b[l].reshape(1, -1), ffn_w_down[l].astype(BF16),
                      min(1024, S), 256)
    return x
```

```python
import functools
import math

import jax
import jax.numpy as jnp
import numpy as np
from jax import lax
from jax.experimental import pallas as pl
from jax.experimental.pallas import tpu as pltpu

F32 = jnp.float32
BF16 = jnp.bfloat16

HEAD_DIM = 64
GROUP_WIDTH = 256
RET_CHUNK = 128
S5_GROUPS = 16
S5_CH = 16
S5_STATE = 64
S5_WIDTH = S5_GROUPS * S5_STATE
SGU_GROUPS = 4
SGU_CHUNK = 128
NSA_KV_WIDTH = 128
CMP_STRIDE = 16
CMP_BLOCK = 32
CMP_HIDDEN = 128
SEL_BLOCK = 64
SEL_TOPK = 8
FORCE_BONUS = 1e3
WIN = 512
CONV_W = 3
ROPE_THETA = 10000.0
EPS = 1e-6
NEG = -1e9
QK_SCALE = HEAD_DIM ** -0.5
LOG2E = math.log2(math.e)
SEL_MASK = 2.0 ** 30

ATT_TILE = 256
VMEM_LIMIT = 56 * 1024 * 1024


def _cparams(sem):
    return pltpu.CompilerParams(dimension_semantics=sem, vmem_limit_bytes=VMEM_LIMIT)


def _dot(a, b):
    return jnp.dot(a.astype(BF16), b.astype(BF16), preferred_element_type=F32)


def _dot_nt(a, b):
    return lax.dot_general(a.astype(BF16), b.astype(BF16), (((1,), (1,)), ((), ())),
                           preferred_element_type=F32)


def _dot_split(a, b_bf16):
    hi = a.astype(BF16)
    lo = (a - hi.astype(F32)).astype(BF16)
    return (jnp.dot(hi, b_bf16, preferred_element_type=F32)
            + jnp.dot(lo, b_bf16, preferred_element_type=F32))


def _sigmoid(x):
    return 0.5 + 0.5 * jnp.tanh(0.5 * x)


def _silu(x):
    h = 0.5 * x
    return h + h * jnp.tanh(h)


def _gelu(x):
    c = math.sqrt(2.0 / math.pi)
    return 0.5 * x * (1.0 + jnp.tanh(c * (x + 0.044715 * (x * x * x))))


def _group_rms(x, ones_bd):
    ms = _dot_split(x * x, ones_bd) * (1.0 / HEAD_DIM)
    return lax.rsqrt(ms + EPS)


def _rot_half(x):
    n = x.shape[-1]
    lane = lax.broadcasted_iota(jnp.int32, x.shape, 1)
    first = (lane % HEAD_DIM) < (HEAD_DIM // 2)
    return jnp.where(first, -pltpu.roll(x, n - HEAD_DIM // 2, 1), pltpu.roll(x, HEAD_DIM // 2, 1))


def _rope(x, cos, sin):
    return x * cos + _rot_half(x) * sin


def _lane_group(shape, width):
    return lax.broadcasted_iota(jnp.int32, shape, len(shape) - 1) // width


def _mod_kernel(c_ref, w_ref, b_ref, o_ref):
    o_ref[...] = _dot(_silu(c_ref[...]), w_ref[...]) + b_ref[...]


def _mod_call(c, ada_w, ada_b):
    L, D, D6 = ada_w.shape
    B = c.shape[0]
    nj = D6 // D
    return pl.pallas_call(
        _mod_kernel,
        grid=(L, nj),
        in_specs=[pl.BlockSpec((B, D), lambda l, j: (0, 0)),
                  pl.BlockSpec((None, D, D), lambda l, j: (l, 0, j)),
                  pl.BlockSpec((None, 1, D), lambda l, j: (l, 0, j))],
        out_specs=pl.BlockSpec((None, B, D), lambda l, j: (l, 0, j)),
        out_shape=jax.ShapeDtypeStruct((L, B, D6), F32),
        compiler_params=_cparams(("arbitrary", "arbitrary")),
        name="adaln_mod",
    )(c, ada_w, ada_b.reshape(L, 1, D6))


C_RET = 0
C_S5 = 1024
C_SGU = 1280
C_NQ = 1792
C_KVC = 2048
C_KV = 2304
C_GATE = 2816
N_GATE = 12
N_PROJ = 2944


def _proj_columns():
    cols = list(range(0, 1792))
    nq0 = 1792
    for g in range(2):
        for kvh in range(2):
            base = nq0 + (kvh * 2 + g) * HEAD_DIM
            cols += list(range(base, base + HEAD_DIM))
    cols += list(range(2048, 2816 + N_GATE))
    return np.asarray(cols, dtype=np.int32)


def _gate_expand():
    e = np.zeros((128, 6 * 128), dtype=np.float32)
    for g in range(2):
        for br in range(3):
            for kvh in range(2):
                c0 = (g * 3 + br) * 128 + kvh * HEAD_DIM
                e[(kvh * 2 + g) * 3 + br, c0:c0 + HEAD_DIM] = 1.0
    return jnp.asarray(e, dtype=BF16)


def _inproj_kernel(x_ref, g_ref, sc_ref, sh_ref, w_ref, cos_ref, sin_ref, ones_ref, qg_ref, kg_ref,
                   gexp_ref, ret_ref, s5_ref, sgu_ref, nq_ref, kvc_ref, kv_ref, gate_ref):
    x = x_ref[...]
    ms = jnp.mean(x * x, axis=-1, keepdims=True)
    h = x * lax.rsqrt(ms + EPS) * g_ref[...]
    h = h * (1.0 + sc_ref[...]) + sh_ref[...]
    hb = h.astype(BF16)

    def proj(c0, n):
        return jnp.dot(hb, w_ref[:, c0:c0 + n], preferred_element_type=F32)

    cos = cos_ref[...]
    sin = sin_ref[...]
    cos_h, sin_h = cos[:, :128], sin[:, :128]
    ones256 = ones_ref[...]
    ones128 = ones256[:128, :128]

    ret_ref[:, 0:256] = _rope(proj(C_RET, 256), cos, sin) * QK_SCALE
    ret_ref[:, 256:512] = _rope(proj(C_RET + 256, 256), cos, sin)
    ret_ref[:, 512:1024] = proj(C_RET + 512, 512)
    s5_ref[...] = proj(C_S5, 256)
    sgu_ref[...] = proj(C_SGU, 512)
    q = proj(C_NQ, 256)
    qn = q * _group_rms(q, ones256) * qg_ref[...]
    nq_ref[...] = _rope(qn, cos, sin) * QK_SCALE
    kvc_ref[...] = proj(C_KVC, 256)
    kg = kg_ref[...]
    for i in range(2):
        kv = proj(C_KV + i * 256, 256)
        k = kv[:, 0:128]
        kn = k * _group_rms(k, ones128) * kg[i + 1:i + 2, :]
        kv_ref[:, i * 256:i * 256 + 128] = _rope(kn, cos_h, sin_h)
        kv_ref[:, i * 256 + 128:i * 256 + 256] = kv[:, 128:256]
    gate_ref[...] = _dot_split(_sigmoid(proj(C_GATE, 128)), gexp_ref[...])


def _inproj_call(x, norm_g, sc, sh, w, cos, sin, ones256, q_gain, k_gain, gexp, ts):
    B, S, D = x.shape
    row = lambda n: pl.BlockSpec((None, ts, n), lambda b, s: (b, s, 0))
    const = lambda shape: pl.BlockSpec(shape, lambda b, s: (0,) * len(shape),
                                       pipeline_mode=pl.Buffered(1))
    vec = pl.BlockSpec((None, 1, D), lambda b, s: (b, 0, 0))
    return pl.pallas_call(
        _inproj_kernel,
        grid=(B, S // ts),
        in_specs=[row(D), const((1, D)), vec, vec, const((D, N_PROJ)),
                  pl.BlockSpec((ts, 256), lambda b, s: (s, 0)),
                  pl.BlockSpec((ts, 256), lambda b, s: (s, 0)),
                  const((256, 256)), const((1, 256)), const((3, 128)), const((128, 768))],
        out_specs=[row(1024), pl.BlockSpec((ts, 256), lambda b, s: (s, b)), row(512), row(256),
                   row(256), row(512), row(768)],
        out_shape=[jax.ShapeDtypeStruct((B, S, 1024), F32),
                   jax.ShapeDtypeStruct((S, B * 256), F32),
                   jax.ShapeDtypeStruct((B, S, 512), F32),
                   jax.ShapeDtypeStruct((B, S, 256), F32),
                   jax.ShapeDtypeStruct((B, S, 256), F32),
                   jax.ShapeDtypeStruct((B, S, 512), F32),
                   jax.ShapeDtypeStruct((B, S, 768), F32)],
        compiler_params=_cparams(("arbitrary", "arbitrary")),
        name="in_proj",
    )(x, norm_g, sc, sh, w, cos, sin, ones256, q_gain, k_gain, gexp)


def _retention_kernel(blk_ref, dec_ref, zeta_ref, xi_ref, gbd_ref, bd_ref, ones_ref, ng_ref,
                      o_ref, r_ref):
    @pl.when(pl.program_id(1) == 0)
    def _():
        r_ref[...] = jnp.zeros_like(r_ref)

    C = RET_CHUNK
    head = _lane_group((C, GROUP_WIDTH), HEAD_DIM)
    for ci in range(blk_ref.shape[0] // C):
        rows = slice(ci * C, (ci + 1) * C)
        q = blk_ref[rows, 0:256]
        k = blk_ref[rows, 256:512]
        g = blk_ref[rows, 768:1024]
        kb = k.astype(BF16)
        vb = blk_ref[rows, 512:768].astype(BF16)
        o = jnp.zeros(q.shape, F32)
        for h in range(GROUP_WIDTH // HEAD_DIM):
            qh = jnp.where(head == h, q, 0.0)
            s = _dot_nt(qh, kb) * dec_ref[h]
            o = o + jnp.where(head == h, _dot(s, vb), 0.0)
        r_prev = r_ref[...]
        o = o + _dot(q, r_prev) * xi_ref[...]
        kz_t = (k * zeta_ref[...]).T
        u = _dot(kz_t, vb)
        r_ref[...] = r_prev * gbd_ref[...] + u * bd_ref[...]
        o = o * _group_rms(o, ones_ref[...]) * ng_ref[...]
        o_ref[rows, :] = _silu(g) * o


def _retention_call(ret, dec, zeta, xi, gbd, bd, ones256, norm_g, rows):
    B, S, _ = ret.shape
    C = RET_CHUNK
    const = lambda shape: pl.BlockSpec(shape, lambda b, n: (0,) * len(shape))
    return pl.pallas_call(
        _retention_kernel,
        grid=(B, S // rows),
        in_specs=[pl.BlockSpec((None, rows, 1024), lambda b, n: (b, n, 0)),
                  const((4, C, C)), const((C, 256)), const((C, 256)),
                  const((256, 256)), const((256, 256)), const((256, 256)), const((1, 256))],
        out_specs=pl.BlockSpec((None, rows, 256), lambda b, n: (b, n, 0)),
        out_shape=jax.ShapeDtypeStruct((B, S, 256), F32),
        scratch_shapes=[pltpu.VMEM((256, 256), F32)],
        compiler_params=_cparams(("arbitrary", "arbitrary")),
        name="retention",
    )(ret, dec, zeta, xi, gbd, bd, ones256, norm_g)


def _retention_consts():
    H, C = GROUP_WIDTH // HEAD_DIM, RET_CHUNK
    log_gamma = np.log1p(-(2.0 ** (-5.0 - np.arange(H, dtype=np.float64))))
    idx = np.arange(C, dtype=np.float64)
    diff = idx[:, None] - idx[None, :]
    dec = np.where(diff >= 0, np.exp(log_gamma[:, None, None] * np.maximum(diff, 0.0)), 0.0)
    zeta = np.exp(log_gamma[:, None] * (C - 1 - idx)[None, :])
    xi = np.exp(log_gamma[:, None] * (idx + 1)[None, :])
    gamma_chunk = np.exp(log_gamma * C)
    zeta256 = np.repeat(zeta.T, HEAD_DIM, axis=1)
    xi256 = np.repeat(xi.T, HEAD_DIM, axis=1)
    hid = np.arange(GROUP_WIDTH) // HEAD_DIM
    bd = (hid[:, None] == hid[None, :]).astype(np.float64)
    gbd = bd * gamma_chunk[hid][:, None]
    f = lambda a: jnp.asarray(a, dtype=F32)
    return f(dec), f(zeta256), f(xi256), f(gbd), f(bd)


def _sgu_kernel(blk_ref, w_ref, bias_ref, ones_ref, ng_ref, o_ref):
    C = w_ref.shape[-1]
    grp = _lane_group((C, GROUP_WIDTH), HEAD_DIM)
    tril = (lax.broadcasted_iota(jnp.int32, (C, C), 0) >= lax.broadcasted_iota(jnp.int32, (C, C), 1))
    w_tril = [jnp.where(tril, w_ref[gi], 0.0).astype(BF16) for gi in range(SGU_GROUPS)]
    for ci in range(blk_ref.shape[0] // C):
        rows = slice(ci * C, (ci + 1) * C)
        u = _gelu(blk_ref[rows, 0:256])
        v = _gelu(blk_ref[rows, 256:512])
        vn = v * _group_rms(v, ones_ref[...]) * ng_ref[...]
        vb = vn.astype(BF16)
        s = bias_ref[...]
        for gi in range(SGU_GROUPS):
            s = s + jnp.where(grp == gi, jnp.dot(w_tril[gi], vb, preferred_element_type=F32), 0.0)
        o_ref[rows, :] = u * s


def _sgu_call(sgu, w_s, bias256, ones256, norm_g, rows):
    B, S, _ = sgu.shape
    C = SGU_CHUNK
    const = lambda shape: pl.BlockSpec(shape, lambda b, n: (0,) * len(shape))
    return pl.pallas_call(
        _sgu_kernel,
        grid=(B, S // rows),
        in_specs=[pl.BlockSpec((None, rows, 512), lambda b, n: (b, n, 0)),
                  const((SGU_GROUPS, C, C)), const((C, 256)), const((256, 256)), const((1, 256))],
        out_specs=pl.BlockSpec((None, rows, 256), lambda b, n: (b, n, 0)),
        out_shape=jax.ShapeDtypeStruct((B, S, 256), F32),
        compiler_params=_cparams(("arbitrary", "arbitrary")),
        name="spatial_gating",
    )(sgu, w_s, bias256, ones256, norm_g)


def _s5_prep_kernel(lr_ref, li_ref, ldt_ref, bre_ref, bim_ref, a_ref, bx_ref):
    lr = lr_ref[...]
    li = li_ref[...]
    dts = jnp.exp(ldt_ref[...])
    mag = jnp.exp(lr * dts)
    a_re = mag * jnp.cos(li * dts)
    a_im = mag * jnp.sin(li * dts)
    den = lr * lr + li * li
    f_re = ((a_re - 1.0) * lr + a_im * li) / den
    f_im = (a_im * lr - (a_re - 1.0) * li) / den
    a_ref[0:1, :] = a_re
    a_ref[1:2, :] = a_im
    bre = bre_ref[...]
    bim = bim_ref[...]
    bx_ref[:, 0:S5_WIDTH] = (f_re * bre - f_im * bim).astype(BF16)
    bx_ref[:, S5_WIDTH:2 * S5_WIDTH] = (f_re * bim + f_im * bre).astype(BF16)


def _s5_prep_call(lam_re, lam_im, log_dt, b_re, b_im):
    L = lam_re.shape[0]
    W = S5_WIDTH
    lr = lam_re.reshape(L, 1, W)
    li = lam_im.reshape(L, 1, W)
    ldt = jnp.repeat(log_dt, S5_STATE, axis=-1).reshape(L, 1, W)
    eye = jnp.eye(S5_GROUPS, dtype=F32)

    def dense(b):
        return jnp.einsum('lgph,gk->lghkp', b, eye).reshape(L, S5_GROUPS * S5_CH, W)

    vec = pl.BlockSpec((None, 1, W), lambda l: (l, 0, 0))
    mat = pl.BlockSpec((None, GROUP_WIDTH, W), lambda l: (l, 0, 0))
    return pl.pallas_call(
        _s5_prep_kernel,
        grid=(L,),
        in_specs=[vec, vec, vec, mat, mat],
        out_specs=[pl.BlockSpec((None, 2, W), lambda l: (l, 0, 0)),
                   pl.BlockSpec((None, GROUP_WIDTH, 2 * W), lambda l: (l, 0, 0))],
        out_shape=[jax.ShapeDtypeStruct((L, 2, W), F32),
                   jax.ShapeDtypeStruct((L, GROUP_WIDTH, 2 * W), BF16)],
        compiler_params=_cparams(("arbitrary",)),
        name="s5_discretise",
    )(lr, li, ldt, dense(b_re), dense(b_im))


def _s5_kernel(u_ref, a_ref, bx_ref, cre_ref, cim_ref, d_ref, gw_ref, gb_ref, o_ref,
               hre_ref, him_ref, xre_ref, xim_ref, *, nb, tc):
    @pl.when(pl.program_id(0) == 0)
    def _():
        hre_ref[...] = jnp.zeros_like(hre_ref)
        him_ref[...] = jnp.zeros_like(him_ref)

    u = u_ref[...]
    ub = u.astype(BF16)
    xre_ref[...] = jnp.dot(ub, bx_ref[:, 0:S5_WIDTH], preferred_element_type=F32)
    xim_ref[...] = jnp.dot(ub, bx_ref[:, S5_WIDTH:2 * S5_WIDTH], preferred_element_type=F32)
    a_re = jnp.broadcast_to(a_ref[0:1, :], (nb, S5_WIDTH))
    a_im = jnp.broadcast_to(a_ref[1:2, :], (nb, S5_WIDTH))

    def step(t, carry):
        h_re, h_im = carry
        r0 = pl.multiple_of(t * nb, nb)
        n_re = a_re * h_re - a_im * h_im + xre_ref[pl.ds(r0, nb), :]
        n_im = a_re * h_im + a_im * h_re + xim_ref[pl.ds(r0, nb), :]
        xre_ref[pl.ds(r0, nb), :] = n_re
        xim_ref[pl.ds(r0, nb), :] = n_im
        return n_re, n_im

    h_re, h_im = lax.fori_loop(0, tc, step, (hre_ref[...], him_ref[...]))
    hre_ref[...] = h_re
    him_ref[...] = h_im
    y = _dot(xre_ref[...], cre_ref[...]) - _dot(xim_ref[...], cim_ref[...])
    y = _gelu(y + d_ref[...] * u)
    o_ref[...] = y * _sigmoid(_dot(y, gw_ref[...]) + gb_ref[...])


def _s5_call(u2d, a, bx, cre, cim, d, glu_w, glu_b, nb, tc):
    rows = u2d.shape[0]
    W = S5_WIDTH
    const = lambda shape: pl.BlockSpec(shape, lambda i: (0,) * len(shape))
    return pl.pallas_call(
        functools.partial(_s5_kernel, nb=nb, tc=tc),
        grid=(rows // (tc * nb),),
        in_specs=[pl.BlockSpec((tc * nb, 256), lambda i: (i, 0)),
                  const((2, W)), const((256, 2 * W)), const((W, 256)), const((W, 256)),
                  const((1, 256)), const((256, 256)), const((1, 256))],
        out_specs=pl.BlockSpec((tc * nb, 256), lambda i: (i, 0)),
        out_shape=jax.ShapeDtypeStruct((rows, 256), F32),
        scratch_shapes=[pltpu.VMEM((nb, W), F32), pltpu.VMEM((nb, W), F32),
                        pltpu.VMEM((tc * nb, W), F32), pltpu.VMEM((tc * nb, W), F32)],
        compiler_params=_cparams(("arbitrary",)),
        name="s5_scan",
    )(u2d, a, bx, cre, cim, d, glu_w, glu_b)


def _compress_kernel(hm_ref, pet_ref, peb_ref, w1t_ref, w1b_ref, w2_ref, ones_ref, kg_ref,
                     cos_ref, sin_ref, k_ref, v_ref):
    hm = hm_ref[...]
    a = _dot(hm + pet_ref[...], w1t_ref[...])
    b = _dot(hm + peb_ref[...], w1b_ref[...])
    n = a.shape[0]
    pre = a + pltpu.roll(b, n - 1, 0)
    cmp = _dot(_gelu(pre), w2_ref[...])
    k = cmp[:, 0:128]
    kn = k * _group_rms(k, ones_ref[...]) * kg_ref[...]
    k_ref[...] = _rope(kn, cos_ref[...], sin_ref[...])
    v_ref[...] = cmp[:, 128:256]


def _compress_call(hm, pet, peb, w1t, w1b, w2, ones128, k_gain, cos_c, sin_c):
    B, n, width = hm.shape
    const = lambda shape: pl.BlockSpec(shape, lambda b: (0,) * len(shape))
    out = pl.BlockSpec((None, n, 128), lambda b: (b, 0, 0))
    return pl.pallas_call(
        _compress_kernel,
        grid=(B,),
        in_specs=[pl.BlockSpec((None, n, width), lambda b: (b, 0, 0)),
                  const((1, width)), const((1, width)), const((width, 512)), const((width, 512)),
                  const((512, 256)), const((128, 128)), const((1, 128)),
                  const((n, 128)), const((n, 128))],
        out_specs=[out, out],
        out_shape=[jax.ShapeDtypeStruct((B, n, 128), F32)] * 2,
        compiler_params=_cparams(("arbitrary",)),
        name="nsa_compress",
    )(hm, pet, peb, w1t, w1b, w2, ones128, k_gain, cos_c, sin_c)


def _nsa_kernel(nq_ref, gate_ref, kc_ref, vc_ref, kv_ref, ovt_ref, o_ref,
                s_ref, m_ref, acc_ref, *, n_cmp_rows, win_keys):
    T = ATT_TILE
    qi = pl.program_id(1)
    t0 = qi * T
    pos_c = t0 + lax.broadcasted_iota(jnp.int32, (T, 1), 0)
    pos_r = t0 + lax.broadcasted_iota(jnp.int32, (1, T), 1)
    kvh_lane = _lane_group((T, 128), HEAD_DIM)
    combos = [(g, kvh) for g in range(2) for kvh in range(2)]
    rows = lambda ci: slice(ci * T, (ci + 1) * T)
    q_all = jnp.concatenate(
        [jnp.where(kvh_lane == kvh, nq_ref[:, g * 128:(g + 1) * 128] * LOG2E, 0.0).astype(BF16)
         for g, kvh in combos], axis=0)
    ones_cols = jnp.ones((T, 128), BF16)

    def head_lanes(x):
        return [jnp.where(kvh_lane == 0, x[rows(2 * g)], x[rows(2 * g + 1)]) for g in range(2)]

    w0 = pl.multiple_of(jnp.maximum(t0 + T - win_keys, 0), T)
    dist = pos_c - (w0 + lax.broadcasted_iota(jnp.int32, (1, win_keys), 1))
    ok_w = (dist >= 0) & (dist < WIN)
    s_w = _dot_nt(q_all, kv_ref[pl.ds(w0, win_keys), 256:384])
    vw_ext = jnp.concatenate([kv_ref[pl.ds(w0, win_keys), 384:512].astype(BF16),
                              jnp.ones((win_keys, 128), BF16)], axis=1)
    o_w = []
    for ci in range(4):
        s = jnp.where(ok_w, s_w[rows(ci)], NEG)
        e = jnp.exp2(s - jnp.max(s, axis=-1, keepdims=True))
        ow = jnp.dot(e.astype(BF16), vw_ext, preferred_element_type=F32)
        o_w.append(ow[:, 0:128] * (1.0 / ow[:, 128:256]))
    o_win = head_lanes(jnp.concatenate(o_w, axis=0))

    cmp_end = lax.broadcasted_iota(jnp.int32, (1, n_cmp_rows), 1) * CMP_STRIDE + (CMP_BLOCK - 1)
    valid = cmp_end <= pos_c
    s_all = _dot_nt(q_all, kc_ref[...])
    p_list = []
    for ci in range(4):
        s = jnp.where(valid, s_all[rows(ci)], NEG)
        e = jnp.exp2(s - jnp.max(s, axis=-1, keepdims=True))
        inv = 1.0 / jnp.sum(e, axis=-1, keepdims=True)
        p_list.append(jnp.where(valid, e * inv, 0.0))
    o_cmp = head_lanes(_dot(jnp.concatenate(p_list, axis=0), vc_ref[...]))

    n_sel = ovt_ref.shape[0]
    blk = lax.broadcasted_iota(jnp.int32, (n_sel, T), 0)
    cur = pos_r // SEL_BLOCK
    forced = (blk == 0) | (blk == cur) | (blk == cur - 1)
    causal_blk = blk <= cur
    ovt = ovt_ref[...]
    sel_rows = []
    for kvh in range(2):
        p_kvh = p_list[kvh] + p_list[2 + kvh]
        ph = p_kvh.astype(BF16)
        plo = (p_kvh - ph.astype(F32)).astype(BF16)
        imp = (lax.dot_general(ovt, ph, (((1,), (1,)), ((), ())), preferred_element_type=F32)
               + lax.dot_general(ovt, plo, (((1,), (1,)), ((), ())), preferred_element_type=F32))
        imp = jnp.where(forced, imp + FORCE_BONUS, imp)
        imp = jnp.where(causal_blk, imp, NEG)
        cnt = jnp.zeros((n_sel, T), F32)
        for j in range(n_sel):
            row = imp[j:j + 1, :]
            ahead = (row > imp) | ((row == imp) & (blk > j))
            cnt = cnt + jnp.where(ahead, 1.0, 0.0)
        sel_t = jnp.where((cnt < SEL_TOPK) & causal_blk, 1.0, 0.0)
        sel_t = jnp.concatenate([sel_t, jnp.zeros((128 - n_sel, T), F32)], axis=0)
        sel_rows.append(sel_t.T.astype(BF16))

    tri = (lax.broadcasted_iota(jnp.int32, (T, T), 0) >= lax.broadcasted_iota(jnp.int32, (T, T), 1))
    bias = [((s.astype(F32) - 1.0) * SEL_MASK).astype(BF16) for s in sel_rows]
    q_ext = jnp.concatenate([q_all, jnp.concatenate([bias[kvh] for _, kvh in combos], axis=0)], axis=1)
    key_blk = lax.broadcasted_iota(jnp.int32, (T, 128), 0) // SEL_BLOCK
    lane_id = lax.broadcasted_iota(jnp.int32, (T, 128), 1)
    m_ref[...] = jnp.full(m_ref.shape, NEG, F32)

    def scores(j, diagonal):
        r0 = pl.multiple_of(j * T, T)
        onehot = jnp.where(lane_id == (T // SEL_BLOCK) * j + key_blk, 1.0, 0.0).astype(BF16)
        k_ext = jnp.concatenate([kv_ref[pl.ds(r0, T), 0:128].astype(BF16), onehot], axis=1)
        s = _dot_nt(q_ext, k_ext)
        for ci in range(4):
            sm = jnp.where(tri, s[rows(ci)], NEG) if diagonal else s[rows(ci)]
            s_ref[j, rows(ci), :] = sm
            half = sm[:, 0:128]
            for c in range(1, T // 128):
                half = jnp.maximum(half, sm[:, c * 128:(c + 1) * 128])
            m_ref[rows(ci), :] = jnp.maximum(m_ref[rows(ci), :], half)

    def pass1(jp, carry):
        scores(2 * jp, False)
        scores(2 * jp + 1, False)
        return carry

    lax.fori_loop(0, qi // 2, pass1, 0)

    @pl.when(qi % 2 == 1)
    def _():
        scores(qi - 1, False)

    scores(qi, True)
    m_ref[...] = jnp.broadcast_to(jnp.max(m_ref[...], axis=-1, keepdims=True), m_ref.shape)
    acc_ref[...] = jnp.zeros_like(acc_ref)

    def weighted(j):
        r0 = pl.multiple_of(j * T, T)
        m = m_ref[...]
        p = jnp.exp2(s_ref[j] - jnp.concatenate([m] * (T // 128), axis=1))
        v_ext = jnp.concatenate([kv_ref[pl.ds(r0, T), 128:256].astype(BF16), ones_cols], axis=1)
        return jnp.dot(p.astype(BF16), v_ext, preferred_element_type=F32)

    def pass2(jp, carry):
        acc_ref[...] += weighted(2 * jp) + weighted(2 * jp + 1)
        return carry

    lax.fori_loop(0, (qi + 1) // 2, pass2, 0)

    @pl.when(qi % 2 == 0)
    def _():
        acc_ref[...] += weighted(qi)
    o_sel = head_lanes(acc_ref[:, 0:128] * (1.0 / acc_ref[:, 128:256]))

    for g in range(2):
        gc = gate_ref[:, (g * 3 + 0) * 128:(g * 3 + 1) * 128]
        gs = gate_ref[:, (g * 3 + 1) * 128:(g * 3 + 2) * 128]
        gw = gate_ref[:, (g * 3 + 2) * 128:(g * 3 + 3) * 128]
        o_ref[:, g * 128:(g + 1) * 128] = gc * o_cmp[g] + gs * o_sel[g] + gw * o_win[g]


def _nsa_call(nq, gates, kcmp, vcmp, kv, ovt):
    B, S, _ = nq.shape
    T = ATT_TILE
    n = kcmp.shape[1]
    n_sel = ovt.shape[0]
    win_keys = min(WIN + T, S)
    return pl.pallas_call(
        functools.partial(_nsa_kernel, n_cmp_rows=n, win_keys=win_keys),
        grid=(B, S // T),
        in_specs=[pl.BlockSpec((None, T, 256), lambda b, i: (b, i, 0)),
                  pl.BlockSpec((None, T, 768), lambda b, i: (b, i, 0)),
                  pl.BlockSpec((None, n, 128), lambda b, i: (b, 0, 0)),
                  pl.BlockSpec((None, n, 128), lambda b, i: (b, 0, 0)),
                  pl.BlockSpec((None, S, 512), lambda b, i: (b, 0, 0)),
                  pl.BlockSpec((n_sel, n), lambda b, i: (0, 0))],
        out_specs=pl.BlockSpec((None, T, 256), lambda b, i: (b, i, 0)),
        out_shape=jax.ShapeDtypeStruct((B, S, 256), F32),
        scratch_shapes=[pltpu.VMEM((S // T, 4 * T, T), F32), pltpu.VMEM((4 * T, 128), F32),
                        pltpu.VMEM((4 * T, 256), F32)],
        compiler_params=_cparams(("arbitrary", "arbitrary")),
        name="nsa_attention",
    )(nq, gates, kcmp, vcmp, kv, ovt)


def _outproj_kernel(x_ref, yr_ref, ys_ref, yg_ref, yn_ref, mg_ref, w_ref, g1_ref, o_ref):
    acc = jnp.zeros(x_ref.shape, F32)
    for i, y_ref in enumerate((yr_ref, ys_ref, yg_ref, yn_ref)):
        y = y_ref[...]
        ms = jnp.mean(y * y, axis=-1, keepdims=True)
        yn = y * lax.rsqrt(ms + EPS) * mg_ref[i:i + 1, :]
        acc = acc + jnp.dot(yn.astype(BF16), w_ref[i * 256:(i + 1) * 256, :],
                            preferred_element_type=F32)
    o_ref[...] = x_ref[...] + g1_ref[...] * acc


def _outproj_call(x, y_ret, y_s5, y_sgu, y_nsa, mix_g, w_out, g1, ts):
    B, S, D = x.shape
    row = lambda n: pl.BlockSpec((None, ts, n), lambda b, s: (b, s, 0))
    return pl.pallas_call(
        _outproj_kernel,
        grid=(B, S // ts),
        in_specs=[row(D), row(256), pl.BlockSpec((ts, 256), lambda b, s: (s, b)), row(256), row(256),
                  pl.BlockSpec((4, 256), lambda b, s: (0, 0)),
                  pl.BlockSpec((D, D), lambda b, s: (0, 0), pipeline_mode=pl.Buffered(1)),
                  pl.BlockSpec((None, 1, D), lambda b, s: (b, 0, 0))],
        out_specs=row(D),
        out_shape=jax.ShapeDtypeStruct((B, S, D), F32),
        compiler_params=_cparams(("arbitrary", "arbitrary")),
        name="out_proj",
    )(x, y_ret, y_s5, y_sgu, y_nsa, mix_g, w_out, g1)


def _ffn_kernel(x_ref, g_ref, sc_ref, sh_ref, g2_ref, wup_ref, cw_ref, cb_ref, wdn_ref, o_ref,
                tail_ref, *, d_ff, tf):
    @pl.when(pl.program_id(1) == 0)
    def _():
        tail_ref[...] = jnp.zeros_like(tail_ref)

    x = x_ref[...]
    ts = x.shape[0]
    ms = jnp.mean(x * x, axis=-1, keepdims=True)
    h = x * lax.rsqrt(ms + EPS) * g_ref[...]
    hb = (h * (1.0 + sc_ref[...]) + sh_ref[...]).astype(BF16)
    row8 = lax.broadcasted_iota(jnp.int32, (8, tf), 0)

    def conv(c0):
        a = jnp.dot(hb, wup_ref[:, c0:c0 + tf], preferred_element_type=F32)
        t6 = tail_ref[6:7, c0:c0 + tf]
        t7 = tail_ref[7:8, c0:c0 + tf]
        a1 = pltpu.roll(a, 1, 0)
        a2 = pltpu.roll(a, 2, 0)
        a1 = jnp.concatenate([jnp.where(row8 == 0, t7, a1[0:8]), a1[8:]], axis=0)
        a2 = jnp.concatenate([jnp.where(row8 == 0, t6, jnp.where(row8 == 1, t7, a2[0:8])), a2[8:]],
                             axis=0)
        tail_ref[:, c0:c0 + tf] = a[ts - 8:ts, :]
        return (cw_ref[2:3, c0:c0 + tf] * a + cw_ref[1:2, c0:c0 + tf] * a1
                + cw_ref[0:1, c0:c0 + tf] * a2 + cb_ref[:, c0:c0 + tf])

    acc = jnp.zeros(x.shape, F32)
    for c in range(d_ff // tf):
        gate = conv(c * tf)
        up = conv(d_ff + c * tf)
        z = (_silu(gate) * up).astype(BF16)
        acc = acc + jnp.dot(z, wdn_ref[c * tf:(c + 1) * tf, :], preferred_element_type=F32)
    o_ref[...] = x + g2_ref[...] * acc


def _ffn_call(x, norm_g, sc, sh, g2, w_up, conv_w, conv_b, w_down, ts, tf):
    B, S, D = x.shape
    d_ff = w_down.shape[0]
    row = pl.BlockSpec((None, ts, D), lambda b, s: (b, s, 0))
    vec = pl.BlockSpec((None, 1, D), lambda b, s: (b, 0, 0))
    const = lambda shape: pl.BlockSpec(shape, lambda b, s: (0,) * len(shape),
                                       pipeline_mode=pl.Buffered(1))
    return pl.pallas_call(
        functools.partial(_ffn_kernel, d_ff=d_ff, tf=tf),
        grid=(B, S // ts),
        in_specs=[row, const((1, D)), vec, vec, vec, const((D, 2 * d_ff)),
                  const((CONV_W, 2 * d_ff)), const((1, 2 * d_ff)), const((d_ff, D))],
        out_specs=row,
        out_shape=jax.ShapeDtypeStruct((B, S, D), F32),
        scratch_shapes=[pltpu.VMEM((8, 2 * d_ff), F32)],
        compiler_params=_cparams(("arbitrary", "arbitrary")),
        name="conv_ffn",
    )(x, norm_g, sc, sh, g2, w_up, conv_w, conv_b, w_down)


def _rope_tables(pos):
    inv = ROPE_THETA ** (-jnp.arange(0, HEAD_DIM, 2, dtype=F32) / HEAD_DIM)
    ang = pos.astype(F32)[:, None] * inv[None, :]
    return jnp.cos(ang), jnp.sin(ang)


def _tile_rope(cos, sin, width):
    reps = width // (HEAD_DIM // 2)
    return jnp.tile(cos, (1, reps)), jnp.tile(sin, (1, reps))


def _block_ones(n):
    hid = np.arange(n) // HEAD_DIM
    return jnp.asarray(hid[:, None] == hid[None, :], dtype=BF16)


def _compress_weights(pe, w1, w2):
    half = CMP_STRIDE * HEAD_DIM
    eye2 = jnp.eye(2, dtype=F32)

    def big(w):
        w = w.reshape(2, CMP_STRIDE, HEAD_DIM, CMP_HIDDEN)
        full = jnp.einsum('ktdj,ka,hb->tkhdabj', w, eye2, eye2)
        return full.reshape(CMP_STRIDE * 4 * HEAD_DIM, 4 * CMP_HIDDEN)

    w1t = big(w1[:, :half])
    w1b = big(w1[:, half:])

    def pe_row(p):
        return jnp.broadcast_to(p[:, :, None, :], (2, CMP_STRIDE, 2, HEAD_DIM)).transpose(
            1, 0, 2, 3).reshape(1, CMP_STRIDE * 4 * HEAD_DIM)

    pet = pe_row(pe[:, :CMP_STRIDE])
    peb = pe_row(pe[:, CMP_STRIDE:])
    w2big = jnp.einsum('kjd,ka,hb->khjabd', w2, eye2, eye2).reshape(4 * CMP_HIDDEN, 4 * HEAD_DIM)
    return pet, peb, w1t.astype(BF16), w1b.astype(BF16), w2big.astype(BF16)


def _nsa_perm():
    idx = np.zeros(GROUP_WIDTH, dtype=np.int32)
    for g in range(2):
        for kvh in range(2):
            for d in range(HEAD_DIM):
                idx[g * 128 + kvh * HEAD_DIM + d] = (kvh * 2 + g) * HEAD_DIM + d
    return idx


def kernel(x, c, norm1_g, norm2_g, ada_w, ada_b, w_in, ret_norm_g, s5_lambda_re, s5_lambda_im, s5_log_dt, s5_b_re, s5_b_im, s5_c_re, s5_c_im, s5_d, s5_glu_w, s5_glu_b, sgu_norm_g, sgu_w, sgu_b, nsa_q_norm_g, nsa_k_norm_g, nsa_cmp_pe, nsa_cmp_w1, nsa_cmp_w2, mix_norm_g, w_out, ffn_w_up, ffn_conv_w, ffn_conv_b, ffn_w_down):
    B, S, D = x.shape
    L = w_in.shape[0]
    ts = min(512, S)
    s5_tc = min(64, S)

    cos, sin = _rope_tables(jnp.arange(S))
    cos256, sin256 = _tile_rope(cos, sin, 256)
    n_cmp_rows = S // CMP_STRIDE
    cos_c, sin_c = _rope_tables(jnp.arange(n_cmp_rows) * CMP_STRIDE + CMP_BLOCK - 1)
    cos_c, sin_c = _tile_rope(cos_c, sin_c, 128)
    ones256 = _block_ones(256)
    ones128 = _block_ones(128)
    dec, zeta256, xi256, gbd, bd = _retention_consts()
    n_sel = S // SEL_BLOCK
    ii = np.arange(n_cmp_rows)[None, :]
    jj = np.arange(n_sel)[:, None]
    overlap_t = ((ii * CMP_STRIDE < (jj + 1) * SEL_BLOCK) & (ii * CMP_STRIDE + CMP_BLOCK > jj * SEL_BLOCK)
                 & (ii < n_cmp_rows - 1))
    ovt = jnp.asarray(overlap_t, dtype=BF16)
    proj_cols = _proj_columns()
    gate_expand = _gate_expand()
    nsa_perm = _nsa_perm()

    mod = _mod_call(c, ada_w, ada_b)
    s5_a, s5_bx = _s5_prep_call(s5_lambda_re, s5_lambda_im, s5_log_dt, s5_b_re, s5_b_im)
    eye_g = jnp.eye(S5_GROUPS, dtype=F32)

    for l in range(L):
        sh1, sc1, g1, sh2, sc2, g2 = [mod[l, :, i * D:(i + 1) * D].reshape(B, 1, D) for i in range(6)]
        w = jnp.pad(w_in[l][:, proj_cols], ((0, 0), (0, N_PROJ - proj_cols.size))).astype(BF16)
        q_gain = jnp.tile(nsa_q_norm_g[l], 4).reshape(1, 256)
        k_gain = jnp.tile(nsa_k_norm_g[l], (1, 2))
        ret, s5u, sgu, nq, kvc, kv, gates = _inproj_call(
            x, norm1_g[l].reshape(1, D), sc1, sh1, w, cos256, sin256, ones256, q_gain, k_gain,
            gate_expand, ts)

        y_ret = _retention_call(ret, dec, zeta256, xi256, gbd, bd, ones256,
                                ret_norm_g[l].reshape(1, 256), ts)

        cre = jnp.einsum('ghp,gk->gpkh', s5_c_re[l], eye_g).reshape(S5_WIDTH, GROUP_WIDTH)
        cim = jnp.einsum('ghp,gk->gpkh', s5_c_im[l], eye_g).reshape(S5_WIDTH, GROUP_WIDTH)
        y_s5 = _s5_call(s5u.reshape(S * B, 256), s5_a[l], s5_bx[l], cre.astype(BF16), cim.astype(BF16),
                        s5_d[l].reshape(1, 256), s5_glu_w[l].astype(BF16), s5_glu_b[l].reshape(1, 256),
                        B, s5_tc).reshape(S, B * 256)

        bias256 = jnp.repeat(sgu_b[l].T, HEAD_DIM, axis=1)
        y_sgu = _sgu_call(sgu, sgu_w[l], bias256, ones256, sgu_norm_g[l].reshape(1, 256), ts)

        pet, peb, w1t, w1b, w2big = _compress_weights(nsa_cmp_pe[l], nsa_cmp_w1[l], nsa_cmp_w2[l])
        kcmp, vcmp = _compress_call(kvc.reshape(B, n_cmp_rows, CMP_STRIDE * 256), pet, peb, w1t, w1b,
                                    w2big, ones128, k_gain[0:1], cos_c, sin_c)
        y_nsa = _nsa_call(nq, gates, kcmp, vcmp, kv, ovt)

        mix_g = mix_norm_g[l].reshape(4, 256)
        mix_g = mix_g.at[3].set(mix_g[3][nsa_perm])
        wo = w_out[l]
        wo = jnp.concatenate([wo[:768], wo[768:][nsa_perm]], axis=0).astype(BF16)
        x = _outproj_call(x, y_ret, y_s5, y_sgu, y_nsa, mix_g, wo, g1, ts)

        x = _ffn_call(x, norm2_g[l].reshape(1, D), sc2, sh2, g2, ffn_w_up[l].astype(BF16),
                      ffn_conv_w[l], ffn_conv_b[l].reshape(1, -1), ffn_w_down[l].astype(BF16),
                      min(1024, S), 256)
    return x
```

```python
import functools
import math

import jax
import jax.numpy as jnp
import numpy as np
from jax import lax
from jax.experimental import pallas as pl
from jax.experimental.pallas import tpu as pltpu

F32 = jnp.float32
BF16 = jnp.bfloat16

HEAD_DIM = 64
GROUP_WIDTH = 256
RET_CHUNK = 128
S5_GROUPS = 16
S5_CH = 16
S5_STATE = 64
S5_WIDTH = S5_GROUPS * S5_STATE
SGU_GROUPS = 4
SGU_CHUNK = 128
NSA_KV_WIDTH = 128
CMP_STRIDE = 16
CMP_BLOCK = 32
CMP_HIDDEN = 128
SEL_BLOCK = 64
SEL_TOPK = 8
FORCE_BONUS = 1e3
WIN = 512
CONV_W = 3
ROPE_THETA = 10000.0
EPS = 1e-6
NEG = -1e9
QK_SCALE = HEAD_DIM ** -0.5
LOG2E = math.log2(math.e)
SEL_MASK = 2.0 ** 30

ATT_TILE = 256
VMEM_LIMIT = 56 * 1024 * 1024


def _cparams(sem):
    return pltpu.CompilerParams(dimension_semantics=sem, vmem_limit_bytes=VMEM_LIMIT)


def _dot(a, b):
    return jnp.dot(a.astype(BF16), b.astype(BF16), preferred_element_type=F32)


def _dot_nt(a, b):
    return lax.dot_general(a.astype(BF16), b.astype(BF16), (((1,), (1,)), ((), ())),
                           preferred_element_type=F32)


def _dot_split(a, b_bf16):
    hi = a.astype(BF16)
    lo = (a - hi.astype(F32)).astype(BF16)
    return (jnp.dot(hi, b_bf16, preferred_element_type=F32)
            + jnp.dot(lo, b_bf16, preferred_element_type=F32))


def _sigmoid(x):
    return 0.5 + 0.5 * jnp.tanh(0.5 * x)


def _silu(x):
    h = 0.5 * x
    return h + h * jnp.tanh(h)


def _gelu(x):
    c = math.sqrt(2.0 / math.pi)
    return 0.5 * x * (1.0 + jnp.tanh(c * (x + 0.044715 * (x * x * x))))


def _group_rms(x, ones_bd):
    ms = _dot_split(x * x, ones_bd) * (1.0 / HEAD_DIM)
    return lax.rsqrt(ms + EPS)


def _rot_half(x):
    n = x.shape[-1]
    lane = lax.broadcasted_iota(jnp.int32, x.shape, 1)
    first = (lane % HEAD_DIM) < (HEAD_DIM // 2)
    return jnp.where(first, -pltpu.roll(x, n - HEAD_DIM // 2, 1), pltpu.roll(x, HEAD_DIM // 2, 1))


def _rope(x, cos, sin):
    return x * cos + _rot_half(x) * sin


def _lane_group(shape, width):
    return lax.broadcasted_iota(jnp.int32, shape, len(shape) - 1) // width


def _mod_kernel(c_ref, w_ref, b_ref, o_ref):
    o_ref[...] = _dot(_silu(c_ref[...]), w_ref[...]) + b_ref[...]


def _mod_call(c, ada_w, ada_b):
    L, D, D6 = ada_w.shape
    B = c.shape[0]
    nj = D6 // D
    return pl.pallas_call(
        _mod_kernel,
        grid=(L, nj),
        in_specs=[pl.BlockSpec((B, D), lambda l, j: (0, 0)),
                  pl.BlockSpec((None, D, D), lambda l, j: (l, 0, j)),
                  pl.BlockSpec((None, 1, D), lambda l, j: (l, 0, j))],
        out_specs=pl.BlockSpec((None, B, D), lambda l, j: (l, 0, j)),
        out_shape=jax.ShapeDtypeStruct((L, B, D6), F32),
        compiler_params=_cparams(("arbitrary", "arbitrary")),
        name="adaln_mod",
    )(c, ada_w, ada_b.reshape(L, 1, D6))


C_RET = 0
C_S5 = 1024
C_SGU = 1280
C_NQ = 1792
C_KVC = 2048
C_KV = 2304
C_GATE = 2816
N_GATE = 12
N_PROJ = 2944


def _proj_columns():
    cols = list(range(0, 1792))
    nq0 = 1792
    for g in range(2):
        for kvh in range(2):
            base = nq0 + (kvh * 2 + g) * HEAD_DIM
            cols += list(range(base, base + HEAD_DIM))
    cols += list(range(2048, 2816 + N_GATE))
    return np.asarray(cols, dtype=np.int32)


def _gate_expand():
    e = np.zeros((128, 6 * 128), dtype=np.float32)
    for g in range(2):
        for br in range(3):
            for kvh in range(2):
                c0 = (g * 3 + br) * 128 + kvh * HEAD_DIM
                e[(kvh * 2 + g) * 3 + br, c0:c0 + HEAD_DIM] = 1.0
    return jnp.asarray(e, dtype=BF16)


def _inproj_kernel(x_ref, g_ref, sc_ref, sh_ref, w_ref, cos_ref, sin_ref, ones_ref, qg_ref, kg_ref,
                   gexp_ref, ret_ref, s5_ref, sgu_ref, nq_ref, kvc_ref, kv_ref, gate_ref):
    x = x_ref[...]
    ms = jnp.mean(x * x, axis=-1, keepdims=True)
    h = x * lax.rsqrt(ms + EPS) * g_ref[...]
    h = h * (1.0 + sc_ref[...]) + sh_ref[...]
    hb = h.astype(BF16)

    p_all = jnp.dot(hb, w_ref[...], preferred_element_type=F32)

    def proj(c0, n):
        return p_all[:, c0:c0 + n]

    cos = cos_ref[...]
    sin = sin_ref[...]
    cos_h, sin_h = cos[:, :128], sin[:, :128]
    ones256 = ones_ref[...]
    ones128 = ones256[:128, :128]

    ret_ref[:, 0:256] = _rope(proj(C_RET, 256), cos, sin) * QK_SCALE
    ret_ref[:, 256:512] = _rope(proj(C_RET + 256, 256), cos, sin)
    ret_ref[:, 512:1024] = proj(C_RET + 512, 512)
    s5_ref[...] = proj(C_S5, 256)
    sgu_ref[...] = proj(C_SGU, 512)
    q = proj(C_NQ, 256)
    qn = q * _group_rms(q, ones256) * qg_ref[...]
    nq_ref[...] = _rope(qn, cos, sin) * QK_SCALE
    kvc_ref[...] = proj(C_KVC, 256)
    kg = kg_ref[...]
    for i in range(2):
        kv = proj(C_KV + i * 256, 256)
        k = kv[:, 0:128]
        kn = k * _group_rms(k, ones128) * kg[i + 1:i + 2, :]
        kv_ref[:, i * 256:i * 256 + 128] = _rope(kn, cos_h, sin_h)
        kv_ref[:, i * 256 + 128:i * 256 + 256] = kv[:, 128:256]
    gate_ref[...] = _dot_split(_sigmoid(proj(C_GATE, 128)), gexp_ref[...])


def _inproj_call(x, norm_g, sc, sh, w, cos, sin, ones256, q_gain, k_gain, gexp, ts):
    B, S, D = x.shape
    row = lambda n: pl.BlockSpec((None, ts, n), lambda b, s: (b, s, 0))
    const = lambda shape: pl.BlockSpec(shape, lambda b, s: (0,) * len(shape),
                                       pipeline_mode=pl.Buffered(1))
    vec = pl.BlockSpec((None, 1, D), lambda b, s: (b, 0, 0))
    return pl.pallas_call(
        _inproj_kernel,
        grid=(B, S // ts),
        in_specs=[row(D), const((1, D)), vec, vec, const((D, N_PROJ)),
                  pl.BlockSpec((ts, 256), lambda b, s: (s, 0)),
                  pl.BlockSpec((ts, 256), lambda b, s: (s, 0)),
                  const((256, 256)), const((1, 256)), const((3, 128)), const((128, 768))],
        out_specs=[row(1024), pl.BlockSpec((ts, 256), lambda b, s: (s, b)), row(512), row(256),
                   row(256), row(512), row(768)],
        out_shape=[jax.ShapeDtypeStruct((B, S, 1024), F32),
                   jax.ShapeDtypeStruct((S, B * 256), F32),
                   jax.ShapeDtypeStruct((B, S, 512), F32),
                   jax.ShapeDtypeStruct((B, S, 256), F32),
                   jax.ShapeDtypeStruct((B, S, 256), F32),
                   jax.ShapeDtypeStruct((B, S, 512), F32),
                   jax.ShapeDtypeStruct((B, S, 768), F32)],
        compiler_params=_cparams(("arbitrary", "arbitrary")),
        name="in_proj",
    )(x, norm_g, sc, sh, w, cos, sin, ones256, q_gain, k_gain, gexp)


def _retention_kernel(blk_ref, dec_ref, zeta_ref, xi_ref, gbd_ref, bd_ref, ones_ref, ng_ref,
                      o_ref, r_ref):
    @pl.when(pl.program_id(1) == 0)
    def _():
        r_ref[...] = jnp.zeros_like(r_ref)

    C = RET_CHUNK
    head = _lane_group((C, GROUP_WIDTH), HEAD_DIM)
    for ci in range(blk_ref.shape[0] // C):
        rows = slice(ci * C, (ci + 1) * C)
        q = blk_ref[rows, 0:256]
        k = blk_ref[rows, 256:512]
        g = blk_ref[rows, 768:1024]
        kb = k.astype(BF16)
        vb = blk_ref[rows, 512:768].astype(BF16)
        o = jnp.zeros(q.shape, F32)
        for h in range(GROUP_WIDTH // HEAD_DIM):
            qh = jnp.where(head == h, q, 0.0)
            s = _dot_nt(qh, kb) * dec_ref[h]
            o = o + jnp.where(head == h, _dot(s, vb), 0.0)
        r_prev = r_ref[...]
        o = o + _dot(q, r_prev) * xi_ref[...]
        kz_t = (k * zeta_ref[...]).T
        u = _dot(kz_t, vb)
        r_ref[...] = r_prev * gbd_ref[...] + u * bd_ref[...]
        o = o * _group_rms(o, ones_ref[...]) * ng_ref[...]
        o_ref[rows, :] = _silu(g) * o


def _retention_call(ret, dec, zeta, xi, gbd, bd, ones256, norm_g, rows):
    B, S, _ = ret.shape
    C = RET_CHUNK
    const = lambda shape: pl.BlockSpec(shape, lambda b, n: (0,) * len(shape))
    return pl.pallas_call(
        _retention_kernel,
        grid=(B, S // rows),
        in_specs=[pl.BlockSpec((None, rows, 1024), lambda b, n: (b, n, 0)),
                  const((4, C, C)), const((C, 256)), const((C, 256)),
                  const((256, 256)), const((256, 256)), const((256, 256)), const((1, 256))],
        out_specs=pl.BlockSpec((None, rows, 256), lambda b, n: (b, n, 0)),
        out_shape=jax.ShapeDtypeStruct((B, S, 256), F32),
        scratch_shapes=[pltpu.VMEM((256, 256), F32)],
        compiler_params=_cparams(("arbitrary", "arbitrary")),
        name="retention",
    )(ret, dec, zeta, xi, gbd, bd, ones256, norm_g)


def _retention_consts():
    H, C = GROUP_WIDTH // HEAD_DIM, RET_CHUNK
    log_gamma = np.log1p(-(2.0 ** (-5.0 - np.arange(H, dtype=np.float64))))
    idx = np.arange(C, dtype=np.float64)
    diff = idx[:, None] - idx[None, :]
    dec = np.where(diff >= 0, np.exp(log_gamma[:, None, None] * np.maximum(diff, 0.0)), 0.0)
    zeta = np.exp(log_gamma[:, None] * (C - 1 - idx)[None, :])
    xi = np.exp(log_gamma[:, None] * (idx + 1)[None, :])
    gamma_chunk = np.exp(log_gamma * C)
    zeta256 = np.repeat(zeta.T, HEAD_DIM, axis=1)
    xi256 = np.repeat(xi.T, HEAD_DIM, axis=1)
    hid = np.arange(GROUP_WIDTH) // HEAD_DIM
    bd = (hid[:, None] == hid[None, :]).astype(np.float64)
    gbd = bd * gamma_chunk[hid][:, None]
    f = lambda a: jnp.asarray(a, dtype=F32)
    return f(dec), f(zeta256), f(xi256), f(gbd), f(bd)


def _sgu_kernel(blk_ref, w_ref, bias_ref, ones_ref, ng_ref, o_ref):
    C = w_ref.shape[-1]
    grp = _lane_group((C, GROUP_WIDTH), HEAD_DIM)
    tril = (lax.broadcasted_iota(jnp.int32, (C, C), 0) >= lax.broadcasted_iota(jnp.int32, (C, C), 1))
    w_tril = [jnp.where(tril, w_ref[gi], 0.0).astype(BF16) for gi in range(SGU_GROUPS)]
    for ci in range(blk_ref.shape[0] // C):
        rows = slice(ci * C, (ci + 1) * C)
        u = _gelu(blk_ref[rows, 0:256])
        v = _gelu(blk_ref[rows, 256:512])
        vn = v * _group_rms(v, ones_ref[...]) * ng_ref[...]
        vb = vn.astype(BF16)
        s = bias_ref[...]
        for gi in range(SGU_GROUPS):
            s = s + jnp.where(grp == gi, jnp.dot(w_tril[gi], vb, preferred_element_type=F32), 0.0)
        o_ref[rows, :] = u * s


def _sgu_call(sgu, w_s, bias256, ones256, norm_g, rows):
    B, S, _ = sgu.shape
    C = SGU_CHUNK
    const = lambda shape: pl.BlockSpec(shape, lambda b, n: (0,) * len(shape))
    return pl.pallas_call(
        _sgu_kernel,
        grid=(B, S // rows),
        in_specs=[pl.BlockSpec((None, rows, 512), lambda b, n: (b, n, 0)),
                  const((SGU_GROUPS, C, C)), const((C, 256)), const((256, 256)), const((1, 256))],
        out_specs=pl.BlockSpec((None, rows, 256), lambda b, n: (b, n, 0)),
        out_shape=jax.ShapeDtypeStruct((B, S, 256), F32),
        compiler_params=_cparams(("arbitrary", "arbitrary")),
        name="spatial_gating",
    )(sgu, w_s, bias256, ones256, norm_g)


def _s5_prep_kernel(lr_ref, li_ref, ldt_ref, bre_ref, bim_ref, a_ref, bx_ref):
    lr = lr_ref[...]
    li = li_ref[...]
    dts = jnp.exp(ldt_ref[...])
    mag = jnp.exp(lr * dts)
    a_re = mag * jnp.cos(li * dts)
    a_im = mag * jnp.sin(li * dts)
    den = lr * lr + li * li
    f_re = ((a_re - 1.0) * lr + a_im * li) / den
    f_im = (a_im * lr - (a_re - 1.0) * li) / den
    a_ref[0:1, :] = a_re
    a_ref[1:2, :] = a_im
    bre = bre_ref[...]
    bim = bim_ref[...]
    bx_ref[:, 0:S5_WIDTH] = (f_re * bre - f_im * bim).astype(BF16)
    bx_ref[:, S5_WIDTH:2 * S5_WIDTH] = (f_re * bim + f_im * bre).astype(BF16)


def _s5_prep_call(lam_re, lam_im, log_dt, b_re, b_im):
    L = lam_re.shape[0]
    W = S5_WIDTH
    lr = lam_re.reshape(L, 1, W)
    li = lam_im.reshape(L, 1, W)
    ldt = jnp.repeat(log_dt, S5_STATE, axis=-1).reshape(L, 1, W)
    eye = jnp.eye(S5_GROUPS, dtype=F32)

    def dense(b):
        return jnp.einsum('lgph,gk->lghkp', b, eye).reshape(L, S5_GROUPS * S5_CH, W)

    vec = pl.BlockSpec((None, 1, W), lambda l: (l, 0, 0))
    mat = pl.BlockSpec((None, GROUP_WIDTH, W), lambda l: (l, 0, 0))
    return pl.pallas_call(
        _s5_prep_kernel,
        grid=(L,),
        in_specs=[vec, vec, vec, mat, mat],
        out_specs=[pl.BlockSpec((None, 2, W), lambda l: (l, 0, 0)),
                   pl.BlockSpec((None, GROUP_WIDTH, 2 * W), lambda l: (l, 0, 0))],
        out_shape=[jax.ShapeDtypeStruct((L, 2, W), F32),
                   jax.ShapeDtypeStruct((L, GROUP_WIDTH, 2 * W), BF16)],
        compiler_params=_cparams(("arbitrary",)),
        name="s5_discretise",
    )(lr, li, ldt, dense(b_re), dense(b_im))


def _s5_kernel(u_ref, a_ref, bx_ref, cre_ref, cim_ref, d_ref, gw_ref, gb_ref, o_ref,
               hre_ref, him_ref, xre_ref, xim_ref, *, nb, tc):
    @pl.when(pl.program_id(0) == 0)
    def _():
        hre_ref[...] = jnp.zeros_like(hre_ref)
        him_ref[...] = jnp.zeros_like(him_ref)

    u = u_ref[...]
    ub = u.astype(BF16)
    xre_ref[...] = jnp.dot(ub, bx_ref[:, 0:S5_WIDTH], preferred_element_type=F32)
    xim_ref[...] = jnp.dot(ub, bx_ref[:, S5_WIDTH:2 * S5_WIDTH], preferred_element_type=F32)
    a_re = jnp.broadcast_to(a_ref[0:1, :], (nb, S5_WIDTH))
    a_im = jnp.broadcast_to(a_ref[1:2, :], (nb, S5_WIDTH))

    def step(t, carry):
        h_re, h_im = carry
        r0 = pl.multiple_of(t * nb, nb)
        n_re = a_re * h_re - a_im * h_im + xre_ref[pl.ds(r0, nb), :]
        n_im = a_re * h_im + a_im * h_re + xim_ref[pl.ds(r0, nb), :]
        xre_ref[pl.ds(r0, nb), :] = n_re
        xim_ref[pl.ds(r0, nb), :] = n_im
        return n_re, n_im

    h_re, h_im = lax.fori_loop(0, tc, step, (hre_ref[...], him_ref[...]))
    hre_ref[...] = h_re
    him_ref[...] = h_im
    y = _dot(xre_ref[...], cre_ref[...]) - _dot(xim_ref[...], cim_ref[...])
    y = _gelu(y + d_ref[...] * u)
    o_ref[...] = y * _sigmoid(_dot(y, gw_ref[...]) + gb_ref[...])


def _s5_call(u2d, a, bx, cre, cim, d, glu_w, glu_b, nb, tc):
    rows = u2d.shape[0]
    W = S5_WIDTH
    const = lambda shape: pl.BlockSpec(shape, lambda i: (0,) * len(shape))
    return pl.pallas_call(
        functools.partial(_s5_kernel, nb=nb, tc=tc),
        grid=(rows // (tc * nb),),
        in_specs=[pl.BlockSpec((tc * nb, 256), lambda i: (i, 0)),
                  const((2, W)), const((256, 2 * W)), const((W, 256)), const((W, 256)),
                  const((1, 256)), const((256, 256)), const((1, 256))],
        out_specs=pl.BlockSpec((tc * nb, 256), lambda i: (i, 0)),
        out_shape=jax.ShapeDtypeStruct((rows, 256), F32),
        scratch_shapes=[pltpu.VMEM((nb, W), F32), pltpu.VMEM((nb, W), F32),
                        pltpu.VMEM((tc * nb, W), F32), pltpu.VMEM((tc * nb, W), F32)],
        compiler_params=_cparams(("arbitrary",)),
        name="s5_scan",
    )(u2d, a, bx, cre, cim, d, glu_w, glu_b)


def _compress_kernel(hm_ref, pet_ref, peb_ref, w1t_ref, w1b_ref, w2_ref, ones_ref, kg_ref,
                     cos_ref, sin_ref, k_ref, v_ref):
    hm = hm_ref[...]
    a = _dot(hm + pet_ref[...], w1t_ref[...])
    b = _dot(hm + peb_ref[...], w1b_ref[...])
    n = a.shape[0]
    pre = a + pltpu.roll(b, n - 1, 0)
    cmp = _dot(_gelu(pre), w2_ref[...])
    k = cmp[:, 0:128]
    kn = k * _group_rms(k, ones_ref[...]) * kg_ref[...]
    k_ref[...] = _rope(kn, cos_ref[...], sin_ref[...])
    v_ref[...] = cmp[:, 128:256]


def _compress_call(hm, pet, peb, w1t, w1b, w2, ones128, k_gain, cos_c, sin_c):
    B, n, width = hm.shape
    const = lambda shape: pl.BlockSpec(shape, lambda b: (0,) * len(shape))
    out = pl.BlockSpec((None, n, 128), lambda b: (b, 0, 0))
    return pl.pallas_call(
        _compress_kernel,
        grid=(B,),
        in_specs=[pl.BlockSpec((None, n, width), lambda b: (b, 0, 0)),
                  const((1, width)), const((1, width)), const((width, 512)), const((width, 512)),
                  const((512, 256)), const((128, 128)), const((1, 128)),
                  const((n, 128)), const((n, 128))],
        out_specs=[out, out],
        out_shape=[jax.ShapeDtypeStruct((B, n, 128), F32)] * 2,
        compiler_params=_cparams(("arbitrary",)),
        name="nsa_compress",
    )(hm, pet, peb, w1t, w1b, w2, ones128, k_gain, cos_c, sin_c)


def _nsa_kernel(nq_ref, gate_ref, kc_ref, vc_ref, kv_ref, ovt_ref, o_ref,
                s_ref, m_ref, acc_ref, *, n_cmp_rows, win_keys):
    T = ATT_TILE
    qi = pl.program_id(1)
    t0 = qi * T
    pos_c = t0 + lax.broadcasted_iota(jnp.int32, (T, 1), 0)
    pos_r = t0 + lax.broadcasted_iota(jnp.int32, (1, T), 1)
    kvh_lane = _lane_group((T, 128), HEAD_DIM)
    combos = [(g, kvh) for g in range(2) for kvh in range(2)]
    rows = lambda ci: slice(ci * T, (ci + 1) * T)
    q_all = jnp.concatenate(
        [jnp.where(kvh_lane == kvh, nq_ref[:, g * 128:(g + 1) * 128] * LOG2E, 0.0).astype(BF16)
         for g, kvh in combos], axis=0)
    ones_cols = jnp.ones((T, 128), BF16)

    def head_lanes(x):
        return [jnp.where(kvh_lane == 0, x[rows(2 * g)], x[rows(2 * g + 1)]) for g in range(2)]

    w0 = pl.multiple_of(jnp.maximum(t0 + T - win_keys, 0), T)
    dist = pos_c - (w0 + lax.broadcasted_iota(jnp.int32, (1, win_keys), 1))
    ok_w = (dist >= 0) & (dist < WIN)
    s_w = _dot_nt(q_all, kv_ref[pl.ds(w0, win_keys), 256:384])
    vw_ext = jnp.concatenate([kv_ref[pl.ds(w0, win_keys), 384:512].astype(BF16),
                              jnp.ones((win_keys, 128), BF16)], axis=1)
    o_w = []
    for ci in range(4):
        s = jnp.where(ok_w, s_w[rows(ci)], NEG)
        e = jnp.exp2(s - jnp.max(s, axis=-1, keepdims=True))
        ow = jnp.dot(e.astype(BF16), vw_ext, preferred_element_type=F32)
        o_w.append(ow[:, 0:128] * (1.0 / ow[:, 128:256]))
    o_win = head_lanes(jnp.concatenate(o_w, axis=0))

    cmp_end = lax.broadcasted_iota(jnp.int32, (1, n_cmp_rows), 1) * CMP_STRIDE + (CMP_BLOCK - 1)
    valid = cmp_end <= pos_c
    s_all = _dot_nt(q_all, kc_ref[...])
    p_list = []
    for ci in range(4):
        s = jnp.where(valid, s_all[rows(ci)], NEG)
        e = jnp.exp2(s - jnp.max(s, axis=-1, keepdims=True))
        inv = 1.0 / jnp.sum(e, axis=-1, keepdims=True)
        p_list.append(jnp.where(valid, e * inv, 0.0))
    o_cmp = head_lanes(_dot(jnp.concatenate(p_list, axis=0), vc_ref[...]))

    n_sel = ovt_ref.shape[0]
    blk = lax.broadcasted_iota(jnp.int32, (n_sel, T), 0)
    cur = pos_r // SEL_BLOCK
    forced = (blk == 0) | (blk == cur) | (blk == cur - 1)
    causal_blk = blk <= cur
    ovt = ovt_ref[...]
    sel_rows = []
    for kvh in range(2):
        p_kvh = p_list[kvh] + p_list[2 + kvh]
        ph = p_kvh.astype(BF16)
        plo = (p_kvh - ph.astype(F32)).astype(BF16)
        imp = (lax.dot_general(ovt, ph, (((1,), (1,)), ((), ())), preferred_element_type=F32)
               + lax.dot_general(ovt, plo, (((1,), (1,)), ((), ())), preferred_element_type=F32))
        imp = jnp.where(forced, imp + FORCE_BONUS, imp)
        imp = jnp.where(causal_blk, imp, NEG)
        cnt = jnp.zeros((n_sel, T), F32)
        for j in range(n_sel):
            row = imp[j:j + 1, :]
            ahead = (row > imp) | ((row == imp) & (blk > j))
            cnt = cnt + jnp.where(ahead, 1.0, 0.0)
        sel_t = jnp.where((cnt < SEL_TOPK) & causal_blk, 1.0, 0.0)
        sel_t = jnp.concatenate([sel_t, jnp.zeros((128 - n_sel, T), F32)], axis=0)
        sel_rows.append(sel_t.T.astype(BF16))

    tri = (lax.broadcasted_iota(jnp.int32, (T, T), 0) >= lax.broadcasted_iota(jnp.int32, (T, T), 1))
    bias = [((s.astype(F32) - 1.0) * SEL_MASK).astype(BF16) for s in sel_rows]
    q_ext = jnp.concatenate([q_all, jnp.concatenate([bias[kvh] for _, kvh in combos], axis=0)], axis=1)
    key_blk = lax.broadcasted_iota(jnp.int32, (T, 128), 0) // SEL_BLOCK
    lane_id = lax.broadcasted_iota(jnp.int32, (T, 128), 1)
    m_ref[...] = jnp.full(m_ref.shape, NEG, F32)

    def scores(j, diagonal):
        r0 = pl.multiple_of(j * T, T)
        onehot = jnp.where(lane_id == (T // SEL_BLOCK) * j + key_blk, 1.0, 0.0).astype(BF16)
        k_ext = jnp.concatenate([kv_ref[pl.ds(r0, T), 0:128].astype(BF16), onehot], axis=1)
        s = _dot_nt(q_ext, k_ext)
        for ci in range(4):
            sm = jnp.where(tri, s[rows(ci)], NEG) if diagonal else s[rows(ci)]
            s_ref[j, rows(ci), :] = sm
            half = sm[:, 0:128]
            for c in range(1, T // 128):
                half = jnp.maximum(half, sm[:, c * 128:(c + 1) * 128])
            m_ref[rows(ci), :] = jnp.maximum(m_ref[rows(ci), :], half)

    def pass1(jp, carry):
        scores(2 * jp, False)
        scores(2 * jp + 1, False)
        return carry

    lax.fori_loop(0, qi // 2, pass1, 0)

    @pl.when(qi % 2 == 1)
    def _():
        scores(qi - 1, False)

    scores(qi, True)
    m_ref[...] = jnp.broadcast_to(jnp.max(m_ref[...], axis=-1, keepdims=True), m_ref.shape)
    acc_ref[...] = jnp.zeros_like(acc_ref)

    def weighted(j):
        r0 = pl.multiple_of(j * T, T)
        m = m_ref[...]
        p = jnp.exp2(s_ref[j] - jnp.concatenate([m] * (T // 128), axis=1))
        v_ext = jnp.concatenate([kv_ref[pl.ds(r0, T), 128:256].astype(BF16), ones_cols], axis=1)
        return jnp.dot(p.astype(BF16), v_ext, preferred_element_type=F32)

    def pass2(jp, carry):
        acc_ref[...] += weighted(2 * jp) + weighted(2 * jp + 1)
        return carry

    lax.fori_loop(0, (qi + 1) // 2, pass2, 0)

    @pl.when(qi % 2 == 0)
    def _():
        acc_ref[...] += weighted(qi)
    o_sel = head_lanes(acc_ref[:, 0:128] * (1.0 / acc_ref[:, 128:256]))

    for g in range(2):
        gc = gate_ref[:, (g * 3 + 0) * 128:(g * 3 + 1) * 128]
        gs = gate_ref[:, (g * 3 + 1) * 128:(g * 3 + 2) * 128]
        gw = gate_ref[:, (g * 3 + 2) * 128:(g * 3 + 3) * 128]
        o_ref[:, g * 128:(g + 1) * 128] = gc * o_cmp[g] + gs * o_sel[g] + gw * o_win[g]


def _nsa_call(nq, gates, kcmp, vcmp, kv, ovt):
    B, S, _ = nq.shape
    T = ATT_TILE
    n = kcmp.shape[1]
    n_sel = ovt.shape[0]
    win_keys = min(WIN + T, S)
    return pl.pallas_call(
        functools.partial(_nsa_kernel, n_cmp_rows=n, win_keys=win_keys),
        grid=(B, S // T),
        in_specs=[pl.BlockSpec((None, T, 256), lambda b, i: (b, i, 0)),
                  pl.BlockSpec((None, T, 768), lambda b, i: (b, i, 0)),
                  pl.BlockSpec((None, n, 128), lambda b, i: (b, 0, 0)),
                  pl.BlockSpec((None, n, 128), lambda b, i: (b, 0, 0)),
                  pl.BlockSpec((None, S, 512), lambda b, i: (b, 0, 0)),
                  pl.BlockSpec((n_sel, n), lambda b, i: (0, 0))],
        out_specs=pl.BlockSpec((None, T, 256), lambda b, i: (b, i, 0)),
        out_shape=jax.ShapeDtypeStruct((B, S, 256), F32),
        scratch_shapes=[pltpu.VMEM((S // T, 4 * T, T), F32), pltpu.VMEM((4 * T, 128), F32),
                        pltpu.VMEM((4 * T, 256), F32)],
        compiler_params=_cparams(("arbitrary", "arbitrary")),
        name="nsa_attention",
    )(nq, gates, kcmp, vcmp, kv, ovt)


def _outproj_kernel(x_ref, yr_ref, ys_ref, yg_ref, yn_ref, mg_ref, w_ref, g1_ref, o_ref):
    normed = []
    for i, y_ref in enumerate((yr_ref, ys_ref, yg_ref, yn_ref)):
        y = y_ref[...]
        ms = jnp.mean(y * y, axis=-1, keepdims=True)
        normed.append((y * lax.rsqrt(ms + EPS) * mg_ref[i:i + 1, :]).astype(BF16))
    acc = jnp.dot(jnp.concatenate(normed, axis=1), w_ref[...], preferred_element_type=F32)
    o_ref[...] = x_ref[...] + g1_ref[...] * acc


def _outproj_call(x, y_ret, y_s5, y_sgu, y_nsa, mix_g, w_out, g1, ts):
    B, S, D = x.shape
    row = lambda n: pl.BlockSpec((None, ts, n), lambda b, s: (b, s, 0))
    return pl.pallas_call(
        _outproj_kernel,
        grid=(B, S // ts),
        in_specs=[row(D), row(256), pl.BlockSpec((ts, 256), lambda b, s: (s, b)), row(256), row(256),
                  pl.BlockSpec((4, 256), lambda b, s: (0, 0)),
                  pl.BlockSpec((D, D), lambda b, s: (0, 0), pipeline_mode=pl.Buffered(1)),
                  pl.BlockSpec((None, 1, D), lambda b, s: (b, 0, 0))],
        out_specs=row(D),
        out_shape=jax.ShapeDtypeStruct((B, S, D), F32),
        compiler_params=_cparams(("arbitrary", "arbitrary")),
        name="out_proj",
    )(x, y_ret, y_s5, y_sgu, y_nsa, mix_g, w_out, g1)


def _ffn_kernel(x_ref, g_ref, sc_ref, sh_ref, g2_ref, wup_ref, cw_ref, cb_ref, wdn_ref, o_ref,
                tail_ref, z_ref, *, d_ff, tf):
    @pl.when(pl.program_id(1) == 0)
    def _():
        tail_ref[...] = jnp.zeros_like(tail_ref)

    x = x_ref[...]
    ts = x.shape[0]
    ms = jnp.mean(x * x, axis=-1, keepdims=True)
    h = x * lax.rsqrt(ms + EPS) * g_ref[...]
    hb = (h * (1.0 + sc_ref[...]) + sh_ref[...]).astype(BF16)
    row8 = lax.broadcasted_iota(jnp.int32, (8, tf), 0)

    def conv(c0):
        a = jnp.dot(hb, wup_ref[:, c0:c0 + tf], preferred_element_type=F32)
        t6 = tail_ref[6:7, c0:c0 + tf]
        t7 = tail_ref[7:8, c0:c0 + tf]
        a1 = pltpu.roll(a, 1, 0)
        a2 = pltpu.roll(a, 2, 0)
        a1 = jnp.concatenate([jnp.where(row8 == 0, t7, a1[0:8]), a1[8:]], axis=0)
        a2 = jnp.concatenate([jnp.where(row8 == 0, t6, jnp.where(row8 == 1, t7, a2[0:8])), a2[8:]],
                             axis=0)
        tail_ref[:, c0:c0 + tf] = a[ts - 8:ts, :]
        return (cw_ref[2:3, c0:c0 + tf] * a + cw_ref[1:2, c0:c0 + tf] * a1
                + cw_ref[0:1, c0:c0 + tf] * a2 + cb_ref[:, c0:c0 + tf])

    n_chunks = d_ff // tf
    split = (n_chunks // 2) * tf
    acc = None
    for c in range(n_chunks):
        gate = conv(c * tf)
        up = conv(d_ff + c * tf)
        z_ref[:, c * tf:(c + 1) * tf] = (_silu(gate) * up).astype(BF16)
        if (c + 1) * tf == split:
            acc = jnp.dot(z_ref[:, 0:split], wdn_ref[0:split, :], preferred_element_type=F32)
    acc = acc + jnp.dot(z_ref[:, split:d_ff], wdn_ref[split:d_ff, :], preferred_element_type=F32)
    o_ref[...] = x + g2_ref[...] * acc


def _ffn_call(x, norm_g, sc, sh, g2, w_up, conv_w, conv_b, w_down, ts, tf):
    B, S, D = x.shape
    d_ff = w_down.shape[0]
    row = pl.BlockSpec((None, ts, D), lambda b, s: (b, s, 0))
    vec = pl.BlockSpec((None, 1, D), lambda b, s: (b, 0, 0))
    const = lambda shape: pl.BlockSpec(shape, lambda b, s: (0,) * len(shape),
                                       pipeline_mode=pl.Buffered(1))
    return pl.pallas_call(
        functools.partial(_ffn_kernel, d_ff=d_ff, tf=tf),
        grid=(B, S // ts),
        in_specs=[row, const((1, D)), vec, vec, vec, const((D, 2 * d_ff)),
                  const((CONV_W, 2 * d_ff)), const((1, 2 * d_ff)), const((d_ff, D))],
        out_specs=row,
        out_shape=jax.ShapeDtypeStruct((B, S, D), F32),
        scratch_shapes=[pltpu.VMEM((8, 2 * d_ff), F32), pltpu.VMEM((ts, d_ff), BF16)],
        compiler_params=_cparams(("arbitrary", "arbitrary")),
        name="conv_ffn",
    )(x, norm_g, sc, sh, g2, w_up, conv_w, conv_b, w_down)


def _rope_tables(pos):
    inv = ROPE_THETA ** (-jnp.arange(0, HEAD_DIM, 2, dtype=F32) / HEAD_DIM)
    ang = pos.astype(F32)[:, None] * inv[None, :]
    return jnp.cos(ang), jnp.sin(ang)


def _tile_rope(cos, sin, width):
    reps = width // (HEAD_DIM // 2)
    return jnp.tile(cos, (1, reps)), jnp.tile(sin, (1, reps))


def _block_ones(n):
    hid = np.arange(n) // HEAD_DIM
    return jnp.asarray(hid[:, None] == hid[None, :], dtype=BF16)


def _compress_weights(pe, w1, w2):
    half = CMP_STRIDE * HEAD_DIM
    eye2 = jnp.eye(2, dtype=F32)

    def big(w):
        w = w.reshape(2, CMP_STRIDE, HEAD_DIM, CMP_HIDDEN)
        full = jnp.einsum('ktdj,ka,hb->tkhdabj', w, eye2, eye2)
        return full.reshape(CMP_STRIDE * 4 * HEAD_DIM, 4 * CMP_HIDDEN)

    w1t = big(w1[:, :half])
    w1b = big(w1[:, half:])

    def pe_row(p):
        return jnp.broadcast_to(p[:, :, None, :], (2, CMP_STRIDE, 2, HEAD_DIM)).transpose(
            1, 0, 2, 3).reshape(1, CMP_STRIDE * 4 * HEAD_DIM)

    pet = pe_row(pe[:, :CMP_STRIDE])
    peb = pe_row(pe[:, CMP_STRIDE:])
    w2big = jnp.einsum('kjd,ka,hb->khjabd', w2, eye2, eye2).reshape(4 * CMP_HIDDEN, 4 * HEAD_DIM)
    return pet, peb, w1t.astype(BF16), w1b.astype(BF16), w2big.astype(BF16)


def _nsa_perm():
    idx = np.zeros(GROUP_WIDTH, dtype=np.int32)
    for g in range(2):
        for kvh in range(2):
            for d in range(HEAD_DIM):
                idx[g * 128 + kvh * HEAD_DIM + d] = (kvh * 2 + g) * HEAD_DIM + d
    return idx


def kernel(x, c, norm1_g, norm2_g, ada_w, ada_b, w_in, ret_norm_g, s5_lambda_re, s5_lambda_im, s5_log_dt, s5_b_re, s5_b_im, s5_c_re, s5_c_im, s5_d, s5_glu_w, s5_glu_b, sgu_norm_g, sgu_w, sgu_b, nsa_q_norm_g, nsa_k_norm_g, nsa_cmp_pe, nsa_cmp_w1, nsa_cmp_w2, mix_norm_g, w_out, ffn_w_up, ffn_conv_w, ffn_conv_b, ffn_w_down):
    B, S, D = x.shape
    L = w_in.shape[0]
    ts = min(512, S)
    s5_tc = min(64, S)

    cos, sin = _rope_tables(jnp.arange(S))
    cos256, sin256 = _tile_rope(cos, sin, 256)
    n_cmp_rows = S // CMP_STRIDE
    cos_c, sin_c = _rope_tables(jnp.arange(n_cmp_rows) * CMP_STRIDE + CMP_BLOCK - 1)
    cos_c, sin_c = _tile_rope(cos_c, sin_c, 128)
    ones256 = _block_ones(256)
    ones128 = _block_ones(128)
    dec, zeta256, xi256, gbd, bd = _retention_consts()
    n_sel = S // SEL_BLOCK
    ii = np.arange(n_cmp_rows)[None, :]
    jj = np.arange(n_sel)[:, None]
    overlap_t = ((ii * CMP_STRIDE < (jj + 1) * SEL_BLOCK) & (ii * CMP_STRIDE + CMP_BLOCK > jj * SEL_BLOCK)
                 & (ii < n_cmp_rows - 1))
    ovt = jnp.asarray(overlap_t, dtype=BF16)
    proj_cols = _proj_columns()
    gate_expand = _gate_expand()
    nsa_perm = _nsa_perm()

    mod = _mod_call(c, ada_w, ada_b)
    s5_a, s5_bx = _s5_prep_call(s5_lambda_re, s5_lambda_im, s5_log_dt, s5_b_re, s5_b_im)
    eye_g = jnp.eye(S5_GROUPS, dtype=F32)

    for l in range(L):
        sh1, sc1, g1, sh2, sc2, g2 = [mod[l, :, i * D:(i + 1) * D].reshape(B, 1, D) for i in range(6)]
        w = jnp.pad(w_in[l][:, proj_cols], ((0, 0), (0, N_PROJ - proj_cols.size))).astype(BF16)
        q_gain = jnp.tile(nsa_q_norm_g[l], 4).reshape(1, 256)
        k_gain = jnp.tile(nsa_k_norm_g[l], (1, 2))
        ret, s5u, sgu, nq, kvc, kv, gates = _inproj_call(
            x, norm1_g[l].reshape(1, D), sc1, sh1, w, cos256, sin256, ones256, q_gain, k_gain,
            gate_expand, ts)

        y_ret = _retention_call(ret, dec, zeta256, xi256, gbd, bd, ones256,
                                ret_norm_g[l].reshape(1, 256), ts)

        cre = jnp.einsum('ghp,gk->gpkh', s5_c_re[l], eye_g).reshape(S5_WIDTH, GROUP_WIDTH)
        cim = jnp.einsum('ghp,gk->gpkh', s5_c_im[l], eye_g).reshape(S5_WIDTH, GROUP_WIDTH)
        y_s5 = _s5_call(s5u.reshape(S * B, 256), s5_a[l], s5_bx[l], cre.astype(BF16), cim.astype(BF16),
                        s5_d[l].reshape(1, 256), s5_glu_w[l].astype(BF16), s5_glu_b[l].reshape(1, 256),
                        B, s5_tc).reshape(S, B * 256)

        bias256 = jnp.repeat(sgu_b[l].T, HEAD_DIM, axis=1)
        y_sgu = _sgu_call(sgu, sgu_w[l], bias256, ones256, sgu_norm_g[l].reshape(1, 256), ts)

        pet, peb, w1t, w1b, w2big = _compress_weights(nsa_cmp_pe[l], nsa_cmp_w1[l], nsa_cmp_w2[l])
        kcmp, vcmp = _compress_call(kvc.reshape(B, n_cmp_rows, CMP_STRIDE * 256), pet, peb, w1t, w1b,
                                    w2big, ones128, k_gain[0:1], cos_c, sin_c)
        y_nsa = _nsa_call(nq, gates, kcmp, vcmp, kv, ovt)

        mix_g = mix_norm_g[l].reshape(4, 256)
        mix_g = mix_g.at[3].set(mix_g[3][nsa_perm])
        wo = w_out[l]
        wo = jnp.concatenate([wo[:768], wo[768:][nsa_perm]], axis=0).astype(BF16)
        x = _outproj_call(x, y_ret, y_s5, y_sgu, y_nsa, mix_g, wo, g1, ts)

        x = _ffn_call(x, norm2_g[l].reshape(1, D), sc2, sh2, g2, ffn_w_up[l].astype(BF16),
                      ffn_conv_w[l], ffn_conv_b[l].reshape(1, -1), ffn_w_down[l].astype(BF16),
                      min(1024, S), 256)
    return x
```

```python
import functools
import math

import jax
import jax.numpy as jnp
import numpy as np
from jax import lax
from jax.experimental import pallas as pl
from jax.experimental.pallas import tpu as pltpu

F32 = jnp.float32
BF16 = jnp.bfloat16

HEAD_DIM = 64
GROUP_WIDTH = 256
RET_CHUNK = 128
S5_GROUPS = 16
S5_CH = 16
S5_STATE = 64
S5_WIDTH = S5_GROUPS * S5_STATE
SGU_GROUPS = 4
SGU_CHUNK = 128
NSA_KV_WIDTH = 128
CMP_STRIDE = 16
CMP_BLOCK = 32
CMP_HIDDEN = 128
SEL_BLOCK = 64
SEL_TOPK = 8
FORCE_BONUS = 1e3
WIN = 512
CONV_W = 3
ROPE_THETA = 10000.0
EPS = 1e-6
NEG = -1e9
QK_SCALE = HEAD_DIM ** -0.5
LOG2E = math.log2(math.e)
SEL_MASK = 2.0 ** 30

ATT_TILE = 256
VMEM_LIMIT = 56 * 1024 * 1024


def _cparams(sem):
    return pltpu.CompilerParams(dimension_semantics=sem, vmem_limit_bytes=VMEM_LIMIT)


def _dot(a, b):
    return jnp.dot(a.astype(BF16), b.astype(BF16), preferred_element_type=F32)


def _dot_nt(a, b):
    return lax.dot_general(a.astype(BF16), b.astype(BF16), (((1,), (1,)), ((), ())),
                           preferred_element_type=F32)


def _dot_split(a, b_bf16):
    hi = a.astype(BF16)
    lo = (a - hi.astype(F32)).astype(BF16)
    return (jnp.dot(hi, b_bf16, preferred_element_type=F32)
            + jnp.dot(lo, b_bf16, preferred_element_type=F32))


def _sigmoid(x):
    return 0.5 + 0.5 * jnp.tanh(0.5 * x)


def _silu(x):
    h = 0.5 * x
    return h + h * jnp.tanh(h)


def _gelu(x):
    c = math.sqrt(2.0 / math.pi)
    return 0.5 * x * (1.0 + jnp.tanh(c * (x + 0.044715 * (x * x * x))))


def _group_rms(x, ones_bd):
    ms = _dot_split(x * x, ones_bd) * (1.0 / HEAD_DIM)
    return lax.rsqrt(ms + EPS)


def _rot_half(x):
    n = x.shape[-1]
    lane = lax.broadcasted_iota(jnp.int32, x.shape, 1)
    first = (lane % HEAD_DIM) < (HEAD_DIM // 2)
    return jnp.where(first, -pltpu.roll(x, n - HEAD_DIM // 2, 1), pltpu.roll(x, HEAD_DIM // 2, 1))


def _rope(x, cos, sin):
    return x * cos + _rot_half(x) * sin


def _lane_group(shape, width):
    return lax.broadcasted_iota(jnp.int32, shape, len(shape) - 1) // width


def _mod_kernel(c_ref, w_ref, b_ref, o_ref):
    o_ref[...] = _dot(_silu(c_ref[...]), w_ref[...]) + b_ref[...]


def _mod_call(c, ada_w, ada_b):
    L, D, D6 = ada_w.shape
    B = c.shape[0]
    nj = D6 // D
    return pl.pallas_call(
        _mod_kernel,
        grid=(L, nj),
        in_specs=[pl.BlockSpec((B, D), lambda l, j: (0, 0)),
                  pl.BlockSpec((None, D, D), lambda l, j: (l, 0, j)),
                  pl.BlockSpec((None, 1, D), lambda l, j: (l, 0, j))],
        out_specs=pl.BlockSpec((None, B, D), lambda l, j: (l, 0, j)),
        out_shape=jax.ShapeDtypeStruct((L, B, D6), F32),
        compiler_params=_cparams(("arbitrary", "arbitrary")),
        name="adaln_mod",
    )(c, ada_w, ada_b.reshape(L, 1, D6))


C_RET = 0
C_S5 = 1024
C_SGU = 1280
C_NQ = 1792
C_KVC = 2048
C_KV = 2304
C_GATE = 2816
N_GATE = 12
N_PROJ = 2944


def _arrange_w_in(w):
    nq0 = 1792
    heads = [w[:, nq0 + (kvh * 2 + g) * HEAD_DIM:nq0 + (kvh * 2 + g + 1) * HEAD_DIM]
             for g in range(2) for kvh in range(2)]
    pad = jnp.zeros((w.shape[0], N_PROJ - (2816 + N_GATE)), w.dtype)
    return jnp.concatenate([w[:, :nq0]] + heads + [w[:, 2048:2816 + N_GATE], pad], axis=1)


def _gate_expand():
    e = np.zeros((128, 6 * 128), dtype=np.float32)
    for g in range(2):
        for br in range(3):
            for kvh in range(2):
                c0 = (g * 3 + br) * 128 + kvh * HEAD_DIM
                e[(kvh * 2 + g) * 3 + br, c0:c0 + HEAD_DIM] = 1.0
    return jnp.asarray(e, dtype=BF16)


def _inproj_kernel(x_ref, g_ref, sc_ref, sh_ref, w_ref, cos_ref, sin_ref, ones_ref, qg_ref, kg_ref,
                   gexp_ref, ret_ref, s5_ref, sgu_ref, nq_ref, kvc_ref, kv_ref, gate_ref):
    x = x_ref[...]
    ms = jnp.mean(x * x, axis=-1, keepdims=True)
    h = x * lax.rsqrt(ms + EPS) * g_ref[...]
    h = h * (1.0 + sc_ref[...]) + sh_ref[...]
    hb = h.astype(BF16)

    p_all = jnp.dot(hb, w_ref[...], preferred_element_type=F32)

    def proj(c0, n):
        return p_all[:, c0:c0 + n]

    cos = cos_ref[...]
    sin = sin_ref[...]
    cos_h, sin_h = cos[:, :128], sin[:, :128]
    ones256 = ones_ref[...]
    ones128 = ones256[:128, :128]

    ret_ref[:, 0:256] = _rope(proj(C_RET, 256), cos, sin) * QK_SCALE
    ret_ref[:, 256:512] = _rope(proj(C_RET + 256, 256), cos, sin)
    ret_ref[:, 512:1024] = proj(C_RET + 512, 512)
    s5_ref[...] = proj(C_S5, 256)
    sgu_ref[...] = proj(C_SGU, 512)
    q = proj(C_NQ, 256)
    qn = q * _group_rms(q, ones256) * qg_ref[...]
    nq_ref[...] = _rope(qn, cos, sin) * QK_SCALE
    kvc_ref[...] = proj(C_KVC, 256)
    kg = kg_ref[...]
    for i in range(2):
        kv = proj(C_KV + i * 256, 256)
        k = kv[:, 0:128]
        kn = k * _group_rms(k, ones128) * kg[i + 1:i + 2, :]
        kv_ref[:, i * 256:i * 256 + 128] = _rope(kn, cos_h, sin_h)
        kv_ref[:, i * 256 + 128:i * 256 + 256] = kv[:, 128:256]
    gate_ref[...] = _dot_split(_sigmoid(proj(C_GATE, 128)), gexp_ref[...])


def _inproj_call(x, norm_g, sc, sh, w, cos, sin, ones256, q_gain, k_gain, gexp, ts):
    B, S, D = x.shape
    row = lambda n: pl.BlockSpec((None, ts, n), lambda b, s: (b, s, 0))
    const = lambda shape: pl.BlockSpec(shape, lambda b, s: (0,) * len(shape),
                                       pipeline_mode=pl.Buffered(1))
    vec = pl.BlockSpec((None, 1, D), lambda b, s: (b, 0, 0))
    return pl.pallas_call(
        _inproj_kernel,
        grid=(B, S // ts),
        in_specs=[row(D), const((1, D)), vec, vec, const((D, N_PROJ)),
                  pl.BlockSpec((ts, 256), lambda b, s: (s, 0)),
                  pl.BlockSpec((ts, 256), lambda b, s: (s, 0)),
                  const((256, 256)), const((1, 256)), const((3, 128)), const((128, 768))],
        out_specs=[row(1024), pl.BlockSpec((ts, 256), lambda b, s: (s, b)), row(512), row(256),
                   row(256), row(512), row(768)],
        out_shape=[jax.ShapeDtypeStruct((B, S, 1024), F32),
                   jax.ShapeDtypeStruct((S, B * 256), F32),
                   jax.ShapeDtypeStruct((B, S, 512), F32),
                   jax.ShapeDtypeStruct((B, S, 256), F32),
                   jax.ShapeDtypeStruct((B, S, 256), F32),
                   jax.ShapeDtypeStruct((B, S, 512), F32),
                   jax.ShapeDtypeStruct((B, S, 768), F32)],
        compiler_params=_cparams(("arbitrary", "arbitrary")),
        name="in_proj",
    )(x, norm_g, sc, sh, w, cos, sin, ones256, q_gain, k_gain, gexp)


def _retention_kernel(blk_ref, dec_ref, zeta_ref, xi_ref, gbd_ref, bd_ref, ones_ref, ng_ref,
                      o_ref, r_ref):
    @pl.when(pl.program_id(1) == 0)
    def _():
        r_ref[...] = jnp.zeros_like(r_ref)

    C = RET_CHUNK
    head = _lane_group((C, GROUP_WIDTH), HEAD_DIM)
    for ci in range(blk_ref.shape[0] // C):
        rows = slice(ci * C, (ci + 1) * C)
        q = blk_ref[rows, 0:256]
        k = blk_ref[rows, 256:512]
        g = blk_ref[rows, 768:1024]
        kb = k.astype(BF16)
        vb = blk_ref[rows, 512:768].astype(BF16)
        o = jnp.zeros(q.shape, F32)
        for h in range(GROUP_WIDTH // HEAD_DIM):
            qh = jnp.where(head == h, q, 0.0)
            s = _dot_nt(qh, kb) * dec_ref[h]
            o = o + jnp.where(head == h, _dot(s, vb), 0.0)
        r_prev = r_ref[...]
        o = o + _dot(q, r_prev) * xi_ref[...]
        kz_t = (k * zeta_ref[...]).T
        u = _dot(kz_t, vb)
        r_ref[...] = r_prev * gbd_ref[...] + u * bd_ref[...]
        o = o * _group_rms(o, ones_ref[...]) * ng_ref[...]
        o_ref[rows, :] = _silu(g) * o


def _retention_call(ret, dec, zeta, xi, gbd, bd, ones256, norm_g, rows):
    B, S, _ = ret.shape
    C = RET_CHUNK
    const = lambda shape: pl.BlockSpec(shape, lambda b, n: (0,) * len(shape))
    return pl.pallas_call(
        _retention_kernel,
        grid=(B, S // rows),
        in_specs=[pl.BlockSpec((None, rows, 1024), lambda b, n: (b, n, 0)),
                  const((4, C, C)), const((C, 256)), const((C, 256)),
                  const((256, 256)), const((256, 256)), const((256, 256)), const((1, 256))],
        out_specs=pl.BlockSpec((None, rows, 256), lambda b, n: (b, n, 0)),
        out_shape=jax.ShapeDtypeStruct((B, S, 256), F32),
        scratch_shapes=[pltpu.VMEM((256, 256), F32)],
        compiler_params=_cparams(("arbitrary", "arbitrary")),
        name="retention",
    )(ret, dec, zeta, xi, gbd, bd, ones256, norm_g)


def _retention_consts():
    H, C = GROUP_WIDTH // HEAD_DIM, RET_CHUNK
    log_gamma = np.log1p(-(2.0 ** (-5.0 - np.arange(H, dtype=np.float64))))
    idx = np.arange(C, dtype=np.float64)
    diff = idx[:, None] - idx[None, :]
    dec = np.where(diff >= 0, np.exp(log_gamma[:, None, None] * np.maximum(diff, 0.0)), 0.0)
    zeta = np.exp(log_gamma[:, None] * (C - 1 - idx)[None, :])
    xi = np.exp(log_gamma[:, None] * (idx + 1)[None, :])
    gamma_chunk = np.exp(log_gamma * C)
    zeta256 = np.repeat(zeta.T, HEAD_DIM, axis=1)
    xi256 = np.repeat(xi.T, HEAD_DIM, axis=1)
    hid = np.arange(GROUP_WIDTH) // HEAD_DIM
    bd = (hid[:, None] == hid[None, :]).astype(np.float64)
    gbd = bd * gamma_chunk[hid][:, None]
    f = lambda a: jnp.asarray(a, dtype=F32)
    return f(dec), f(zeta256), f(xi256), f(gbd), f(bd)


def _sgu_kernel(blk_ref, w_ref, bias_ref, ones_ref, ng_ref, o_ref):
    C = w_ref.shape[-1]
    grp = _lane_group((C, GROUP_WIDTH), HEAD_DIM)
    tril = (lax.broadcasted_iota(jnp.int32, (C, C), 0) >= lax.broadcasted_iota(jnp.int32, (C, C), 1))
    w_tril = [jnp.where(tril, w_ref[gi], 0.0).astype(BF16) for gi in range(SGU_GROUPS)]
    for ci in range(blk_ref.shape[0] // C):
        rows = slice(ci * C, (ci + 1) * C)
        u = _gelu(blk_ref[rows, 0:256])
        v = _gelu(blk_ref[rows, 256:512])
        vn = v * _group_rms(v, ones_ref[...]) * ng_ref[...]
        vb = vn.astype(BF16)
        s = bias_ref[...]
        for gi in range(SGU_GROUPS):
            s = s + jnp.where(grp == gi, jnp.dot(w_tril[gi], vb, preferred_element_type=F32), 0.0)
        o_ref[rows, :] = u * s


def _sgu_call(sgu, w_s, bias256, ones256, norm_g, rows):
    B, S, _ = sgu.shape
    C = SGU_CHUNK
    const = lambda shape: pl.BlockSpec(shape, lambda b, n: (0,) * len(shape))
    return pl.pallas_call(
        _sgu_kernel,
        grid=(B, S // rows),
        in_specs=[pl.BlockSpec((None, rows, 512), lambda b, n: (b, n, 0)),
                  const((SGU_GROUPS, C, C)), const((C, 256)), const((256, 256)), const((1, 256))],
        out_specs=pl.BlockSpec((None, rows, 256), lambda b, n: (b, n, 0)),
        out_shape=jax.ShapeDtypeStruct((B, S, 256), F32),
        compiler_params=_cparams(("arbitrary", "arbitrary")),
        name="spatial_gating",
    )(sgu, w_s, bias256, ones256, norm_g)


def _s5_prep_kernel(lr_ref, li_ref, ldt_ref, bre_ref, bim_ref, a_ref, bx_ref):
    lr = lr_ref[...]
    li = li_ref[...]
    dts = jnp.exp(ldt_ref[...])
    mag = jnp.exp(lr * dts)
    a_re = mag * jnp.cos(li * dts)
    a_im = mag * jnp.sin(li * dts)
    den = lr * lr + li * li
    f_re = ((a_re - 1.0) * lr + a_im * li) / den
    f_im = (a_im * lr - (a_re - 1.0) * li) / den
    a_ref[0:1, :] = a_re
    a_ref[1:2, :] = a_im
    bre = bre_ref[...]
    bim = bim_ref[...]
    bx_ref[:, 0:S5_WIDTH] = (f_re * bre - f_im * bim).astype(BF16)
    bx_ref[:, S5_WIDTH:2 * S5_WIDTH] = (f_re * bim + f_im * bre).astype(BF16)


def _s5_prep_call(lam_re, lam_im, log_dt, b_re, b_im):
    L = lam_re.shape[0]
    W = S5_WIDTH
    lr = lam_re.reshape(L, 1, W)
    li = lam_im.reshape(L, 1, W)
    ldt = jnp.repeat(log_dt, S5_STATE, axis=-1).reshape(L, 1, W)
    eye = jnp.eye(S5_GROUPS, dtype=F32)

    def dense(b):
        return jnp.einsum('lgph,gk->lghkp', b, eye).reshape(L, S5_GROUPS * S5_CH, W)

    vec = pl.BlockSpec((None, 1, W), lambda l: (l, 0, 0))
    mat = pl.BlockSpec((None, GROUP_WIDTH, W), lambda l: (l, 0, 0))
    return pl.pallas_call(
        _s5_prep_kernel,
        grid=(L,),
        in_specs=[vec, vec, vec, mat, mat],
        out_specs=[pl.BlockSpec((None, 2, W), lambda l: (l, 0, 0)),
                   pl.BlockSpec((None, GROUP_WIDTH, 2 * W), lambda l: (l, 0, 0))],
        out_shape=[jax.ShapeDtypeStruct((L, 2, W), F32),
                   jax.ShapeDtypeStruct((L, GROUP_WIDTH, 2 * W), BF16)],
        compiler_params=_cparams(("arbitrary",)),
        name="s5_discretise",
    )(lr, li, ldt, dense(b_re), dense(b_im))


def _split_bf16(x):
    hi = x.astype(BF16)
    return hi, (x - hi.astype(F32)).astype(BF16)


def _s5_kernel(u_ref, a_ref, bx_ref, cre_ref, cim_ref, d_ref, gw_ref, gb_ref, pf_ref, pb_ref, o_ref,
               hre_ref, him_ref, xre_ref, xim_ref, utb_ref, *, nb, tc, sub):
    @pl.when(pl.program_id(0) == 0)
    def _():
        hre_ref[...] = jnp.zeros_like(hre_ref)
        him_ref[...] = jnp.zeros_like(him_ref)

    blk = sub * nb
    for sb in range(tc // sub):
        t_rows = slice(sb * sub, (sb + 1) * sub)
        u_bt = jnp.concatenate([u_ref[t_rows, b * 256:(b + 1) * 256] for b in range(nb)], axis=0)
        hi, lo = _split_bf16(u_bt)
        utb_ref[sb * blk:(sb + 1) * blk, :] = (
            jnp.dot(pf_ref[...], hi, preferred_element_type=F32)
            + jnp.dot(pf_ref[...], lo, preferred_element_type=F32))
    u = utb_ref[...]
    ub = u.astype(BF16)
    xre_ref[...] = jnp.dot(ub, bx_ref[:, 0:S5_WIDTH], preferred_element_type=F32)
    xim_ref[...] = jnp.dot(ub, bx_ref[:, S5_WIDTH:2 * S5_WIDTH], preferred_element_type=F32)
    a_re = jnp.broadcast_to(a_ref[0:1, :], (nb, S5_WIDTH))
    a_im = jnp.broadcast_to(a_ref[1:2, :], (nb, S5_WIDTH))

    def step(t, carry):
        h_re, h_im = carry
        r0 = pl.multiple_of(t * nb, nb)
        n_re = a_re * h_re - a_im * h_im + xre_ref[pl.ds(r0, nb), :]
        n_im = a_re * h_im + a_im * h_re + xim_ref[pl.ds(r0, nb), :]
        xre_ref[pl.ds(r0, nb), :] = n_re
        xim_ref[pl.ds(r0, nb), :] = n_im
        return n_re, n_im

    h_re, h_im = lax.fori_loop(0, tc, step, (hre_ref[...], him_ref[...]))
    hre_ref[...] = h_re
    him_ref[...] = h_im
    y = _dot(xre_ref[...], cre_ref[...]) - _dot(xim_ref[...], cim_ref[...])
    y = _gelu(y + d_ref[...] * u)
    y = y * _sigmoid(_dot(y, gw_ref[...]) + gb_ref[...])
    for sb in range(tc // sub):
        hi, lo = _split_bf16(y[sb * blk:(sb + 1) * blk, :])
        y_bt = (jnp.dot(pb_ref[...], hi, preferred_element_type=F32)
                + jnp.dot(pb_ref[...], lo, preferred_element_type=F32))
        for b in range(nb):
            o_ref[sb * sub:(sb + 1) * sub, b * 256:(b + 1) * 256] = y_bt[b * sub:(b + 1) * sub, :]


def _s5_perms(nb, sub):
    n = nb * sub
    fwd = np.zeros((n, n), dtype=np.float32)
    for t in range(sub):
        for b in range(nb):
            fwd[t * nb + b, b * sub + t] = 1.0
    return jnp.asarray(fwd, dtype=BF16), jnp.asarray(fwd.T, dtype=BF16)


def _s5_call(u, a, bx, cre, cim, d, glu_w, glu_b, nb, tc):
    S = u.shape[0]
    W = S5_WIDTH
    sub = min(tc, max(8, 256 // nb))
    perm_f, perm_b = _s5_perms(nb, sub)
    const = lambda shape: pl.BlockSpec(shape, lambda i: (0,) * len(shape))
    return pl.pallas_call(
        functools.partial(_s5_kernel, nb=nb, tc=tc, sub=sub),
        grid=(S // tc,),
        in_specs=[pl.BlockSpec((tc, nb * 256), lambda i: (i, 0)),
                  const((2, W)), const((256, 2 * W)), const((W, 256)), const((W, 256)),
                  const((1, 256)), const((256, 256)), const((1, 256)),
                  const((nb * sub, nb * sub)), const((nb * sub, nb * sub))],
        out_specs=pl.BlockSpec((tc, nb * 256), lambda i: (i, 0)),
        out_shape=jax.ShapeDtypeStruct((S, nb * 256), F32),
        scratch_shapes=[pltpu.VMEM((nb, W), F32), pltpu.VMEM((nb, W), F32),
                        pltpu.VMEM((tc * nb, W), F32), pltpu.VMEM((tc * nb, W), F32),
                        pltpu.VMEM((tc * nb, 256), F32)],
        compiler_params=_cparams(("arbitrary",)),
        name="s5_scan",
    )(u, a, bx, cre, cim, d, glu_w, glu_b, perm_f, perm_b)


def _compress_kernel(hm_ref, pet_ref, peb_ref, w1t_ref, w1b_ref, w2_ref, ones_ref, kg_ref,
                     cos_ref, sin_ref, k_ref, v_ref):
    hm = hm_ref[...]
    a = _dot(hm + pet_ref[...], w1t_ref[...])
    b = _dot(hm + peb_ref[...], w1b_ref[...])
    n = a.shape[0]
    pre = a + pltpu.roll(b, n - 1, 0)
    cmp = _dot(_gelu(pre), w2_ref[...])
    k = cmp[:, 0:128]
    kn = k * _group_rms(k, ones_ref[...]) * kg_ref[...]
    k_ref[...] = _rope(kn, cos_ref[...], sin_ref[...])
    v_ref[...] = cmp[:, 128:256]


def _compress_call(hm, pet, peb, w1t, w1b, w2, ones128, k_gain, cos_c, sin_c):
    B, n, width = hm.shape
    const = lambda shape: pl.BlockSpec(shape, lambda b: (0,) * len(shape))
    out = pl.BlockSpec((None, n, 128), lambda b: (b, 0, 0))
    return pl.pallas_call(
        _compress_kernel,
        grid=(B,),
        in_specs=[pl.BlockSpec((None, n, width), lambda b: (b, 0, 0)),
                  const((1, width)), const((1, width)), const((width, 512)), const((width, 512)),
                  const((512, 256)), const((128, 128)), const((1, 128)),
                  const((n, 128)), const((n, 128))],
        out_specs=[out, out],
        out_shape=[jax.ShapeDtypeStruct((B, n, 128), F32)] * 2,
        compiler_params=_cparams(("arbitrary",)),
        name="nsa_compress",
    )(hm, pet, peb, w1t, w1b, w2, ones128, k_gain, cos_c, sin_c)


def _nsa_kernel(nq_ref, gate_ref, kc_ref, vc_ref, kv_ref, ovt_ref, o_ref,
                s_ref, m_ref, acc_ref, *, n_cmp_rows, win_keys):
    T = ATT_TILE
    qi = pl.program_id(1)
    t0 = qi * T
    pos_c = t0 + lax.broadcasted_iota(jnp.int32, (T, 1), 0)
    pos_r = t0 + lax.broadcasted_iota(jnp.int32, (1, T), 1)
    kvh_lane = _lane_group((T, 128), HEAD_DIM)
    combos = [(g, kvh) for g in range(2) for kvh in range(2)]
    rows = lambda ci: slice(ci * T, (ci + 1) * T)
    q_all = jnp.concatenate(
        [jnp.where(kvh_lane == kvh, nq_ref[:, g * 128:(g + 1) * 128] * LOG2E, 0.0).astype(BF16)
         for g, kvh in combos], axis=0)
    ones_cols = jnp.ones((T, 128), BF16)

    def head_lanes(x):
        return [jnp.where(kvh_lane == 0, x[rows(2 * g)], x[rows(2 * g + 1)]) for g in range(2)]

    w0 = pl.multiple_of(jnp.maximum(t0 + T - win_keys, 0), T)
    dist = pos_c - (w0 + lax.broadcasted_iota(jnp.int32, (1, win_keys), 1))
    ok_w = (dist >= 0) & (dist < WIN)
    s_w = _dot_nt(q_all, kv_ref[pl.ds(w0, win_keys), 256:384])
    vw_ext = jnp.concatenate([kv_ref[pl.ds(w0, win_keys), 384:512].astype(BF16),
                              jnp.ones((win_keys, 128), BF16)], axis=1)
    o_w = []
    for ci in range(4):
        s = jnp.where(ok_w, s_w[rows(ci)], NEG)
        e = jnp.exp2(s - jnp.max(s, axis=-1, keepdims=True))
        ow = jnp.dot(e.astype(BF16), vw_ext, preferred_element_type=F32)
        o_w.append(ow[:, 0:128] * (1.0 / ow[:, 128:256]))
    o_win = head_lanes(jnp.concatenate(o_w, axis=0))

    cmp_end = lax.broadcasted_iota(jnp.int32, (1, n_cmp_rows), 1) * CMP_STRIDE + (CMP_BLOCK - 1)
    valid = cmp_end <= pos_c
    s_all = _dot_nt(q_all, kc_ref[...])
    p_list = []
    for ci in range(4):
        s = jnp.where(valid, s_all[rows(ci)], NEG)
        e = jnp.exp2(s - jnp.max(s, axis=-1, keepdims=True))
        inv = 1.0 / jnp.sum(e, axis=-1, keepdims=True)
        p_list.append(jnp.where(valid, e * inv, 0.0))
    o_cmp = head_lanes(_dot(jnp.concatenate(p_list, axis=0), vc_ref[...]))

    n_sel = ovt_ref.shape[0]
    blk = lax.broadcasted_iota(jnp.int32, (n_sel, T), 0)
    cur = pos_r // SEL_BLOCK
    forced = (blk == 0) | (blk == cur) | (blk == cur - 1)
    causal_blk = blk <= cur
    ovt = ovt_ref[...]
    sel_rows = []
    for kvh in range(2):
        p_kvh = p_list[kvh] + p_list[2 + kvh]
        ph = p_kvh.astype(BF16)
        plo = (p_kvh - ph.astype(F32)).astype(BF16)
        imp = (lax.dot_general(ovt, ph, (((1,), (1,)), ((), ())), preferred_element_type=F32)
               + lax.dot_general(ovt, plo, (((1,), (1,)), ((), ())), preferred_element_type=F32))
        imp = jnp.where(forced, imp + FORCE_BONUS, imp)
        imp = jnp.where(causal_blk, imp, NEG)
        cnt = jnp.zeros((n_sel, T), F32)
        for j in range(n_sel):
            row = imp[j:j + 1, :]
            ahead = (row > imp) | ((row == imp) & (blk > j))
            cnt = cnt + jnp.where(ahead, 1.0, 0.0)
        sel_t = jnp.where((cnt < SEL_TOPK) & causal_blk, 1.0, 0.0)
        sel_t = jnp.concatenate([sel_t, jnp.zeros((128 - n_sel, T), F32)], axis=0)
        sel_rows.append(sel_t.T.astype(BF16))

    tri = (lax.broadcasted_iota(jnp.int32, (T, T), 0) >= lax.broadcasted_iota(jnp.int32, (T, T), 1))
    bias = [((s.astype(F32) - 1.0) * SEL_MASK).astype(BF16) for s in sel_rows]
    q_ext = jnp.concatenate([q_all, jnp.concatenate([bias[kvh] for _, kvh in combos], axis=0)], axis=1)
    key_blk = lax.broadcasted_iota(jnp.int32, (T, 128), 0) // SEL_BLOCK
    lane_id = lax.broadcasted_iota(jnp.int32, (T, 128), 1)
    m_ref[...] = jnp.full(m_ref.shape, NEG, F32)

    def scores(j, diagonal):
        r0 = pl.multiple_of(j * T, T)
        onehot = jnp.where(lane_id == (T // SEL_BLOCK) * j + key_blk, 1.0, 0.0).astype(BF16)
        k_ext = jnp.concatenate([kv_ref[pl.ds(r0, T), 0:128].astype(BF16), onehot], axis=1)
        s = _dot_nt(q_ext, k_ext)
        for ci in range(4):
            sm = jnp.where(tri, s[rows(ci)], NEG) if diagonal else s[rows(ci)]
            s_ref[j, rows(ci), :] = sm
            half = sm[:, 0:128]
            for c in range(1, T // 128):
                half = jnp.maximum(half, sm[:, c * 128:(c + 1) * 128])
            m_ref[rows(ci), :] = jnp.maximum(m_ref[rows(ci), :], half)

    def pass1(jp, carry):
        scores(2 * jp, False)
        scores(2 * jp + 1, False)
        return carry

    lax.fori_loop(0, qi // 2, pass1, 0)

    @pl.when(qi % 2 == 1)
    def _():
        scores(qi - 1, False)

    scores(qi, True)
    m_ref[...] = jnp.broadcast_to(jnp.max(m_ref[...], axis=-1, keepdims=True), m_ref.shape)
    acc_ref[...] = jnp.zeros_like(acc_ref)

    def weighted(j):
        r0 = pl.multiple_of(j * T, T)
        m = m_ref[...]
        p = jnp.exp2(s_ref[j] - jnp.concatenate([m] * (T // 128), axis=1))
        v_ext = jnp.concatenate([kv_ref[pl.ds(r0, T), 128:256].astype(BF16), ones_cols], axis=1)
        return jnp.dot(p.astype(BF16), v_ext, preferred_element_type=F32)

    def pass2(jp, carry):
        acc_ref[...] += weighted(2 * jp) + weighted(2 * jp + 1)
        return carry

    lax.fori_loop(0, (qi + 1) // 2, pass2, 0)

    @pl.when(qi % 2 == 0)
    def _():
        acc_ref[...] += weighted(qi)
    o_sel = head_lanes(acc_ref[:, 0:128] * (1.0 / acc_ref[:, 128:256]))

    for g in range(2):
        gc = gate_ref[:, (g * 3 + 0) * 128:(g * 3 + 1) * 128]
        gs = gate_ref[:, (g * 3 + 1) * 128:(g * 3 + 2) * 128]
        gw = gate_ref[:, (g * 3 + 2) * 128:(g * 3 + 3) * 128]
        o_ref[:, g * 128:(g + 1) * 128] = gc * o_cmp[g] + gs * o_sel[g] + gw * o_win[g]


def _nsa_call(nq, gates, kcmp, vcmp, kv, ovt):
    B, S, _ = nq.shape
    T = ATT_TILE
    n = kcmp.shape[1]
    n_sel = ovt.shape[0]
    win_keys = min(WIN + T, S)
    return pl.pallas_call(
        functools.partial(_nsa_kernel, n_cmp_rows=n, win_keys=win_keys),
        grid=(B, S // T),
        in_specs=[pl.BlockSpec((None, T, 256), lambda b, i: (b, i, 0)),
                  pl.BlockSpec((None, T, 768), lambda b, i: (b, i, 0)),
                  pl.BlockSpec((None, n, 128), lambda b, i: (b, 0, 0)),
                  pl.BlockSpec((None, n, 128), lambda b, i: (b, 0, 0)),
                  pl.BlockSpec((None, S, 512), lambda b, i: (b, 0, 0)),
                  pl.BlockSpec((n_sel, n), lambda b, i: (0, 0))],
        out_specs=pl.BlockSpec((None, T, 256), lambda b, i: (b, i, 0)),
        out_shape=jax.ShapeDtypeStruct((B, S, 256), F32),
        scratch_shapes=[pltpu.VMEM((S // T, 4 * T, T), F32), pltpu.VMEM((4 * T, 128), F32),
                        pltpu.VMEM((4 * T, 256), F32)],
        compiler_params=_cparams(("arbitrary", "arbitrary")),
        name="nsa_attention",
    )(nq, gates, kcmp, vcmp, kv, ovt)


def _mix_ffn_kernel(x_ref, yr_ref, ys_ref, yg_ref, yn_ref, mg_ref, wo_ref, g1_ref,
                    g_ref, sc_ref, sh_ref, g2_ref, wup_ref, cw_ref, cb_ref, wdn_ref, o_ref,
                    tail_ref, z_ref, *, d_ff, tf):
    @pl.when(pl.program_id(1) == 0)
    def _():
        tail_ref[...] = jnp.zeros_like(tail_ref)

    normed = []
    for i, y_ref in enumerate((yr_ref, ys_ref, yg_ref, yn_ref)):
        y = y_ref[...]
        ms = jnp.mean(y * y, axis=-1, keepdims=True)
        normed.append((y * lax.rsqrt(ms + EPS) * mg_ref[i:i + 1, :]).astype(BF16))
    mixed = jnp.dot(jnp.concatenate(normed, axis=1), wo_ref[...], preferred_element_type=F32)
    x = x_ref[...] + g1_ref[...] * mixed

    ts = x.shape[0]
    ms = jnp.mean(x * x, axis=-1, keepdims=True)
    h = x * lax.rsqrt(ms + EPS) * g_ref[...]
    hb = (h * (1.0 + sc_ref[...]) + sh_ref[...]).astype(BF16)
    row8 = lax.broadcasted_iota(jnp.int32, (8, tf), 0)

    def conv(c0):
        a = jnp.dot(hb, wup_ref[:, c0:c0 + tf], preferred_element_type=F32)
        t6 = tail_ref[6:7, c0:c0 + tf]
        t7 = tail_ref[7:8, c0:c0 + tf]
        a1 = pltpu.roll(a, 1, 0)
        a2 = pltpu.roll(a, 2, 0)
        a1 = jnp.concatenate([jnp.where(row8 == 0, t7, a1[0:8]), a1[8:]], axis=0)
        a2 = jnp.concatenate([jnp.where(row8 == 0, t6, jnp.where(row8 == 1, t7, a2[0:8])), a2[8:]],
                             axis=0)
        tail_ref[:, c0:c0 + tf] = a[ts - 8:ts, :]
        return (cw_ref[2:3, c0:c0 + tf] * a + cw_ref[1:2, c0:c0 + tf] * a1
                + cw_ref[0:1, c0:c0 + tf] * a2 + cb_ref[:, c0:c0 + tf])

    n_chunks = d_ff // tf
    split = (n_chunks // 2) * tf
    acc = None
    for c in range(n_chunks):
        gate = conv(c * tf)
        up = conv(d_ff + c * tf)
        z_ref[:, c * tf:(c + 1) * tf] = (_silu(gate) * up).astype(BF16)
        if (c + 1) * tf == split:
            acc = jnp.dot(z_ref[:, 0:split], wdn_ref[0:split, :], preferred_element_type=F32)
    acc = acc + jnp.dot(z_ref[:, split:d_ff], wdn_ref[split:d_ff, :], preferred_element_type=F32)
    o_ref[...] = x + g2_ref[...] * acc


def _mix_ffn_call(x, y_ret, y_s5, y_sgu, y_nsa, mix_g, w_out, g1,
                  norm_g, sc, sh, g2, w_up, conv_w, conv_b, w_down, ts, tf):
    B, S, D = x.shape
    d_ff = w_down.shape[0]
    row = lambda n: pl.BlockSpec((None, ts, n), lambda b, s: (b, s, 0))
    vec = pl.BlockSpec((None, 1, D), lambda b, s: (b, 0, 0))
    const = lambda shape: pl.BlockSpec(shape, lambda b, s: (0,) * len(shape),
                                       pipeline_mode=pl.Buffered(1))
    return pl.pallas_call(
        functools.partial(_mix_ffn_kernel, d_ff=d_ff, tf=tf),
        grid=(B, S // ts),
        in_specs=[row(D), row(256), pl.BlockSpec((ts, 256), lambda b, s: (s, b)), row(256), row(256),
                  const((4, 256)), const((D, D)), vec,
                  const((1, D)), vec, vec, vec, const((D, 2 * d_ff)),
                  const((CONV_W, 2 * d_ff)), const((1, 2 * d_ff)), const((d_ff, D))],
        out_specs=row(D),
        out_shape=jax.ShapeDtypeStruct((B, S, D), F32),
        scratch_shapes=[pltpu.VMEM((8, 2 * d_ff), F32), pltpu.VMEM((ts, d_ff), BF16)],
        compiler_params=_cparams(("arbitrary", "arbitrary")),
        name="mix_ffn",
    )(x, y_ret, y_s5, y_sgu, y_nsa, mix_g, w_out, g1, norm_g, sc, sh, g2, w_up, conv_w, conv_b, w_down)


def _rope_tables(pos):
    inv = ROPE_THETA ** (-jnp.arange(0, HEAD_DIM, 2, dtype=F32) / HEAD_DIM)
    ang = pos.astype(F32)[:, None] * inv[None, :]
    return jnp.cos(ang), jnp.sin(ang)


def _tile_rope(cos, sin, width):
    reps = width // (HEAD_DIM // 2)
    return jnp.tile(cos, (1, reps)), jnp.tile(sin, (1, reps))


def _block_ones(n):
    hid = np.arange(n) // HEAD_DIM
    return jnp.asarray(hid[:, None] == hid[None, :], dtype=BF16)


def _compress_weights(pe, w1, w2):
    half = CMP_STRIDE * HEAD_DIM
    eye2 = jnp.eye(2, dtype=F32)

    def big(w):
        w = w.reshape(2, CMP_STRIDE, HEAD_DIM, CMP_HIDDEN)
        full = jnp.einsum('ktdj,ka,hb->tkhdabj', w, eye2, eye2)
        return full.reshape(CMP_STRIDE * 4 * HEAD_DIM, 4 * CMP_HIDDEN)

    w1t = big(w1[:, :half])
    w1b = big(w1[:, half:])

    def pe_row(p):
        return jnp.broadcast_to(p[:, :, None, :], (2, CMP_STRIDE, 2, HEAD_DIM)).transpose(
            1, 0, 2, 3).reshape(1, CMP_STRIDE * 4 * HEAD_DIM)

    pet = pe_row(pe[:, :CMP_STRIDE])
    peb = pe_row(pe[:, CMP_STRIDE:])
    w2big = jnp.einsum('kjd,ka,hb->khjabd', w2, eye2, eye2).reshape(4 * CMP_HIDDEN, 4 * HEAD_DIM)
    return pet, peb, w1t.astype(BF16), w1b.astype(BF16), w2big.astype(BF16)


def _nsa_perm():
    idx = np.zeros(GROUP_WIDTH, dtype=np.int32)
    for g in range(2):
        for kvh in range(2):
            for d in range(HEAD_DIM):
                idx[g * 128 + kvh * HEAD_DIM + d] = (kvh * 2 + g) * HEAD_DIM + d
    return idx


def kernel(x, c, norm1_g, norm2_g, ada_w, ada_b, w_in, ret_norm_g, s5_lambda_re, s5_lambda_im, s5_log_dt, s5_b_re, s5_b_im, s5_c_re, s5_c_im, s5_d, s5_glu_w, s5_glu_b, sgu_norm_g, sgu_w, sgu_b, nsa_q_norm_g, nsa_k_norm_g, nsa_cmp_pe, nsa_cmp_w1, nsa_cmp_w2, mix_norm_g, w_out, ffn_w_up, ffn_conv_w, ffn_conv_b, ffn_w_down):
    B, S, D = x.shape
    L = w_in.shape[0]
    ts = min(512, S)
    s5_tc = min(64, S)

    cos, sin = _rope_tables(jnp.arange(S))
    cos256, sin256 = _tile_rope(cos, sin, 256)
    n_cmp_rows = S // CMP_STRIDE
    cos_c, sin_c = _rope_tables(jnp.arange(n_cmp_rows) * CMP_STRIDE + CMP_BLOCK - 1)
    cos_c, sin_c = _tile_rope(cos_c, sin_c, 128)
    ones256 = _block_ones(256)
    ones128 = _block_ones(128)
    dec, zeta256, xi256, gbd, bd = _retention_consts()
    n_sel = S // SEL_BLOCK
    ii = np.arange(n_cmp_rows)[None, :]
    jj = np.arange(n_sel)[:, None]
    overlap_t = ((ii * CMP_STRIDE < (jj + 1) * SEL_BLOCK) & (ii * CMP_STRIDE + CMP_BLOCK > jj * SEL_BLOCK)
                 & (ii < n_cmp_rows - 1))
    ovt = jnp.asarray(overlap_t, dtype=BF16)
    gate_expand = _gate_expand()
    nsa_perm = _nsa_perm()

    mod = _mod_call(c, ada_w, ada_b)
    s5_a, s5_bx = _s5_prep_call(s5_lambda_re, s5_lambda_im, s5_log_dt, s5_b_re, s5_b_im)
    eye_g = jnp.eye(S5_GROUPS, dtype=F32)

    for l in range(L):
        sh1, sc1, g1, sh2, sc2, g2 = [mod[l, :, i * D:(i + 1) * D].reshape(B, 1, D) for i in range(6)]
        w = _arrange_w_in(w_in[l]).astype(BF16)
        q_gain = jnp.tile(nsa_q_norm_g[l], 4).reshape(1, 256)
        k_gain = jnp.tile(nsa_k_norm_g[l], (1, 2))
        ret, s5u, sgu, nq, kvc, kv, gates = _inproj_call(
            x, norm1_g[l].reshape(1, D), sc1, sh1, w, cos256, sin256, ones256, q_gain, k_gain,
            gate_expand, ts)

        y_ret = _retention_call(ret, dec, zeta256, xi256, gbd, bd, ones256,
                                ret_norm_g[l].reshape(1, 256), ts)

        cre = jnp.einsum('ghp,gk->gpkh', s5_c_re[l], eye_g).reshape(S5_WIDTH, GROUP_WIDTH)
        cim = jnp.einsum('ghp,gk->gpkh', s5_c_im[l], eye_g).reshape(S5_WIDTH, GROUP_WIDTH)
        y_s5 = _s5_call(s5u, s5_a[l], s5_bx[l], cre.astype(BF16), cim.astype(BF16),
                        s5_d[l].reshape(1, 256), s5_glu_w[l].astype(BF16), s5_glu_b[l].reshape(1, 256),
                        B, s5_tc)

        bias256 = jnp.repeat(sgu_b[l].T, HEAD_DIM, axis=1)
        y_sgu = _sgu_call(sgu, sgu_w[l], bias256, ones256, sgu_norm_g[l].reshape(1, 256), ts)

        pet, peb, w1t, w1b, w2big = _compress_weights(nsa_cmp_pe[l], nsa_cmp_w1[l], nsa_cmp_w2[l])
        kcmp, vcmp = _compress_call(kvc.reshape(B, n_cmp_rows, CMP_STRIDE * 256), pet, peb, w1t, w1b,
                                    w2big, ones128, k_gain[0:1], cos_c, sin_c)
        y_nsa = _nsa_call(nq, gates, kcmp, vcmp, kv, ovt)

        mix_g = mix_norm_g[l].reshape(4, 256)
        mix_g = mix_g.at[3].set(mix_g[3][nsa_perm])
        wo = w_out[l]
        wo = jnp.concatenate([wo[:768], wo[768:][nsa_perm]], axis=0).astype(BF16)
        x = _mix_ffn_call(x, y_ret, y_s5, y_sgu, y_nsa, mix_g, wo, g1,
                          norm2_g[l].reshape(1, D), sc2, sh2, g2, ffn_w_up[l].astype(BF16),
                          ffn_conv_w[l], ffn_conv_b[l].reshape(1, -1), ffn_w_down[l].astype(BF16),
                          ts, 256)
    return x
```

```python
import functools
import math

import jax
import jax.numpy as jnp
import numpy as np
from jax import lax
from jax.experimental import pallas as pl
from jax.experimental.pallas import tpu as pltpu

F32 = jnp.float32
BF16 = jnp.bfloat16

HEAD_DIM = 64
GROUP_WIDTH = 256
RET_CHUNK = 128
S5_GROUPS = 16
S5_CH = 16
S5_STATE = 64
S5_WIDTH = S5_GROUPS * S5_STATE
SGU_GROUPS = 4
SGU_CHUNK = 128
NSA_KV_WIDTH = 128
CMP_STRIDE = 16
CMP_BLOCK = 32
CMP_HIDDEN = 128
SEL_BLOCK = 64
SEL_TOPK = 8
FORCE_BONUS = 1e3
WIN = 512
CONV_W = 3
ROPE_THETA = 10000.0
EPS = 1e-6
NEG = -1e9
QK_SCALE = HEAD_DIM ** -0.5
LOG2E = math.log2(math.e)
SEL_MASK = 2.0 ** 30

ATT_TILE = 256
VMEM_LIMIT = 56 * 1024 * 1024


def _cparams(sem):
    return pltpu.CompilerParams(dimension_semantics=sem, vmem_limit_bytes=VMEM_LIMIT)


def _dot(a, b):
    return jnp.dot(a.astype(BF16), b.astype(BF16), preferred_element_type=F32)


def _dot_nt(a, b):
    return lax.dot_general(a.astype(BF16), b.astype(BF16), (((1,), (1,)), ((), ())),
                           preferred_element_type=F32)


def _dot_split(a, b_bf16):
    hi = a.astype(BF16)
    lo = (a - hi.astype(F32)).astype(BF16)
    return (jnp.dot(hi, b_bf16, preferred_element_type=F32)
            + jnp.dot(lo, b_bf16, preferred_element_type=F32))


def _sigmoid(x):
    return 0.5 + 0.5 * jnp.tanh(0.5 * x)


def _silu(x):
    h = 0.5 * x
    return h + h * jnp.tanh(h)


def _gelu(x):
    c = math.sqrt(2.0 / math.pi)
    return 0.5 * x * (1.0 + jnp.tanh(c * (x + 0.044715 * (x * x * x))))


def _group_rms(x, ones_bd):
    ms = _dot_split(x * x, ones_bd) * (1.0 / HEAD_DIM)
    return lax.rsqrt(ms + EPS)


def _rot_half(x):
    n = x.shape[-1]
    lane = lax.broadcasted_iota(jnp.int32, x.shape, 1)
    first = (lane % HEAD_DIM) < (HEAD_DIM // 2)
    return jnp.where(first, -pltpu.roll(x, n - HEAD_DIM // 2, 1), pltpu.roll(x, HEAD_DIM // 2, 1))


def _rope(x, cos, sin):
    return x * cos + _rot_half(x) * sin


def _lane_group(shape, width):
    return lax.broadcasted_iota(jnp.int32, shape, len(shape) - 1) // width


def _mod_kernel(c_ref, w_ref, b_ref, o_ref):
    o_ref[...] = _dot(_silu(c_ref[...]), w_ref[...]) + b_ref[...]


def _mod_call(c, ada_w, ada_b):
    L, D, D6 = ada_w.shape
    B = c.shape[0]
    nj = D6 // D
    return pl.pallas_call(
        _mod_kernel,
        grid=(L, nj),
        in_specs=[pl.BlockSpec((B, D), lambda l, j: (0, 0)),
                  pl.BlockSpec((None, D, D), lambda l, j: (l, 0, j)),
                  pl.BlockSpec((None, 1, D), lambda l, j: (l, 0, j))],
        out_specs=pl.BlockSpec((None, B, D), lambda l, j: (l, 0, j)),
        out_shape=jax.ShapeDtypeStruct((L, B, D6), F32),
        compiler_params=_cparams(("arbitrary", "arbitrary")),
        name="adaln_mod",
    )(c, ada_w, ada_b.reshape(L, 1, D6))


C_RET = 0
C_S5 = 1024
C_SGU = 1280
C_NQ = 1792
C_KVC = 2048
C_KV = 2304
C_GATE = 2816
N_GATE = 12
N_PROJ = 2944


def _proj_columns():
    cols = list(range(0, 1792))
    nq0 = 1792
    for g in range(2):
        for kvh in range(2):
            base = nq0 + (kvh * 2 + g) * HEAD_DIM
            cols += list(range(base, base + HEAD_DIM))
    cols += list(range(2048, 2816 + N_GATE))
    return np.asarray(cols, dtype=np.int32)


def _gate_expand():
    e = np.zeros((128, 6 * 128), dtype=np.float32)
    for g in range(2):
        for br in range(3):
            for kvh in range(2):
                c0 = (g * 3 + br) * 128 + kvh * HEAD_DIM
                e[(kvh * 2 + g) * 3 + br, c0:c0 + HEAD_DIM] = 1.0
    return jnp.asarray(e, dtype=BF16)


def _inproj_kernel(x_ref, g_ref, sc_ref, sh_ref, w_ref, cos_ref, sin_ref, ones_ref, qg_ref, kg_ref,
                   gexp_ref, ret_ref, s5_ref, sgu_ref, nq_ref, kvc_ref, kv_ref, gate_ref):
    x = x_ref[...]
    ms = jnp.mean(x * x, axis=-1, keepdims=True)
    h = x * lax.rsqrt(ms + EPS) * g_ref[...]
    h = h * (1.0 + sc_ref[...]) + sh_ref[...]
    hb = h.astype(BF16)

    p_all = jnp.dot(hb, w_ref[...], preferred_element_type=F32)

    def proj(c0, n):
        return p_all[:, c0:c0 + n]

    cos = cos_ref[...]
    sin = sin_ref[...]
    cos_h, sin_h = cos[:, :128], sin[:, :128]
    ones256 = ones_ref[...]
    ones128 = ones256[:128, :128]

    ret_ref[:, 0:256] = _rope(proj(C_RET, 256), cos, sin) * QK_SCALE
    ret_ref[:, 256:512] = _rope(proj(C_RET + 256, 256), cos, sin)
    ret_ref[:, 512:1024] = proj(C_RET + 512, 512)
    s5_ref[...] = proj(C_S5, 256)
    sgu_ref[...] = proj(C_SGU, 512)
    q = proj(C_NQ, 256)
    qn = q * _group_rms(q, ones256) * qg_ref[...]
    nq_ref[...] = _rope(qn, cos, sin) * QK_SCALE
    kvc_ref[...] = proj(C_KVC, 256)
    kg = kg_ref[...]
    for i in range(2):
        kv = proj(C_KV + i * 256, 256)
        k = kv[:, 0:128]
        kn = k * _group_rms(k, ones128) * kg[i + 1:i + 2, :]
        kv_ref[:, i * 256:i * 256 + 128] = _rope(kn, cos_h, sin_h)
        kv_ref[:, i * 256 + 128:i * 256 + 256] = kv[:, 128:256]
    gate_ref[...] = _dot_split(_sigmoid(proj(C_GATE, 128)), gexp_ref[...])


def _inproj_call(x, norm_g, sc, sh, w, cos, sin, ones256, q_gain, k_gain, gexp, ts):
    B, S, D = x.shape
    row = lambda n: pl.BlockSpec((None, ts, n), lambda b, s: (b, s, 0))
    const = lambda shape: pl.BlockSpec(shape, lambda b, s: (0,) * len(shape),
                                       pipeline_mode=pl.Buffered(1))
    vec = pl.BlockSpec((None, 1, D), lambda b, s: (b, 0, 0))
    return pl.pallas_call(
        _inproj_kernel,
        grid=(B, S // ts),
        in_specs=[row(D), const((1, D)), vec, vec, const((D, N_PROJ)),
                  pl.BlockSpec((ts, 256), lambda b, s: (s, 0)),
                  pl.BlockSpec((ts, 256), lambda b, s: (s, 0)),
                  const((256, 256)), const((1, 256)), const((3, 128)), const((128, 768))],
        out_specs=[row(1024), pl.BlockSpec((ts, 256), lambda b, s: (s, b)), row(512), row(256),
                   row(256), row(512), row(768)],
        out_shape=[jax.ShapeDtypeStruct((B, S, 1024), F32),
                   jax.ShapeDtypeStruct((S, B * 256), F32),
                   jax.ShapeDtypeStruct((B, S, 512), F32),
                   jax.ShapeDtypeStruct((B, S, 256), F32),
                   jax.ShapeDtypeStruct((B, S, 256), F32),
                   jax.ShapeDtypeStruct((B, S, 512), F32),
                   jax.ShapeDtypeStruct((B, S, 768), F32)],
        compiler_params=_cparams(("arbitrary", "arbitrary")),
        name="in_proj",
    )(x, norm_g, sc, sh, w, cos, sin, ones256, q_gain, k_gain, gexp)


def _retention_kernel(blk_ref, dec_ref, zeta_ref, xi_ref, gbd_ref, bd_ref, ones_ref, ng_ref,
                      o_ref, r_ref):
    @pl.when(pl.program_id(1) == 0)
    def _():
        r_ref[...] = jnp.zeros_like(r_ref)

    C = RET_CHUNK
    head = _lane_group((C, GROUP_WIDTH), HEAD_DIM)
    for ci in range(blk_ref.shape[0] // C):
        rows = slice(ci * C, (ci + 1) * C)
        q = blk_ref[rows, 0:256]
        k = blk_ref[rows, 256:512]
        g = blk_ref[rows, 768:1024]
        kb = k.astype(BF16)
        vb = blk_ref[rows, 512:768].astype(BF16)
        o = jnp.zeros(q.shape, F32)
        for h in range(GROUP_WIDTH // HEAD_DIM):
            qh = jnp.where(head == h, q, 0.0)
            s = _dot_nt(qh, kb) * dec_ref[h]
            o = o + jnp.where(head == h, _dot(s, vb), 0.0)
        r_prev = r_ref[...]
        o = o + _dot(q, r_prev) * xi_ref[...]
        kz_t = (k * zeta_ref[...]).T
        u = _dot(kz_t, vb)
        r_ref[...] = r_prev * gbd_ref[...] + u * bd_ref[...]
        o = o * _group_rms(o, ones_ref[...]) * ng_ref[...]
        o_ref[rows, :] = _silu(g) * o


def _retention_call(ret, dec, zeta, xi, gbd, bd, ones256, norm_g, rows):
    B, S, _ = ret.shape
    C = RET_CHUNK
    const = lambda shape: pl.BlockSpec(shape, lambda b, n: (0,) * len(shape))
    return pl.pallas_call(
        _retention_kernel,
        grid=(B, S // rows),
        in_specs=[pl.BlockSpec((None, rows, 1024), lambda b, n: (b, n, 0)),
                  const((4, C, C)), const((C, 256)), const((C, 256)),
                  const((256, 256)), const((256, 256)), const((256, 256)), const((1, 256))],
        out_specs=pl.BlockSpec((None, rows, 256), lambda b, n: (b, n, 0)),
        out_shape=jax.ShapeDtypeStruct((B, S, 256), F32),
        scratch_shapes=[pltpu.VMEM((256, 256), F32)],
        compiler_params=_cparams(("arbitrary", "arbitrary")),
        name="retention",
    )(ret, dec, zeta, xi, gbd, bd, ones256, norm_g)


def _retention_consts():
    H, C = GROUP_WIDTH // HEAD_DIM, RET_CHUNK
    log_gamma = np.log1p(-(2.0 ** (-5.0 - np.arange(H, dtype=np.float64))))
    idx = np.arange(C, dtype=np.float64)
    diff = idx[:, None] - idx[None, :]
    dec = np.where(diff >= 0, np.exp(log_gamma[:, None, None] * np.maximum(diff, 0.0)), 0.0)
    zeta = np.exp(log_gamma[:, None] * (C - 1 - idx)[None, :])
    xi = np.exp(log_gamma[:, None] * (idx + 1)[None, :])
    gamma_chunk = np.exp(log_gamma * C)
    zeta256 = np.repeat(zeta.T, HEAD_DIM, axis=1)
    xi256 = np.repeat(xi.T, HEAD_DIM, axis=1)
    hid = np.arange(GROUP_WIDTH) // HEAD_DIM
    bd = (hid[:, None] == hid[None, :]).astype(np.float64)
    gbd = bd * gamma_chunk[hid][:, None]
    f = lambda a: jnp.asarray(a, dtype=F32)
    return f(dec), f(zeta256), f(xi256), f(gbd), f(bd)


def _sgu_kernel(blk_ref, w_ref, bias_ref, ones_ref, ng_ref, o_ref):
    C = w_ref.shape[-1]
    grp = _lane_group((C, GROUP_WIDTH), HEAD_DIM)
    tril = (lax.broadcasted_iota(jnp.int32, (C, C), 0) >= lax.broadcasted_iota(jnp.int32, (C, C), 1))
    w_tril = [jnp.where(tril, w_ref[gi], 0.0).astype(BF16) for gi in range(SGU_GROUPS)]
    for ci in range(blk_ref.shape[0] // C):
        rows = slice(ci * C, (ci + 1) * C)
        u = _gelu(blk_ref[rows, 0:256])
        v = _gelu(blk_ref[rows, 256:512])
        vn = v * _group_rms(v, ones_ref[...]) * ng_ref[...]
        vb = vn.astype(BF16)
        s = bias_ref[...]
        for gi in range(SGU_GROUPS):
            s = s + jnp.where(grp == gi, jnp.dot(w_tril[gi], vb, preferred_element_type=F32), 0.0)
        o_ref[rows, :] = u * s


def _sgu_call(sgu, w_s, bias256, ones256, norm_g, rows):
    B, S, _ = sgu.shape
    C = SGU_CHUNK
    const = lambda shape: pl.BlockSpec(shape, lambda b, n: (0,) * len(shape))
    return pl.pallas_call(
        _sgu_kernel,
        grid=(B, S // rows),
        in_specs=[pl.BlockSpec((None, rows, 512), lambda b, n: (b, n, 0)),
                  const((SGU_GROUPS, C, C)), const((C, 256)), const((256, 256)), const((1, 256))],
        out_specs=pl.BlockSpec((None, rows, 256), lambda b, n: (b, n, 0)),
        out_shape=jax.ShapeDtypeStruct((B, S, 256), F32),
        compiler_params=_cparams(("arbitrary", "arbitrary")),
        name="spatial_gating",
    )(sgu, w_s, bias256, ones256, norm_g)


def _s5_prep_kernel(lr_ref, li_ref, ldt_ref, bre_ref, bim_ref, a_ref, bx_ref):
    lr = lr_ref[...]
    li = li_ref[...]
    dts = jnp.exp(ldt_ref[...])
    mag = jnp.exp(lr * dts)
    a_re = mag * jnp.cos(li * dts)
    a_im = mag * jnp.sin(li * dts)
    den = lr * lr + li * li
    f_re = ((a_re - 1.0) * lr + a_im * li) / den
    f_im = (a_im * lr - (a_re - 1.0) * li) / den
    a_ref[0:1, :] = a_re
    a_ref[1:2, :] = a_im
    bre = bre_ref[...]
    bim = bim_ref[...]
    bx_ref[:, 0:S5_WIDTH] = (f_re * bre - f_im * bim).astype(BF16)
    bx_ref[:, S5_WIDTH:2 * S5_WIDTH] = (f_re * bim + f_im * bre).astype(BF16)


def _s5_prep_call(lam_re, lam_im, log_dt, b_re, b_im):
    L = lam_re.shape[0]
    W = S5_WIDTH
    lr = lam_re.reshape(L, 1, W)
    li = lam_im.reshape(L, 1, W)
    ldt = jnp.repeat(log_dt, S5_STATE, axis=-1).reshape(L, 1, W)
    eye = jnp.eye(S5_GROUPS, dtype=F32)

    def dense(b):
        return jnp.einsum('lgph,gk->lghkp', b, eye).reshape(L, S5_GROUPS * S5_CH, W)

    vec = pl.BlockSpec((None, 1, W), lambda l: (l, 0, 0))
    mat = pl.BlockSpec((None, GROUP_WIDTH, W), lambda l: (l, 0, 0))
    return pl.pallas_call(
        _s5_prep_kernel,
        grid=(L,),
        in_specs=[vec, vec, vec, mat, mat],
        out_specs=[pl.BlockSpec((None, 2, W), lambda l: (l, 0, 0)),
                   pl.BlockSpec((None, GROUP_WIDTH, 2 * W), lambda l: (l, 0, 0))],
        out_shape=[jax.ShapeDtypeStruct((L, 2, W), F32),
                   jax.ShapeDtypeStruct((L, GROUP_WIDTH, 2 * W), BF16)],
        compiler_params=_cparams(("arbitrary",)),
        name="s5_discretise",
    )(lr, li, ldt, dense(b_re), dense(b_im))


def _split_bf16(x):
    hi = x.astype(BF16)
    return hi, (x - hi.astype(F32)).astype(BF16)


def _s5_kernel(u_ref, a_ref, bx_ref, cre_ref, cim_ref, d_ref, gw_ref, gb_ref, pf_ref, pb_ref, o_ref,
               hre_ref, him_ref, xre_ref, xim_ref, utb_ref, *, nb, tc, sub):
    @pl.when(pl.program_id(0) == 0)
    def _():
        hre_ref[...] = jnp.zeros_like(hre_ref)
        him_ref[...] = jnp.zeros_like(him_ref)

    blk = sub * nb
    for sb in range(tc // sub):
        t_rows = slice(sb * sub, (sb + 1) * sub)
        u_bt = jnp.concatenate([u_ref[t_rows, b * 256:(b + 1) * 256] for b in range(nb)], axis=0)
        hi, lo = _split_bf16(u_bt)
        utb_ref[sb * blk:(sb + 1) * blk, :] = (
            jnp.dot(pf_ref[...], hi, preferred_element_type=F32)
            + jnp.dot(pf_ref[...], lo, preferred_element_type=F32))
    u = utb_ref[...]
    ub = u.astype(BF16)
    xre_ref[...] = jnp.dot(ub, bx_ref[:, 0:S5_WIDTH], preferred_element_type=F32)
    xim_ref[...] = jnp.dot(ub, bx_ref[:, S5_WIDTH:2 * S5_WIDTH], preferred_element_type=F32)
    a_re = jnp.broadcast_to(a_ref[0:1, :], (nb, S5_WIDTH))
    a_im = jnp.broadcast_to(a_ref[1:2, :], (nb, S5_WIDTH))

    def step(t, carry):
        h_re, h_im = carry
        r0 = pl.multiple_of(t * nb, nb)
        n_re = a_re * h_re - a_im * h_im + xre_ref[pl.ds(r0, nb), :]
        n_im = a_re * h_im + a_im * h_re + xim_ref[pl.ds(r0, nb), :]
        xre_ref[pl.ds(r0, nb), :] = n_re
        xim_ref[pl.ds(r0, nb), :] = n_im
        return n_re, n_im

    h_re, h_im = lax.fori_loop(0, tc, step, (hre_ref[...], him_ref[...]))
    hre_ref[...] = h_re
    him_ref[...] = h_im
    y = _dot(xre_ref[...], cre_ref[...]) - _dot(xim_ref[...], cim_ref[...])
    y = _gelu(y + d_ref[...] * u)
    y = y * _sigmoid(_dot(y, gw_ref[...]) + gb_ref[...])
    for sb in range(tc // sub):
        hi, lo = _split_bf16(y[sb * blk:(sb + 1) * blk, :])
        y_bt = (jnp.dot(pb_ref[...], hi, preferred_element_type=F32)
                + jnp.dot(pb_ref[...], lo, preferred_element_type=F32))
        for b in range(nb):
            o_ref[sb * sub:(sb + 1) * sub, b * 256:(b + 1) * 256] = y_bt[b * sub:(b + 1) * sub, :]


def _s5_perms(nb, sub):
    n = nb * sub
    fwd = np.zeros((n, n), dtype=np.float32)
    for t in range(sub):
        for b in range(nb):
            fwd[t * nb + b, b * sub + t] = 1.0
    return jnp.asarray(fwd, dtype=BF16), jnp.asarray(fwd.T, dtype=BF16)


def _s5_call(u, a, bx, cre, cim, d, glu_w, glu_b, nb, tc):
    S = u.shape[0]
    W = S5_WIDTH
    sub = min(tc, max(8, 256 // nb))
    perm_f, perm_b = _s5_perms(nb, sub)
    const = lambda shape: pl.BlockSpec(shape, lambda i: (0,) * len(shape))
    return pl.pallas_call(
        functools.partial(_s5_kernel, nb=nb, tc=tc, sub=sub),
        grid=(S // tc,),
        in_specs=[pl.BlockSpec((tc, nb * 256), lambda i: (i, 0)),
                  const((2, W)), const((256, 2 * W)), const((W, 256)), const((W, 256)),
                  const((1, 256)), const((256, 256)), const((1, 256)),
                  const((nb * sub, nb * sub)), const((nb * sub, nb * sub))],
        out_specs=pl.BlockSpec((tc, nb * 256), lambda i: (i, 0)),
        out_shape=jax.ShapeDtypeStruct((S, nb * 256), F32),
        scratch_shapes=[pltpu.VMEM((nb, W), F32), pltpu.VMEM((nb, W), F32),
                        pltpu.VMEM((tc * nb, W), F32), pltpu.VMEM((tc * nb, W), F32),
                        pltpu.VMEM((tc * nb, 256), F32)],
        compiler_params=_cparams(("arbitrary",)),
        name="s5_scan",
    )(u, a, bx, cre, cim, d, glu_w, glu_b, perm_f, perm_b)


def _compress_kernel(hm_ref, pet_ref, peb_ref, w1t_ref, w1b_ref, w2_ref, ones_ref, kg_ref,
                     cos_ref, sin_ref, k_ref, v_ref):
    hm = hm_ref[...]
    a = _dot(hm + pet_ref[...], w1t_ref[...])
    b = _dot(hm + peb_ref[...], w1b_ref[...])
    n = a.shape[0]
    pre = a + pltpu.roll(b, n - 1, 0)
    cmp = _dot(_gelu(pre), w2_ref[...])
    k = cmp[:, 0:128]
    kn = k * _group_rms(k, ones_ref[...]) * kg_ref[...]
    k_ref[...] = _rope(kn, cos_ref[...], sin_ref[...])
    v_ref[...] = cmp[:, 128:256]


def _compress_call(hm, pet, peb, w1t, w1b, w2, ones128, k_gain, cos_c, sin_c):
    B, n, width = hm.shape
    const = lambda shape: pl.BlockSpec(shape, lambda b: (0,) * len(shape))
    out = pl.BlockSpec((None, n, 128), lambda b: (b, 0, 0))
    return pl.pallas_call(
        _compress_kernel,
        grid=(B,),
        in_specs=[pl.BlockSpec((None, n, width), lambda b: (b, 0, 0)),
                  const((1, width)), const((1, width)), const((width, 512)), const((width, 512)),
                  const((512, 256)), const((128, 128)), const((1, 128)),
                  const((n, 128)), const((n, 128))],
        out_specs=[out, out],
        out_shape=[jax.ShapeDtypeStruct((B, n, 128), F32)] * 2,
        compiler_params=_cparams(("arbitrary",)),
        name="nsa_compress",
    )(hm, pet, peb, w1t, w1b, w2, ones128, k_gain, cos_c, sin_c)


def _nsa_kernel(nq_ref, gate_ref, kc_ref, vc_ref, kv_ref, ovt_ref, o_ref,
                s_ref, m_ref, acc_ref, *, n_cmp_rows, win_keys):
    T = ATT_TILE
    qi = pl.program_id(1)
    t0 = qi * T
    pos_c = t0 + lax.broadcasted_iota(jnp.int32, (T, 1), 0)
    pos_r = t0 + lax.broadcasted_iota(jnp.int32, (1, T), 1)
    kvh_lane = _lane_group((T, 128), HEAD_DIM)
    combos = [(g, kvh) for g in range(2) for kvh in range(2)]
    rows = lambda ci: slice(ci * T, (ci + 1) * T)
    q_all = jnp.concatenate(
        [jnp.where(kvh_lane == kvh, nq_ref[:, g * 128:(g + 1) * 128] * LOG2E, 0.0).astype(BF16)
         for g, kvh in combos], axis=0)
    ones_cols = jnp.ones((T, 128), BF16)

    def head_lanes(x):
        return [jnp.where(kvh_lane == 0, x[rows(2 * g)], x[rows(2 * g + 1)]) for g in range(2)]

    w0 = pl.multiple_of(jnp.maximum(t0 + T - win_keys, 0), T)
    dist = pos_c - (w0 + lax.broadcasted_iota(jnp.int32, (1, win_keys), 1))
    ok_w = (dist >= 0) & (dist < WIN)
    s_w = _dot_nt(q_all, kv_ref[pl.ds(w0, win_keys), 256:384])
    vw_ext = jnp.concatenate([kv_ref[pl.ds(w0, win_keys), 384:512].astype(BF16),
                              jnp.ones((win_keys, 128), BF16)], axis=1)
    o_w = []
    for ci in range(4):
        s = jnp.where(ok_w, s_w[rows(ci)], NEG)
        e = jnp.exp2(s - jnp.max(s, axis=-1, keepdims=True))
        ow = jnp.dot(e.astype(BF16), vw_ext, preferred_element_type=F32)
        o_w.append(ow[:, 0:128] * (1.0 / ow[:, 128:256]))
    o_win = head_lanes(jnp.concatenate(o_w, axis=0))

    cmp_end = lax.broadcasted_iota(jnp.int32, (1, n_cmp_rows), 1) * CMP_STRIDE + (CMP_BLOCK - 1)
    valid = cmp_end <= pos_c
    s_all = _dot_nt(q_all, kc_ref[...])
    p_list = []
    for ci in range(4):
        s = jnp.where(valid, s_all[rows(ci)], NEG)
        e = jnp.exp2(s - jnp.max(s, axis=-1, keepdims=True))
        inv = 1.0 / jnp.sum(e, axis=-1, keepdims=True)
        p_list.append(jnp.where(valid, e * inv, 0.0))
    o_cmp = head_lanes(_dot(jnp.concatenate(p_list, axis=0), vc_ref[...]))

    n_sel = ovt_ref.shape[0]
    blk = lax.broadcasted_iota(jnp.int32, (n_sel, T), 0)
    cur = pos_r // SEL_BLOCK
    forced = (blk == 0) | (blk == cur) | (blk == cur - 1)
    causal_blk = blk <= cur
    ovt = ovt_ref[...]
    sel_rows = []
    for kvh in range(2):
        p_kvh = p_list[kvh] + p_list[2 + kvh]
        ph = p_kvh.astype(BF16)
        plo = (p_kvh - ph.astype(F32)).astype(BF16)
        imp = (lax.dot_general(ovt, ph, (((1,), (1,)), ((), ())), preferred_element_type=F32)
               + lax.dot_general(ovt, plo, (((1,), (1,)), ((), ())), preferred_element_type=F32))
        imp = jnp.where(forced, imp + FORCE_BONUS, imp)
        imp = jnp.where(causal_blk, imp, NEG)
        cnt = jnp.zeros((n_sel, T), F32)
        for j in range(n_sel):
            row = imp[j:j + 1, :]
            ahead = (row > imp) | ((row == imp) & (blk > j))
            cnt = cnt + jnp.where(ahead, 1.0, 0.0)
        sel_t = jnp.where((cnt < SEL_TOPK) & causal_blk, 1.0, 0.0)
        sel_t = jnp.concatenate([sel_t, jnp.zeros((128 - n_sel, T), F32)], axis=0)
        sel_rows.append(sel_t.T.astype(BF16))

    tri = (lax.broadcasted_iota(jnp.int32, (T, T), 0) >= lax.broadcasted_iota(jnp.int32, (T, T), 1))
    bias = [((s.astype(F32) - 1.0) * SEL_MASK).astype(BF16) for s in sel_rows]
    q_ext = jnp.concatenate([q_all, jnp.concatenate([bias[kvh] for _, kvh in combos], axis=0)], axis=1)
    key_blk = lax.broadcasted_iota(jnp.int32, (T, 128), 0) // SEL_BLOCK
    lane_id = lax.broadcasted_iota(jnp.int32, (T, 128), 1)
    m_ref[...] = jnp.full(m_ref.shape, NEG, F32)

    def scores(j, diagonal):
        r0 = pl.multiple_of(j * T, T)
        onehot = jnp.where(lane_id == (T // SEL_BLOCK) * j + key_blk, 1.0, 0.0).astype(BF16)
        k_ext = jnp.concatenate([kv_ref[pl.ds(r0, T), 0:128].astype(BF16), onehot], axis=1)
        s = _dot_nt(q_ext, k_ext)
        for ci in range(4):
            sm = jnp.where(tri, s[rows(ci)], NEG) if diagonal else s[rows(ci)]
            s_ref[j, rows(ci), :] = sm
            half = sm[:, 0:128]
            for c in range(1, T // 128):
                half = jnp.maximum(half, sm[:, c * 128:(c + 1) * 128])
            m_ref[rows(ci), :] = jnp.maximum(m_ref[rows(ci), :], half)

    def pass1(jp, carry):
        scores(2 * jp, False)
        scores(2 * jp + 1, False)
        return carry

    lax.fori_loop(0, qi // 2, pass1, 0)

    @pl.when(qi % 2 == 1)
    def _():
        scores(qi - 1, False)

    scores(qi, True)
    m_ref[...] = jnp.broadcast_to(jnp.max(m_ref[...], axis=-1, keepdims=True), m_ref.shape)
    acc_ref[...] = jnp.zeros_like(acc_ref)

    def weighted(j):
        r0 = pl.multiple_of(j * T, T)
        m = m_ref[...]
        p = jnp.exp2(s_ref[j] - jnp.concatenate([m] * (T // 128), axis=1))
        v_ext = jnp.concatenate([kv_ref[pl.ds(r0, T), 128:256].astype(BF16), ones_cols], axis=1)
        return jnp.dot(p.astype(BF16), v_ext, preferred_element_type=F32)

    def pass2(jp, carry):
        acc_ref[...] += weighted(2 * jp) + weighted(2 * jp + 1)
        return carry

    lax.fori_loop(0, (qi + 1) // 2, pass2, 0)

    @pl.when(qi % 2 == 0)
    def _():
        acc_ref[...] += weighted(qi)
    o_sel = head_lanes(acc_ref[:, 0:128] * (1.0 / acc_ref[:, 128:256]))

    for g in range(2):
        gc = gate_ref[:, (g * 3 + 0) * 128:(g * 3 + 1) * 128]
        gs = gate_ref[:, (g * 3 + 1) * 128:(g * 3 + 2) * 128]
        gw = gate_ref[:, (g * 3 + 2) * 128:(g * 3 + 3) * 128]
        o_ref[:, g * 128:(g + 1) * 128] = gc * o_cmp[g] + gs * o_sel[g] + gw * o_win[g]


def _nsa_call(nq, gates, kcmp, vcmp, kv, ovt):
    B, S, _ = nq.shape
    T = ATT_TILE
    n = kcmp.shape[1]
    n_sel = ovt.shape[0]
    win_keys = min(WIN + T, S)
    return pl.pallas_call(
        functools.partial(_nsa_kernel, n_cmp_rows=n, win_keys=win_keys),
        grid=(B, S // T),
        in_specs=[pl.BlockSpec((None, T, 256), lambda b, i: (b, i, 0)),
                  pl.BlockSpec((None, T, 768), lambda b, i: (b, i, 0)),
                  pl.BlockSpec((None, n, 128), lambda b, i: (b, 0, 0)),
                  pl.BlockSpec((None, n, 128), lambda b, i: (b, 0, 0)),
                  pl.BlockSpec((None, S, 512), lambda b, i: (b, 0, 0)),
                  pl.BlockSpec((n_sel, n), lambda b, i: (0, 0))],
        out_specs=pl.BlockSpec((None, T, 256), lambda b, i: (b, i, 0)),
        out_shape=jax.ShapeDtypeStruct((B, S, 256), F32),
        scratch_shapes=[pltpu.VMEM((S // T, 4 * T, T), F32), pltpu.VMEM((4 * T, 128), F32),
                        pltpu.VMEM((4 * T, 256), F32)],
        compiler_params=_cparams(("arbitrary", "arbitrary")),
        name="nsa_attention",
    )(nq, gates, kcmp, vcmp, kv, ovt)


def _mix_ffn_kernel(x_ref, yr_ref, ys_ref, yg_ref, yn_ref, mg_ref, wo_ref, g1_ref,
                    g_ref, sc_ref, sh_ref, g2_ref, wup_ref, cw_ref, cb_ref, wdn_ref, o_ref,
                    tail_ref, z_ref, *, d_ff, tf):
    @pl.when(pl.program_id(1) == 0)
    def _():
        tail_ref[...] = jnp.zeros_like(tail_ref)

    normed = []
    for i, y_ref in enumerate((yr_ref, ys_ref, yg_ref, yn_ref)):
        y = y_ref[...]
        ms = jnp.mean(y * y, axis=-1, keepdims=True)
        normed.append((y * lax.rsqrt(ms + EPS) * mg_ref[i:i + 1, :]).astype(BF16))
    mixed = jnp.dot(jnp.concatenate(normed, axis=1), wo_ref[...], preferred_element_type=F32)
    x = x_ref[...] + g1_ref[...] * mixed

    ts = x.shape[0]
    ms = jnp.mean(x * x, axis=-1, keepdims=True)
    h = x * lax.rsqrt(ms + EPS) * g_ref[...]
    hb = (h * (1.0 + sc_ref[...]) + sh_ref[...]).astype(BF16)
    row8 = lax.broadcasted_iota(jnp.int32, (8, tf), 0)

    def conv(c0):
        a = jnp.dot(hb, wup_ref[:, c0:c0 + tf], preferred_element_type=F32)
        t6 = tail_ref[6:7, c0:c0 + tf]
        t7 = tail_ref[7:8, c0:c0 + tf]
        a1 = pltpu.roll(a, 1, 0)
        a2 = pltpu.roll(a, 2, 0)
        a1 = jnp.concatenate([jnp.where(row8 == 0, t7, a1[0:8]), a1[8:]], axis=0)
        a2 = jnp.concatenate([jnp.where(row8 == 0, t6, jnp.where(row8 == 1, t7, a2[0:8])), a2[8:]],
                             axis=0)
        tail_ref[:, c0:c0 + tf] = a[ts - 8:ts, :]
        return (cw_ref[2:3, c0:c0 + tf] * a + cw_ref[1:2, c0:c0 + tf] * a1
                + cw_ref[0:1, c0:c0 + tf] * a2 + cb_ref[:, c0:c0 + tf])

    n_chunks = d_ff // tf
    split = (n_chunks // 2) * tf
    acc = None
    for c in range(n_chunks):
        gate = conv(c * tf)
        up = conv(d_ff + c * tf)
        z_ref[:, c * tf:(c + 1) * tf] = (_silu(gate) * up).astype(BF16)
        if (c + 1) * tf == split:
            acc = jnp.dot(z_ref[:, 0:split], wdn_ref[0:split, :], preferred_element_type=F32)
    acc = acc + jnp.dot(z_ref[:, split:d_ff], wdn_ref[split:d_ff, :], preferred_element_type=F32)
    o_ref[...] = x + g2_ref[...] * acc


def _mix_ffn_call(x, y_ret, y_s5, y_sgu, y_nsa, mix_g, w_out, g1,
                  norm_g, sc, sh, g2, w_up, conv_w, conv_b, w_down, ts, tf):
    B, S, D = x.shape
    d_ff = w_down.shape[0]
    row = lambda n: pl.BlockSpec((None, ts, n), lambda b, s: (b, s, 0))
    vec = pl.BlockSpec((None, 1, D), lambda b, s: (b, 0, 0))
    const = lambda shape: pl.BlockSpec(shape, lambda b, s: (0,) * len(shape),
                                       pipeline_mode=pl.Buffered(1))
    return pl.pallas_call(
        functools.partial(_mix_ffn_kernel, d_ff=d_ff, tf=tf),
        grid=(B, S // ts),
        in_specs=[row(D), row(256), pl.BlockSpec((ts, 256), lambda b, s: (s, b)), row(256), row(256),
                  const((4, 256)), const((D, D)), vec,
                  const((1, D)), vec, vec, vec, const((D, 2 * d_ff)),
                  const((CONV_W, 2 * d_ff)), const((1, 2 * d_ff)), const((d_ff, D))],
        out_specs=row(D),
        out_shape=jax.ShapeDtypeStruct((B, S, D), F32),
        scratch_shapes=[pltpu.VMEM((8, 2 * d_ff), F32), pltpu.VMEM((ts, d_ff), BF16)],
        compiler_params=_cparams(("arbitrary", "arbitrary")),
        name="mix_ffn",
    )(x, y_ret, y_s5, y_sgu, y_nsa, mix_g, w_out, g1, norm_g, sc, sh, g2, w_up, conv_w, conv_b, w_down)


def _rope_tables(pos):
    inv = ROPE_THETA ** (-jnp.arange(0, HEAD_DIM, 2, dtype=F32) / HEAD_DIM)
    ang = pos.astype(F32)[:, None] * inv[None, :]
    return jnp.cos(ang), jnp.sin(ang)


def _tile_rope(cos, sin, width):
    reps = width // (HEAD_DIM // 2)
    return jnp.tile(cos, (1, reps)), jnp.tile(sin, (1, reps))


def _block_ones(n):
    hid = np.arange(n) // HEAD_DIM
    return jnp.asarray(hid[:, None] == hid[None, :], dtype=BF16)


def _compress_weights(pe, w1, w2):
    half = CMP_STRIDE * HEAD_DIM
    eye2 = jnp.eye(2, dtype=F32)

    def big(w):
        w = w.reshape(2, CMP_STRIDE, HEAD_DIM, CMP_HIDDEN)
        full = jnp.einsum('ktdj,ka,hb->tkhdabj', w, eye2, eye2)
        return full.reshape(CMP_STRIDE * 4 * HEAD_DIM, 4 * CMP_HIDDEN)

    w1t = big(w1[:, :half])
    w1b = big(w1[:, half:])

    def pe_row(p):
        return jnp.broadcast_to(p[:, :, None, :], (2, CMP_STRIDE, 2, HEAD_DIM)).transpose(
            1, 0, 2, 3).reshape(1, CMP_STRIDE * 4 * HEAD_DIM)

    pet = pe_row(pe[:, :CMP_STRIDE])
    peb = pe_row(pe[:, CMP_STRIDE:])
    w2big = jnp.einsum('kjd,ka,hb->khjabd', w2, eye2, eye2).reshape(4 * CMP_HIDDEN, 4 * HEAD_DIM)
    return pet, peb, w1t.astype(BF16), w1b.astype(BF16), w2big.astype(BF16)


def _nsa_perm():
    idx = np.zeros(GROUP_WIDTH, dtype=np.int32)
    for g in range(2):
        for kvh in range(2):
            for d in range(HEAD_DIM):
                idx[g * 128 + kvh * HEAD_DIM + d] = (kvh * 2 + g) * HEAD_DIM + d
    return idx


def kernel(x, c, norm1_g, norm2_g, ada_w, ada_b, w_in, ret_norm_g, s5_lambda_re, s5_lambda_im, s5_log_dt, s5_b_re, s5_b_im, s5_c_re, s5_c_im, s5_d, s5_glu_w, s5_glu_b, sgu_norm_g, sgu_w, sgu_b, nsa_q_norm_g, nsa_k_norm_g, nsa_cmp_pe, nsa_cmp_w1, nsa_cmp_w2, mix_norm_g, w_out, ffn_w_up, ffn_conv_w, ffn_conv_b, ffn_w_down):
    B, S, D = x.shape
    L = w_in.shape[0]
    ts = min(512, S)
    s5_tc = min(64, S)

    cos, sin = _rope_tables(jnp.arange(S))
    cos256, sin256 = _tile_rope(cos, sin, 256)
    n_cmp_rows = S // CMP_STRIDE
    cos_c, sin_c = _rope_tables(jnp.arange(n_cmp_rows) * CMP_STRIDE + CMP_BLOCK - 1)
    cos_c, sin_c = _tile_rope(cos_c, sin_c, 128)
    ones256 = _block_ones(256)
    ones128 = _block_ones(128)
    dec, zeta256, xi256, gbd, bd = _retention_consts()
    n_sel = S // SEL_BLOCK
    ii = np.arange(n_cmp_rows)[None, :]
    jj = np.arange(n_sel)[:, None]
    overlap_t = ((ii * CMP_STRIDE < (jj + 1) * SEL_BLOCK) & (ii * CMP_STRIDE + CMP_BLOCK > jj * SEL_BLOCK)
                 & (ii < n_cmp_rows - 1))
    ovt = jnp.asarray(overlap_t, dtype=BF16)
    proj_cols = _proj_columns()
    gate_expand = _gate_expand()
    nsa_perm = _nsa_perm()

    mod = _mod_call(c, ada_w, ada_b)
    s5_a, s5_bx = _s5_prep_call(s5_lambda_re, s5_lambda_im, s5_log_dt, s5_b_re, s5_b_im)
    eye_g = jnp.eye(S5_GROUPS, dtype=F32)

    for l in range(L):
        sh1, sc1, g1, sh2, sc2, g2 = [mod[l, :, i * D:(i + 1) * D].reshape(B, 1, D) for i in range(6)]
        w = jnp.pad(w_in[l][:, proj_cols], ((0, 0), (0, N_PROJ - proj_cols.size))).astype(BF16)
        q_gain = jnp.tile(nsa_q_norm_g[l], 4).reshape(1, 256)
        k_gain = jnp.tile(nsa_k_norm_g[l], (1, 2))
        ret, s5u, sgu, nq, kvc, kv, gates = _inproj_call(
            x, norm1_g[l].reshape(1, D), sc1, sh1, w, cos256, sin256, ones256, q_gain, k_gain,
            gate_expand, ts)

        y_ret = _retention_call(ret, dec, zeta256, xi256, gbd, bd, ones256,
                                ret_norm_g[l].reshape(1, 256), ts)

        cre = jnp.einsum('ghp,gk->gpkh', s5_c_re[l], eye_g).reshape(S5_WIDTH, GROUP_WIDTH)
        cim = jnp.einsum('ghp,gk->gpkh', s5_c_im[l], eye_g).reshape(S5_WIDTH, GROUP_WIDTH)
        y_s5 = _s5_call(s5u, s5_a[l], s5_bx[l], cre.astype(BF16), cim.astype(BF16),
                        s5_d[l].reshape(1, 256), s5_glu_w[l].astype(BF16), s5_glu_b[l].reshape(1, 256),
                        B, s5_tc)

        bias256 = jnp.repeat(sgu_b[l].T, HEAD_DIM, axis=1)
        y_sgu = _sgu_call(sgu, sgu_w[l], bias256, ones256, sgu_norm_g[l].reshape(1, 256), ts)

        pet, peb, w1t, w1b, w2big = _compress_weights(nsa_cmp_pe[l], nsa_cmp_w1[l], nsa_cmp_w2[l])
        kcmp, vcmp = _compress_call(kvc.reshape(B, n_cmp_rows, CMP_STRIDE * 256), pet, peb, w1t, w1b,
                                    w2big, ones128, k_gain[0:1], cos_c, sin_c)
        y_nsa = _nsa_call(nq, gates, kcmp, vcmp, kv, ovt)

        mix_g = mix_norm_g[l].reshape(4, 256)
        mix_g = mix_g.at[3].set(mix_g[3][nsa_perm])
        wo = w_out[l]
        wo = jnp.concatenate([wo[:768], wo[768:][nsa_perm]], axis=0).astype(BF16)
        x = _mix_ffn_call(x, y_ret, y_s5, y_sgu, y_nsa, mix_g, wo, g1,
                          norm2_g[l].reshape(1, D), sc2, sh2, g2, ffn_w_up[l].astype(BF16),
                          ffn_conv_w[l], ffn_conv_b[l].reshape(1, -1), ffn_w_down[l].astype(BF16),
                          ts, 256)
    return x
```

```python
import functools
import math

import jax
import jax.numpy as jnp
import numpy as np
from jax import lax
from jax.experimental import pallas as pl
from jax.experimental.pallas import tpu as pltpu

F32 = jnp.float32
BF16 = jnp.bfloat16

HEAD_DIM = 64
GROUP_WIDTH = 256
RET_CHUNK = 128
S5_GROUPS = 16
S5_CH = 16
S5_STATE = 64
S5_WIDTH = S5_GROUPS * S5_STATE
SGU_GROUPS = 4
SGU_CHUNK = 128
NSA_KV_WIDTH = 128
CMP_STRIDE = 16
CMP_BLOCK = 32
CMP_HIDDEN = 128
SEL_BLOCK = 64
SEL_TOPK = 8
FORCE_BONUS = 1e3
WIN = 512
CONV_W = 3
ROPE_THETA = 10000.0
EPS = 1e-6
NEG = -1e9
QK_SCALE = HEAD_DIM ** -0.5
LOG2E = math.log2(math.e)
SEL_MASK = 2.0 ** 30

ATT_TILE = 256
VMEM_LIMIT = 60 * 1024 * 1024


def _cparams(sem):
    return pltpu.CompilerParams(dimension_semantics=sem, vmem_limit_bytes=VMEM_LIMIT)


def _dot(a, b):
    return jnp.dot(a.astype(BF16), b.astype(BF16), preferred_element_type=F32)


def _dot_nt(a, b):
    return lax.dot_general(a.astype(BF16), b.astype(BF16), (((1,), (1,)), ((), ())),
                           preferred_element_type=F32)


def _dot_split(a, b_bf16):
    hi = a.astype(BF16)
    lo = (a - hi.astype(F32)).astype(BF16)
    return (jnp.dot(hi, b_bf16, preferred_element_type=F32)
            + jnp.dot(lo, b_bf16, preferred_element_type=F32))


def _sigmoid(x):
    return 0.5 + 0.5 * jnp.tanh(0.5 * x)


def _silu(x):
    h = 0.5 * x
    return h + h * jnp.tanh(h)


def _gelu(x):
    c = math.sqrt(2.0 / math.pi)
    return 0.5 * x * (1.0 + jnp.tanh(c * (x + 0.044715 * (x * x * x))))


def _group_rms(x, ones_bd):
    ms = _dot_split(x * x, ones_bd) * (1.0 / HEAD_DIM)
    return lax.rsqrt(ms + EPS)


def _rot_half(x):
    n = x.shape[-1]
    lane = lax.broadcasted_iota(jnp.int32, x.shape, 1)
    first = (lane % HEAD_DIM) < (HEAD_DIM // 2)
    return jnp.where(first, -pltpu.roll(x, n - HEAD_DIM // 2, 1), pltpu.roll(x, HEAD_DIM // 2, 1))


def _rope(x, cos, sin):
    return x * cos + _rot_half(x) * sin


def _lane_group(shape, width):
    return lax.broadcasted_iota(jnp.int32, shape, len(shape) - 1) // width


def _mod_kernel(c_ref, w_ref, b_ref, o_ref):
    o_ref[...] = _dot(_silu(c_ref[...]), w_ref[...]) + b_ref[...]


def _mod_call(c, ada_w, ada_b):
    L, D, D6 = ada_w.shape
    B = c.shape[0]
    nj = D6 // D
    return pl.pallas_call(
        _mod_kernel,
        grid=(L, nj),
        in_specs=[pl.BlockSpec((B, D), lambda l, j: (0, 0)),
                  pl.BlockSpec((None, D, D), lambda l, j: (l, 0, j)),
                  pl.BlockSpec((None, 1, D), lambda l, j: (l, 0, j))],
        out_specs=pl.BlockSpec((None, B, D), lambda l, j: (l, 0, j)),
        out_shape=jax.ShapeDtypeStruct((L, B, D6), F32),
        compiler_params=_cparams(("arbitrary", "arbitrary")),
        name="adaln_mod",
    )(c, ada_w, ada_b.reshape(L, 1, D6))


C_RET = 0
C_S5 = 1024
C_SGU = 1280
C_NQ = 1792
C_KVC = 2048
C_KV = 2304
C_GATE = 2816
N_GATE = 12
N_PROJ = 2944


def _proj_columns():
    cols = list(range(0, 1792))
    nq0 = 1792
    for g in range(2):
        for kvh in range(2):
            base = nq0 + (kvh * 2 + g) * HEAD_DIM
            cols += list(range(base, base + HEAD_DIM))
    cols += list(range(2048, 2816 + N_GATE))
    return np.asarray(cols, dtype=np.int32)


def _gate_expand():
    e = np.zeros((128, 6 * 128), dtype=np.float32)
    for g in range(2):
        for br in range(3):
            for kvh in range(2):
                c0 = (g * 3 + br) * 128 + kvh * HEAD_DIM
                e[(kvh * 2 + g) * 3 + br, c0:c0 + HEAD_DIM] = 1.0
    return jnp.asarray(e, dtype=BF16)


def _inproj_kernel(x_ref, g_ref, sc_ref, sh_ref, w_ref, cos_ref, sin_ref, ones_ref, qg_ref, kg_ref,
                   gexp_ref, ret_ref, s5_ref, sgu_ref, nq_ref, kvc_ref, kv_ref, gate_ref):
    x = x_ref[...]
    ms = jnp.mean(x * x, axis=-1, keepdims=True)
    h = x * lax.rsqrt(ms + EPS) * g_ref[...]
    h = h * (1.0 + sc_ref[...]) + sh_ref[...]
    hb = h.astype(BF16)

    p_all = jnp.dot(hb, w_ref[...], preferred_element_type=F32)

    def proj(c0, n):
        return p_all[:, c0:c0 + n]

    cos = cos_ref[...]
    sin = sin_ref[...]
    cos_h, sin_h = cos[:, :128], sin[:, :128]
    ones256 = ones_ref[...]
    ones128 = ones256[:128, :128]

    ret_ref[:, 0:256] = _rope(proj(C_RET, 256), cos, sin) * QK_SCALE
    ret_ref[:, 256:512] = _rope(proj(C_RET + 256, 256), cos, sin)
    ret_ref[:, 512:1024] = proj(C_RET + 512, 512)
    s5_ref[...] = proj(C_S5, 256)
    sgu_ref[...] = proj(C_SGU, 512)
    q = proj(C_NQ, 256)
    qn = q * _group_rms(q, ones256) * qg_ref[...]
    nq_ref[...] = _rope(qn, cos, sin) * QK_SCALE
    kvc_ref[...] = proj(C_KVC, 256)
    kg = kg_ref[...]
    for i in range(2):
        kv = proj(C_KV + i * 256, 256)
        k = kv[:, 0:128]
        kn = k * _group_rms(k, ones128) * kg[i + 1:i + 2, :]
        kv_ref[:, i * 256:i * 256 + 128] = _rope(kn, cos_h, sin_h)
        kv_ref[:, i * 256 + 128:i * 256 + 256] = kv[:, 128:256]
    gate_ref[...] = _dot_split(_sigmoid(proj(C_GATE, 128)), gexp_ref[...])


def _inproj_call(x, norm_g, sc, sh, w, cos, sin, ones256, q_gain, k_gain, gexp, ts):
    B, S, D = x.shape
    row = lambda n: pl.BlockSpec((None, ts, n), lambda b, s: (b, s, 0))
    const = lambda shape: pl.BlockSpec(shape, lambda b, s: (0,) * len(shape),
                                       pipeline_mode=pl.Buffered(1))
    vec = pl.BlockSpec((None, 1, D), lambda b, s: (b, 0, 0))
    return pl.pallas_call(
        _inproj_kernel,
        grid=(B, S // ts),
        in_specs=[row(D), const((1, D)), vec, vec, const((D, N_PROJ)),
                  pl.BlockSpec((ts, 256), lambda b, s: (s, 0)),
                  pl.BlockSpec((ts, 256), lambda b, s: (s, 0)),
                  const((256, 256)), const((1, 256)), const((3, 128)), const((128, 768))],
        out_specs=[row(1024), pl.BlockSpec((ts, 256), lambda b, s: (s, b)), row(512), row(256),
                   row(256), row(512), row(768)],
        out_shape=[jax.ShapeDtypeStruct((B, S, 1024), F32),
                   jax.ShapeDtypeStruct((S, B * 256), F32),
                   jax.ShapeDtypeStruct((B, S, 512), F32),
                   jax.ShapeDtypeStruct((B, S, 256), F32),
                   jax.ShapeDtypeStruct((B, S, 256), F32),
                   jax.ShapeDtypeStruct((B, S, 512), F32),
                   jax.ShapeDtypeStruct((B, S, 768), F32)],
        compiler_params=_cparams(("arbitrary", "arbitrary")),
        name="in_proj",
    )(x, norm_g, sc, sh, w, cos, sin, ones256, q_gain, k_gain, gexp)


def _retention_kernel(blk_ref, dec_ref, zeta_ref, xi_ref, gbd_ref, bd_ref, ones_ref, ng_ref,
                      o_ref, r_ref):
    @pl.when(pl.program_id(1) == 0)
    def _():
        r_ref[...] = jnp.zeros_like(r_ref)

    C = RET_CHUNK
    head = _lane_group((C, GROUP_WIDTH), HEAD_DIM)
    for ci in range(blk_ref.shape[0] // C):
        rows = slice(ci * C, (ci + 1) * C)
        q = blk_ref[rows, 0:256]
        k = blk_ref[rows, 256:512]
        g = blk_ref[rows, 768:1024]
        kb = k.astype(BF16)
        vb = blk_ref[rows, 512:768].astype(BF16)
        n_h = GROUP_WIDTH // HEAD_DIM
        q4 = jnp.concatenate([jnp.where(head == h, q, 0.0) for h in range(n_h)], axis=0)
        o4 = _dot(_dot_nt(q4, kb) * dec_ref[...], vb)
        o = jnp.zeros(q.shape, F32)
        for h in range(n_h):
            o = o + jnp.where(head == h, o4[h * C:(h + 1) * C, :], 0.0)
        r_prev = r_ref[...]
        o = o + _dot(q, r_prev) * xi_ref[...]
        kz_t = (k * zeta_ref[...]).T
        u = _dot(kz_t, vb)
        r_ref[...] = r_prev * gbd_ref[...] + u * bd_ref[...]
        o = o * _group_rms(o, ones_ref[...]) * ng_ref[...]
        o_ref[rows, :] = _silu(g) * o


def _retention_call(ret, dec, zeta, xi, gbd, bd, ones256, norm_g, rows):
    B, S, _ = ret.shape
    C = RET_CHUNK
    const = lambda shape: pl.BlockSpec(shape, lambda b, n: (0,) * len(shape))
    return pl.pallas_call(
        _retention_kernel,
        grid=(B, S // rows),
        in_specs=[pl.BlockSpec((None, rows, 1024), lambda b, n: (b, n, 0)),
                  const((4 * C, C)), const((C, 256)), const((C, 256)),
                  const((256, 256)), const((256, 256)), const((256, 256)), const((1, 256))],
        out_specs=pl.BlockSpec((None, rows, 256), lambda b, n: (b, n, 0)),
        out_shape=jax.ShapeDtypeStruct((B, S, 256), F32),
        scratch_shapes=[pltpu.VMEM((256, 256), F32)],
        compiler_params=_cparams(("arbitrary", "arbitrary")),
        name="retention",
    )(ret, dec, zeta, xi, gbd, bd, ones256, norm_g)


def _retention_consts():
    H, C = GROUP_WIDTH // HEAD_DIM, RET_CHUNK
    log_gamma = np.log1p(-(2.0 ** (-5.0 - np.arange(H, dtype=np.float64))))
    idx = np.arange(C, dtype=np.float64)
    diff = idx[:, None] - idx[None, :]
    dec = np.where(diff >= 0, np.exp(log_gamma[:, None, None] * np.maximum(diff, 0.0)), 0.0)
    zeta = np.exp(log_gamma[:, None] * (C - 1 - idx)[None, :])
    xi = np.exp(log_gamma[:, None] * (idx + 1)[None, :])
    gamma_chunk = np.exp(log_gamma * C)
    zeta256 = np.repeat(zeta.T, HEAD_DIM, axis=1)
    xi256 = np.repeat(xi.T, HEAD_DIM, axis=1)
    hid = np.arange(GROUP_WIDTH) // HEAD_DIM
    bd = (hid[:, None] == hid[None, :]).astype(np.float64)
    gbd = bd * gamma_chunk[hid][:, None]
    f = lambda a: jnp.asarray(a, dtype=F32)
    return f(dec.reshape(H * C, C)), f(zeta256), f(xi256), f(gbd), f(bd)


def _sgu_kernel(blk_ref, w_ref, bias_ref, ones_ref, ng_ref, o_ref):
    C = w_ref.shape[-1]
    grp = _lane_group((C, GROUP_WIDTH), HEAD_DIM)
    tril = (lax.broadcasted_iota(jnp.int32, (C, C), 0) >= lax.broadcasted_iota(jnp.int32, (C, C), 1))
    w_all = jnp.concatenate([jnp.where(tril, w_ref[gi], 0.0).astype(BF16)
                             for gi in range(SGU_GROUPS)], axis=0)
    for ci in range(blk_ref.shape[0] // C):
        rows = slice(ci * C, (ci + 1) * C)
        u = _gelu(blk_ref[rows, 0:256])
        v = _gelu(blk_ref[rows, 256:512])
        vn = v * _group_rms(v, ones_ref[...]) * ng_ref[...]
        r = jnp.dot(w_all, vn.astype(BF16), preferred_element_type=F32)
        s = bias_ref[...]
        for gi in range(SGU_GROUPS):
            s = s + jnp.where(grp == gi, r[gi * C:(gi + 1) * C, :], 0.0)
        o_ref[rows, :] = u * s


def _sgu_call(sgu, w_s, bias256, ones256, norm_g, rows):
    B, S, _ = sgu.shape
    C = SGU_CHUNK
    const = lambda shape: pl.BlockSpec(shape, lambda b, n: (0,) * len(shape))
    return pl.pallas_call(
        _sgu_kernel,
        grid=(B, S // rows),
        in_specs=[pl.BlockSpec((None, rows, 512), lambda b, n: (b, n, 0)),
                  const((SGU_GROUPS, C, C)), const((C, 256)), const((256, 256)), const((1, 256))],
        out_specs=pl.BlockSpec((None, rows, 256), lambda b, n: (b, n, 0)),
        out_shape=jax.ShapeDtypeStruct((B, S, 256), F32),
        compiler_params=_cparams(("arbitrary", "arbitrary")),
        name="spatial_gating",
    )(sgu, w_s, bias256, ones256, norm_g)


def _s5_prep_kernel(lr_ref, li_ref, ldt_ref, bre_ref, bim_ref, a_ref, bx_ref):
    lr = lr_ref[...]
    li = li_ref[...]
    dts = jnp.exp(ldt_ref[...])
    mag = jnp.exp(lr * dts)
    a_re = mag * jnp.cos(li * dts)
    a_im = mag * jnp.sin(li * dts)
    den = lr * lr + li * li
    f_re = ((a_re - 1.0) * lr + a_im * li) / den
    f_im = (a_im * lr - (a_re - 1.0) * li) / den
    a_ref[0:1, :] = a_re
    a_ref[1:2, :] = a_im
    bre = bre_ref[...]
    bim = bim_ref[...]
    bx_ref[:, 0:S5_WIDTH] = (f_re * bre - f_im * bim).astype(BF16)
    bx_ref[:, S5_WIDTH:2 * S5_WIDTH] = (f_re * bim + f_im * bre).astype(BF16)


def _s5_prep_call(lam_re, lam_im, log_dt, b_re, b_im):
    L = lam_re.shape[0]
    W = S5_WIDTH
    lr = lam_re.reshape(L, 1, W)
    li = lam_im.reshape(L, 1, W)
    ldt = jnp.repeat(log_dt, S5_STATE, axis=-1).reshape(L, 1, W)
    eye = jnp.eye(S5_GROUPS, dtype=F32)

    def dense(b):
        return jnp.einsum('lgph,gk->lghkp', b, eye).reshape(L, S5_GROUPS * S5_CH, W)

    vec = pl.BlockSpec((None, 1, W), lambda l: (l, 0, 0))
    mat = pl.BlockSpec((None, GROUP_WIDTH, W), lambda l: (l, 0, 0))
    return pl.pallas_call(
        _s5_prep_kernel,
        grid=(L,),
        in_specs=[vec, vec, vec, mat, mat],
        out_specs=[pl.BlockSpec((None, 2, W), lambda l: (l, 0, 0)),
                   pl.BlockSpec((None, GROUP_WIDTH, 2 * W), lambda l: (l, 0, 0))],
        out_shape=[jax.ShapeDtypeStruct((L, 2, W), F32),
                   jax.ShapeDtypeStruct((L, GROUP_WIDTH, 2 * W), BF16)],
        compiler_params=_cparams(("arbitrary",)),
        name="s5_discretise",
    )(lr, li, ldt, dense(b_re), dense(b_im))


def _split_bf16(x):
    hi = x.astype(BF16)
    return hi, (x - hi.astype(F32)).astype(BF16)


def _s5_kernel(u_ref, a_ref, bx_ref, cre_ref, cim_ref, d_ref, gw_ref, gb_ref, pf_ref, pb_ref, o_ref,
               hre_ref, him_ref, xre_ref, xim_ref, utb_ref, *, nb, tc, sub):
    @pl.when(pl.program_id(0) == 0)
    def _():
        hre_ref[...] = jnp.zeros_like(hre_ref)
        him_ref[...] = jnp.zeros_like(him_ref)

    blk = sub * nb
    for sb in range(tc // sub):
        t_rows = slice(sb * sub, (sb + 1) * sub)
        u_bt = jnp.concatenate([u_ref[t_rows, b * 256:(b + 1) * 256] for b in range(nb)], axis=0)
        hi, lo = _split_bf16(u_bt)
        utb_ref[sb * blk:(sb + 1) * blk, :] = (
            jnp.dot(pf_ref[...], hi, preferred_element_type=F32)
            + jnp.dot(pf_ref[...], lo, preferred_element_type=F32))
    u = utb_ref[...]
    ub = u.astype(BF16)
    xre_ref[...] = jnp.dot(ub, bx_ref[:, 0:S5_WIDTH], preferred_element_type=F32)
    xim_ref[...] = jnp.dot(ub, bx_ref[:, S5_WIDTH:2 * S5_WIDTH], preferred_element_type=F32)
    a_re = jnp.broadcast_to(a_ref[0:1, :], (nb, S5_WIDTH))
    a_im = jnp.broadcast_to(a_ref[1:2, :], (nb, S5_WIDTH))

    def step(t, carry):
        h_re, h_im = carry
        r0 = pl.multiple_of(t * nb, nb)
        n_re = a_re * h_re - a_im * h_im + xre_ref[pl.ds(r0, nb), :]
        n_im = a_re * h_im + a_im * h_re + xim_ref[pl.ds(r0, nb), :]
        xre_ref[pl.ds(r0, nb), :] = n_re
        xim_ref[pl.ds(r0, nb), :] = n_im
        return n_re, n_im

    h_re, h_im = lax.fori_loop(0, tc, step, (hre_ref[...], him_ref[...]))
    hre_ref[...] = h_re
    him_ref[...] = h_im
    y = _dot(xre_ref[...], cre_ref[...]) - _dot(xim_ref[...], cim_ref[...])
    y = _gelu(y + d_ref[...] * u)
    y = y * _sigmoid(_dot(y, gw_ref[...]) + gb_ref[...])
    for sb in range(tc // sub):
        hi, lo = _split_bf16(y[sb * blk:(sb + 1) * blk, :])
        y_bt = (jnp.dot(pb_ref[...], hi, preferred_element_type=F32)
                + jnp.dot(pb_ref[...], lo, preferred_element_type=F32))
        for b in range(nb):
            o_ref[sb * sub:(sb + 1) * sub, b * 256:(b + 1) * 256] = y_bt[b * sub:(b + 1) * sub, :]


def _s5_perms(nb, sub):
    n = nb * sub
    fwd = np.zeros((n, n), dtype=np.float32)
    for t in range(sub):
        for b in range(nb):
            fwd[t * nb + b, b * sub + t] = 1.0
    return jnp.asarray(fwd, dtype=BF16), jnp.asarray(fwd.T, dtype=BF16)


def _s5_call(u, a, bx, cre, cim, d, glu_w, glu_b, nb, tc):
    S = u.shape[0]
    W = S5_WIDTH
    sub = min(tc, max(8, 256 // nb))
    perm_f, perm_b = _s5_perms(nb, sub)
    const = lambda shape: pl.BlockSpec(shape, lambda i: (0,) * len(shape))
    return pl.pallas_call(
        functools.partial(_s5_kernel, nb=nb, tc=tc, sub=sub),
        grid=(S // tc,),
        in_specs=[pl.BlockSpec((tc, nb * 256), lambda i: (i, 0)),
                  const((2, W)), const((256, 2 * W)), const((W, 256)), const((W, 256)),
                  const((1, 256)), const((256, 256)), const((1, 256)),
                  const((nb * sub, nb * sub)), const((nb * sub, nb * sub))],
        out_specs=pl.BlockSpec((tc, nb * 256), lambda i: (i, 0)),
        out_shape=jax.ShapeDtypeStruct((S, nb * 256), F32),
        scratch_shapes=[pltpu.VMEM((nb, W), F32), pltpu.VMEM((nb, W), F32),
                        pltpu.VMEM((tc * nb, W), F32), pltpu.VMEM((tc * nb, W), F32),
                        pltpu.VMEM((tc * nb, 256), F32)],
        compiler_params=_cparams(("arbitrary",)),
        name="s5_scan",
    )(u, a, bx, cre, cim, d, glu_w, glu_b, perm_f, perm_b)


def _compress_kernel(hm_ref, pet_ref, peb_ref, w1t_ref, w1b_ref, w2_ref, ones_ref, kg_ref,
                     cos_ref, sin_ref, k_ref, v_ref):
    hm = hm_ref[...]
    a = _dot(hm + pet_ref[...], w1t_ref[...])
    b = _dot(hm + peb_ref[...], w1b_ref[...])
    n = a.shape[0]
    pre = a + pltpu.roll(b, n - 1, 0)
    cmp = _dot(_gelu(pre), w2_ref[...])
    k = cmp[:, 0:128]
    kn = k * _group_rms(k, ones_ref[...]) * kg_ref[...]
    k_ref[...] = _rope(kn, cos_ref[...], sin_ref[...])
    v_ref[...] = cmp[:, 128:256]


def _compress_call(hm, pet, peb, w1t, w1b, w2, ones128, k_gain, cos_c, sin_c):
    B, n, width = hm.shape
    const = lambda shape: pl.BlockSpec(shape, lambda b: (0,) * len(shape))
    out = pl.BlockSpec((None, n, 128), lambda b: (b, 0, 0))
    return pl.pallas_call(
        _compress_kernel,
        grid=(B,),
        in_specs=[pl.BlockSpec((None, n, width), lambda b: (b, 0, 0)),
                  const((1, width)), const((1, width)), const((width, 512)), const((width, 512)),
                  const((512, 256)), const((128, 128)), const((1, 128)),
                  const((n, 128)), const((n, 128))],
        out_specs=[out, out],
        out_shape=[jax.ShapeDtypeStruct((B, n, 128), F32)] * 2,
        compiler_params=_cparams(("arbitrary",)),
        name="nsa_compress",
    )(hm, pet, peb, w1t, w1b, w2, ones128, k_gain, cos_c, sin_c)


def _nsa_kernel(nq_ref, gate_ref, kc_ref, vc_ref, kv_ref, ovt_ref, o_ref,
                s_ref, m_ref, acc_ref, *, n_cmp_rows, win_keys):
    T = ATT_TILE
    qi = pl.program_id(1)
    t0 = qi * T
    pos_c = t0 + lax.broadcasted_iota(jnp.int32, (T, 1), 0)
    pos_r = t0 + lax.broadcasted_iota(jnp.int32, (1, T), 1)
    kvh_lane = _lane_group((T, 128), HEAD_DIM)
    combos = [(g, kvh) for g in range(2) for kvh in range(2)]
    rows = lambda ci: slice(ci * T, (ci + 1) * T)
    q_all = jnp.concatenate(
        [jnp.where(kvh_lane == kvh, nq_ref[:, g * 128:(g + 1) * 128] * LOG2E, 0.0).astype(BF16)
         for g, kvh in combos], axis=0)
    ones_cols = jnp.ones((T, 128), BF16)

    def head_lanes(x):
        return [jnp.where(kvh_lane == 0, x[rows(2 * g)], x[rows(2 * g + 1)]) for g in range(2)]

    w0 = pl.multiple_of(jnp.maximum(t0 + T - win_keys, 0), T)
    dist = pos_c - (w0 + lax.broadcasted_iota(jnp.int32, (1, win_keys), 1))
    ok_w = (dist >= 0) & (dist < WIN)
    s_w = _dot_nt(q_all, kv_ref[pl.ds(w0, win_keys), 256:384])
    vw_ext = jnp.concatenate([kv_ref[pl.ds(w0, win_keys), 384:512].astype(BF16),
                              jnp.ones((win_keys, 128), BF16)], axis=1)
    o_w = []
    for ci in range(4):
        s = jnp.where(ok_w, s_w[rows(ci)], NEG)
        e = jnp.exp2(s - jnp.max(s, axis=-1, keepdims=True))
        ow = jnp.dot(e.astype(BF16), vw_ext, preferred_element_type=F32)
        o_w.append(ow[:, 0:128] * (1.0 / ow[:, 128:256]))
    o_win = head_lanes(jnp.concatenate(o_w, axis=0))

    cmp_end = lax.broadcasted_iota(jnp.int32, (1, n_cmp_rows), 1) * CMP_STRIDE + (CMP_BLOCK - 1)
    valid = cmp_end <= pos_c
    s_all = _dot_nt(q_all, kc_ref[...])
    p_list = []
    for ci in range(4):
        s = jnp.where(valid, s_all[rows(ci)], NEG)
        e = jnp.exp2(s - jnp.max(s, axis=-1, keepdims=True))
        inv = 1.0 / jnp.sum(e, axis=-1, keepdims=True)
        p_list.append(jnp.where(valid, e * inv, 0.0))
    o_cmp = head_lanes(_dot(jnp.concatenate(p_list, axis=0), vc_ref[...]))

    n_sel = ovt_ref.shape[0]
    blk = lax.broadcasted_iota(jnp.int32, (n_sel, T), 0)
    cur = pos_r // SEL_BLOCK
    forced = (blk == 0) | (blk == cur) | (blk == cur - 1)
    causal_blk = blk <= cur
    ovt = ovt_ref[...]
    sel_rows = []
    for kvh in range(2):
        p_kvh = p_list[kvh] + p_list[2 + kvh]
        ph = p_kvh.astype(BF16)
        plo = (p_kvh - ph.astype(F32)).astype(BF16)
        imp = (lax.dot_general(ovt, ph, (((1,), (1,)), ((), ())), preferred_element_type=F32)
               + lax.dot_general(ovt, plo, (((1,), (1,)), ((), ())), preferred_element_type=F32))
        imp = jnp.where(forced, imp + FORCE_BONUS, imp)
        imp = jnp.where(causal_blk, imp, NEG)
        cnt = jnp.zeros((n_sel, T), F32)
        for j in range(n_sel):
            row = imp[j:j + 1, :]
            ahead = (row > imp) | ((row == imp) & (blk > j))
            cnt = cnt + jnp.where(ahead, 1.0, 0.0)
        sel_t = jnp.where((cnt < SEL_TOPK) & causal_blk, 1.0, 0.0)
        sel_t = jnp.concatenate([sel_t, jnp.zeros((128 - n_sel, T), F32)], axis=0)
        sel_rows.append(sel_t.T.astype(BF16))

    tri = (lax.broadcasted_iota(jnp.int32, (T, T), 0) >= lax.broadcasted_iota(jnp.int32, (T, T), 1))
    bias = [((s.astype(F32) - 1.0) * SEL_MASK).astype(BF16) for s in sel_rows]
    q_ext = jnp.concatenate([q_all, jnp.concatenate([bias[kvh] for _, kvh in combos], axis=0)], axis=1)
    key_blk = lax.broadcasted_iota(jnp.int32, (T, 128), 0) // SEL_BLOCK
    lane_id = lax.broadcasted_iota(jnp.int32, (T, 128), 1)
    m_ref[...] = jnp.full(m_ref.shape, NEG, F32)

    def scores(j, diagonal):
        r0 = pl.multiple_of(j * T, T)
        onehot = jnp.where(lane_id == (T // SEL_BLOCK) * j + key_blk, 1.0, 0.0).astype(BF16)
        k_ext = jnp.concatenate([kv_ref[pl.ds(r0, T), 0:128].astype(BF16), onehot], axis=1)
        s = _dot_nt(q_ext, k_ext)
        for ci in range(4):
            sm = jnp.where(tri, s[rows(ci)], NEG) if diagonal else s[rows(ci)]
            s_ref[j, rows(ci), :] = sm
            half = sm[:, 0:128]
            for c in range(1, T // 128):
                half = jnp.maximum(half, sm[:, c * 128:(c + 1) * 128])
            m_ref[rows(ci), :] = jnp.maximum(m_ref[rows(ci), :], half)

    def pass1(jp, carry):
        scores(2 * jp, False)
        scores(2 * jp + 1, False)
        return carry

    lax.fori_loop(0, qi // 2, pass1, 0)

    @pl.when(qi % 2 == 1)
    def _():
        scores(qi - 1, False)

    scores(qi, True)
    m_ref[...] = jnp.broadcast_to(jnp.max(m_ref[...], axis=-1, keepdims=True), m_ref.shape)
    acc_ref[...] = jnp.zeros_like(acc_ref)

    def weighted(j):
        r0 = pl.multiple_of(j * T, T)
        m = m_ref[...]
        p = jnp.exp2(s_ref[j] - jnp.concatenate([m] * (T // 128), axis=1))
        v_ext = jnp.concatenate([kv_ref[pl.ds(r0, T), 128:256].astype(BF16), ones_cols], axis=1)
        return jnp.dot(p.astype(BF16), v_ext, preferred_element_type=F32)

    def pass2(jp, carry):
        acc_ref[...] += weighted(2 * jp) + weighted(2 * jp + 1)
        return carry

    lax.fori_loop(0, (qi + 1) // 2, pass2, 0)

    @pl.when(qi % 2 == 0)
    def _():
        acc_ref[...] += weighted(qi)
    o_sel = head_lanes(acc_ref[:, 0:128] * (1.0 / acc_ref[:, 128:256]))

    for g in range(2):
        gc = gate_ref[:, (g * 3 + 0) * 128:(g * 3 + 1) * 128]
        gs = gate_ref[:, (g * 3 + 1) * 128:(g * 3 + 2) * 128]
        gw = gate_ref[:, (g * 3 + 2) * 128:(g * 3 + 3) * 128]
        o_ref[:, g * 128:(g + 1) * 128] = gc * o_cmp[g] + gs * o_sel[g] + gw * o_win[g]


def _nsa_call(nq, gates, kcmp, vcmp, kv, ovt):
    B, S, _ = nq.shape
    T = ATT_TILE
    n = kcmp.shape[1]
    n_sel = ovt.shape[0]
    win_keys = min(WIN + T, S)
    return pl.pallas_call(
        functools.partial(_nsa_kernel, n_cmp_rows=n, win_keys=win_keys),
        grid=(B, S // T),
        in_specs=[pl.BlockSpec((None, T, 256), lambda b, i: (b, i, 0)),
                  pl.BlockSpec((None, T, 768), lambda b, i: (b, i, 0)),
                  pl.BlockSpec((None, n, 128), lambda b, i: (b, 0, 0)),
                  pl.BlockSpec((None, n, 128), lambda b, i: (b, 0, 0)),
                  pl.BlockSpec((None, S, 512), lambda b, i: (b, 0, 0)),
                  pl.BlockSpec((n_sel, n), lambda b, i: (0, 0))],
        out_specs=pl.BlockSpec((None, T, 256), lambda b, i: (b, i, 0)),
        out_shape=jax.ShapeDtypeStruct((B, S, 256), F32),
        scratch_shapes=[pltpu.VMEM((S // T, 4 * T, T), F32), pltpu.VMEM((4 * T, 128), F32),
                        pltpu.VMEM((4 * T, 256), F32)],
        compiler_params=_cparams(("arbitrary", "arbitrary")),
        name="nsa_attention",
    )(nq, gates, kcmp, vcmp, kv, ovt)


def _mix_ffn_kernel(x_ref, yr_ref, ys_ref, yg_ref, yn_ref, mg_ref, wo_ref, g1_ref,
                    g_ref, sc_ref, sh_ref, g2_ref, wup_ref, cw_ref, cb_ref, wdn_ref, o_ref,
                    tail_ref, z_ref, *, d_ff, tf):
    @pl.when(pl.program_id(1) == 0)
    def _():
        tail_ref[...] = jnp.zeros_like(tail_ref)

    normed = []
    for i, y_ref in enumerate((yr_ref, ys_ref, yg_ref, yn_ref)):
        y = y_ref[...]
        ms = jnp.mean(y * y, axis=-1, keepdims=True)
        normed.append((y * lax.rsqrt(ms + EPS) * mg_ref[i:i + 1, :]).astype(BF16))
    mixed = jnp.dot(jnp.concatenate(normed, axis=1), wo_ref[...], preferred_element_type=F32)
    x = x_ref[...] + g1_ref[...] * mixed

    ts = x.shape[0]
    ms = jnp.mean(x * x, axis=-1, keepdims=True)
    h = x * lax.rsqrt(ms + EPS) * g_ref[...]
    hb = (h * (1.0 + sc_ref[...]) + sh_ref[...]).astype(BF16)
    row8 = lax.broadcasted_iota(jnp.int32, (8, tf), 0)

    def conv(c0):
        a = jnp.dot(hb, wup_ref[:, c0:c0 + tf], preferred_element_type=F32)
        t6 = tail_ref[6:7, c0:c0 + tf]
        t7 = tail_ref[7:8, c0:c0 + tf]
        a1 = pltpu.roll(a, 1, 0)
        a2 = pltpu.roll(a, 2, 0)
        a1 = jnp.concatenate([jnp.where(row8 == 0, t7, a1[0:8]), a1[8:]], axis=0)
        a2 = jnp.concatenate([jnp.where(row8 == 0, t6, jnp.where(row8 == 1, t7, a2[0:8])), a2[8:]],
                             axis=0)
        tail_ref[:, c0:c0 + tf] = a[ts - 8:ts, :]
        return (cw_ref[2:3, c0:c0 + tf] * a + cw_ref[1:2, c0:c0 + tf] * a1
                + cw_ref[0:1, c0:c0 + tf] * a2 + cb_ref[:, c0:c0 + tf])

    n_chunks = d_ff // tf
    split = (n_chunks // 2) * tf
    acc = None
    for c in range(n_chunks):
        gate = conv(c * tf)
        up = conv(d_ff + c * tf)
        z_ref[:, c * tf:(c + 1) * tf] = (_silu(gate) * up).astype(BF16)
        if (c + 1) * tf == split:
            acc = jnp.dot(z_ref[:, 0:split], wdn_ref[0:split, :], preferred_element_type=F32)
    acc = acc + jnp.dot(z_ref[:, split:d_ff], wdn_ref[split:d_ff, :], preferred_element_type=F32)
    o_ref[...] = x + g2_ref[...] * acc


def _mix_ffn_call(x, y_ret, y_s5, y_sgu, y_nsa, mix_g, w_out, g1,
                  norm_g, sc, sh, g2, w_up, conv_w, conv_b, w_down, ts, tf):
    B, S, D = x.shape
    d_ff = w_down.shape[0]
    row = lambda n: pl.BlockSpec((None, ts, n), lambda b, s: (b, s, 0))
    vec = pl.BlockSpec((None, 1, D), lambda b, s: (b, 0, 0))
    const = lambda shape: pl.BlockSpec(shape, lambda b, s: (0,) * len(shape),
                                       pipeline_mode=pl.Buffered(1))
    return pl.pallas_call(
        functools.partial(_mix_ffn_kernel, d_ff=d_ff, tf=tf),
        grid=(B, S // ts),
        in_specs=[row(D), row(256), pl.BlockSpec((ts, 256), lambda b, s: (s, b)), row(256), row(256),
                  const((4, 256)), const((D, D)), vec,
                  const((1, D)), vec, vec, vec, const((D, 2 * d_ff)),
                  const((CONV_W, 2 * d_ff)), const((1, 2 * d_ff)), const((d_ff, D))],
        out_specs=row(D),
        out_shape=jax.ShapeDtypeStruct((B, S, D), F32),
        scratch_shapes=[pltpu.VMEM((8, 2 * d_ff), F32), pltpu.VMEM((ts, d_ff), BF16)],
        compiler_params=_cparams(("arbitrary", "arbitrary")),
        name="mix_ffn",
    )(x, y_ret, y_s5, y_sgu, y_nsa, mix_g, w_out, g1, norm_g, sc, sh, g2, w_up, conv_w, conv_b, w_down)


def _rope_tables(pos):
    inv = ROPE_THETA ** (-jnp.arange(0, HEAD_DIM, 2, dtype=F32) / HEAD_DIM)
    ang = pos.astype(F32)[:, None] * inv[None, :]
    return jnp.cos(ang), jnp.sin(ang)


def _tile_rope(cos, sin, width):
    reps = width // (HEAD_DIM // 2)
    return jnp.tile(cos, (1, reps)), jnp.tile(sin, (1, reps))


def _block_ones(n):
    hid = np.arange(n) // HEAD_DIM
    return jnp.asarray(hid[:, None] == hid[None, :], dtype=BF16)


def _compress_weights(pe, w1, w2):
    half = CMP_STRIDE * HEAD_DIM
    eye2 = jnp.eye(2, dtype=F32)

    def big(w):
        w = w.reshape(2, CMP_STRIDE, HEAD_DIM, CMP_HIDDEN)
        full = jnp.einsum('ktdj,ka,hb->tkhdabj', w, eye2, eye2)
        return full.reshape(CMP_STRIDE * 4 * HEAD_DIM, 4 * CMP_HIDDEN)

    w1t = big(w1[:, :half])
    w1b = big(w1[:, half:])

    def pe_row(p):
        return jnp.broadcast_to(p[:, :, None, :], (2, CMP_STRIDE, 2, HEAD_DIM)).transpose(
            1, 0, 2, 3).reshape(1, CMP_STRIDE * 4 * HEAD_DIM)

    pet = pe_row(pe[:, :CMP_STRIDE])
    peb = pe_row(pe[:, CMP_STRIDE:])
    w2big = jnp.einsum('kjd,ka,hb->khjabd', w2, eye2, eye2).reshape(4 * CMP_HIDDEN, 4 * HEAD_DIM)
    return pet, peb, w1t.astype(BF16), w1b.astype(BF16), w2big.astype(BF16)


def _nsa_perm():
    idx = np.zeros(GROUP_WIDTH, dtype=np.int32)
    for g in range(2):
        for kvh in range(2):
            for d in range(HEAD_DIM):
                idx[g * 128 + kvh * HEAD_DIM + d] = (kvh * 2 + g) * HEAD_DIM + d
    return idx


def kernel(x, c, norm1_g, norm2_g, ada_w, ada_b, w_in, ret_norm_g, s5_lambda_re, s5_lambda_im, s5_log_dt, s5_b_re, s5_b_im, s5_c_re, s5_c_im, s5_d, s5_glu_w, s5_glu_b, sgu_norm_g, sgu_w, sgu_b, nsa_q_norm_g, nsa_k_norm_g, nsa_cmp_pe, nsa_cmp_w1, nsa_cmp_w2, mix_norm_g, w_out, ffn_w_up, ffn_conv_w, ffn_conv_b, ffn_w_down):
    B, S, D = x.shape
    L = w_in.shape[0]
    ts = min(512, S)
    s5_tc = min(64, S)

    cos, sin = _rope_tables(jnp.arange(S))
    cos256, sin256 = _tile_rope(cos, sin, 256)
    n_cmp_rows = S // CMP_STRIDE
    cos_c, sin_c = _rope_tables(jnp.arange(n_cmp_rows) * CMP_STRIDE + CMP_BLOCK - 1)
    cos_c, sin_c = _tile_rope(cos_c, sin_c, 128)
    ones256 = _block_ones(256)
    ones128 = _block_ones(128)
    dec, zeta256, xi256, gbd, bd = _retention_consts()
    n_sel = S // SEL_BLOCK
    ii = np.arange(n_cmp_rows)[None, :]
    jj = np.arange(n_sel)[:, None]
    overlap_t = ((ii * CMP_STRIDE < (jj + 1) * SEL_BLOCK) & (ii * CMP_STRIDE + CMP_BLOCK > jj * SEL_BLOCK)
                 & (ii < n_cmp_rows - 1))
    ovt = jnp.asarray(overlap_t, dtype=BF16)
    proj_cols = _proj_columns()
    gate_expand = _gate_expand()
    nsa_perm = _nsa_perm()

    mod = _mod_call(c, ada_w, ada_b)
    s5_a, s5_bx = _s5_prep_call(s5_lambda_re, s5_lambda_im, s5_log_dt, s5_b_re, s5_b_im)
    eye_g = jnp.eye(S5_GROUPS, dtype=F32)

    for l in range(L):
        sh1, sc1, g1, sh2, sc2, g2 = [mod[l, :, i * D:(i + 1) * D].reshape(B, 1, D) for i in range(6)]
        w = jnp.pad(w_in[l][:, proj_cols], ((0, 0), (0, N_PROJ - proj_cols.size))).astype(BF16)
        q_gain = jnp.tile(nsa_q_norm_g[l], 4).reshape(1, 256)
        k_gain = jnp.tile(nsa_k_norm_g[l], (1, 2))
        ret, s5u, sgu, nq, kvc, kv, gates = _inproj_call(
            x, norm1_g[l].reshape(1, D), sc1, sh1, w, cos256, sin256, ones256, q_gain, k_gain,
            gate_expand, ts)

        y_ret = _retention_call(ret, dec, zeta256, xi256, gbd, bd, ones256,
                                ret_norm_g[l].reshape(1, 256), ts)

        cre = jnp.einsum('ghp,gk->gpkh', s5_c_re[l], eye_g).reshape(S5_WIDTH, GROUP_WIDTH)
        cim = jnp.einsum('ghp,gk->gpkh', s5_c_im[l], eye_g).reshape(S5_WIDTH, GROUP_WIDTH)
        y_s5 = _s5_call(s5u, s5_a[l], s5_bx[l], cre.astype(BF16), cim.astype(BF16),
                        s5_d[l].reshape(1, 256), s5_glu_w[l].astype(BF16), s5_glu_b[l].reshape(1, 256),
                        B, s5_tc)

        bias256 = jnp.repeat(sgu_b[l].T, HEAD_DIM, axis=1)
        y_sgu = _sgu_call(sgu, sgu_w[l], bias256, ones256, sgu_norm_g[l].reshape(1, 256), ts)

        pet, peb, w1t, w1b, w2big = _compress_weights(nsa_cmp_pe[l], nsa_cmp_w1[l], nsa_cmp_w2[l])
        kcmp, vcmp = _compress_call(kvc.reshape(B, n_cmp_rows, CMP_STRIDE * 256), pet, peb, w1t, w1b,
                                    w2big, ones128, k_gain[0:1], cos_c, sin_c)
        y_nsa = _nsa_call(nq, gates, kcmp, vcmp, kv, ovt)

        mix_g = mix_norm_g[l].reshape(4, 256)
        mix_g = mix_g.at[3].set(mix_g[3][nsa_perm])
        wo = w_out[l]
        wo = jnp.concatenate([wo[:768], wo[768:][nsa_perm]], axis=0).astype(BF16)
        x = _mix_ffn_call(x, y_ret, y_s5, y_sgu, y_nsa, mix_g, wo, g1,
                          norm2_g[l].reshape(1, D), sc2, sh2, g2, ffn_w_up[l].astype(BF16),
                          ffn_conv_w[l], ffn_conv_b[l].reshape(1, -1), ffn_w_down[l].astype(BF16),
                          min(1024, S), 256)
    return x
```

```python
import functools
import math

import jax
import jax.numpy as jnp
import numpy as np
from jax import lax
from jax.experimental import pallas as pl
from jax.experimental.pallas import tpu as pltpu

F32 = jnp.float32
BF16 = jnp.bfloat16
MIX_DTYPE = BF16

HEAD_DIM = 64
GROUP_WIDTH = 256
RET_CHUNK = 128
S5_GROUPS = 16
S5_CH = 16
S5_STATE = 64
S5_WIDTH = S5_GROUPS * S5_STATE
SGU_GROUPS = 4
SGU_CHUNK = 128
NSA_KV_WIDTH = 128
CMP_STRIDE = 16
CMP_BLOCK = 32
CMP_HIDDEN = 128
SEL_BLOCK = 64
SEL_TOPK = 8
FORCE_BONUS = 1e3
WIN = 512
CONV_W = 3
ROPE_THETA = 10000.0
EPS = 1e-6
NEG = -1e9
QK_SCALE = HEAD_DIM ** -0.5
LOG2E = math.log2(math.e)
SEL_MASK = 2.0 ** 30

ATT_TILE = 256
VMEM_LIMIT = 60 * 1024 * 1024


def _cparams(sem):
    return pltpu.CompilerParams(dimension_semantics=sem, vmem_limit_bytes=VMEM_LIMIT)


def _dot(a, b):
    return jnp.dot(a.astype(BF16), b.astype(BF16), preferred_element_type=F32)


def _dot_nt(a, b):
    return lax.dot_general(a.astype(BF16), b.astype(BF16), (((1,), (1,)), ((), ())),
                           preferred_element_type=F32)


def _dot_split(a, b_bf16):
    hi = a.astype(BF16)
    lo = (a - hi.astype(F32)).astype(BF16)
    return (jnp.dot(hi, b_bf16, preferred_element_type=F32)
            + jnp.dot(lo, b_bf16, preferred_element_type=F32))


def _sigmoid(x):
    return 0.5 + 0.5 * jnp.tanh(0.5 * x)


def _silu(x):
    h = 0.5 * x
    return h + h * jnp.tanh(h)


def _gelu(x):
    c = math.sqrt(2.0 / math.pi)
    return 0.5 * x * (1.0 + jnp.tanh(c * (x + 0.044715 * (x * x * x))))


def _group_rms(x, ones_bd):
    ms = _dot_split(x * x, ones_bd) * (1.0 / HEAD_DIM)
    return lax.rsqrt(ms + EPS)


def _rot_half(x):
    n = x.shape[-1]
    lane = lax.broadcasted_iota(jnp.int32, x.shape, 1)
    first = (lane % HEAD_DIM) < (HEAD_DIM // 2)
    return jnp.where(first, -pltpu.roll(x, n - HEAD_DIM // 2, 1), pltpu.roll(x, HEAD_DIM // 2, 1))


def _rope(x, cos, sin):
    return x * cos + _rot_half(x) * sin


def _lane_group(shape, width):
    return lax.broadcasted_iota(jnp.int32, shape, len(shape) - 1) // width


def _mod_kernel(c_ref, w_ref, b_ref, o_ref):
    o_ref[...] = _dot(_silu(c_ref[...]), w_ref[...]) + b_ref[...]


def _mod_call(c, ada_w, ada_b):
    L, D, D6 = ada_w.shape
    B = c.shape[0]
    nj = D6 // D
    return pl.pallas_call(
        _mod_kernel,
        grid=(L, nj),
        in_specs=[pl.BlockSpec((B, D), lambda l, j: (0, 0)),
                  pl.BlockSpec((None, D, D), lambda l, j: (l, 0, j)),
                  pl.BlockSpec((None, 1, D), lambda l, j: (l, 0, j))],
        out_specs=pl.BlockSpec((None, B, D), lambda l, j: (l, 0, j)),
        out_shape=jax.ShapeDtypeStruct((L, B, D6), F32),
        compiler_params=_cparams(("arbitrary", "arbitrary")),
        name="adaln_mod",
    )(c, ada_w, ada_b.reshape(L, 1, D6))


C_RET = 0
C_S5 = 1024
C_SGU = 1280
C_NQ = 1792
C_KVC = 2048
C_KV = 2304
C_GATE = 2816
N_GATE = 12
N_PROJ = 2944


def _proj_columns():
    cols = list(range(0, 1792))
    nq0 = 1792
    for g in range(2):
        for kvh in range(2):
            base = nq0 + (kvh * 2 + g) * HEAD_DIM
            cols += list(range(base, base + HEAD_DIM))
    cols += list(range(2048, 2816 + N_GATE))
    return np.asarray(cols, dtype=np.int32)


def _gate_expand():
    e = np.zeros((128, 6 * 128), dtype=np.float32)
    for g in range(2):
        for br in range(3):
            for kvh in range(2):
                c0 = (g * 3 + br) * 128 + kvh * HEAD_DIM
                e[(kvh * 2 + g) * 3 + br, c0:c0 + HEAD_DIM] = 1.0
    return jnp.asarray(e, dtype=BF16)


def _inproj_kernel(x_ref, g_ref, sc_ref, sh_ref, w_ref, cos_ref, sin_ref, ones_ref, qg_ref, kg_ref,
                   gexp_ref, ret_ref, s5_ref, sgu_ref, nq_ref, kvc_ref, kv_ref, gate_ref):
    x = x_ref[...]
    ms = jnp.mean(x * x, axis=-1, keepdims=True)
    h = x * lax.rsqrt(ms + EPS) * g_ref[...]
    h = h * (1.0 + sc_ref[...]) + sh_ref[...]
    hb = h.astype(BF16)

    p_all = jnp.dot(hb, w_ref[...], preferred_element_type=F32)

    def proj(c0, n):
        return p_all[:, c0:c0 + n]

    cos = cos_ref[...]
    sin = sin_ref[...]
    cos_h, sin_h = cos[:, :128], sin[:, :128]
    ones256 = ones_ref[...]
    ones128 = ones256[:128, :128]

    ret_ref[:, 0:256] = _rope(proj(C_RET, 256), cos, sin) * QK_SCALE
    ret_ref[:, 256:512] = _rope(proj(C_RET + 256, 256), cos, sin)
    ret_ref[:, 512:1024] = proj(C_RET + 512, 512)
    s5_ref[...] = proj(C_S5, 256)
    sgu_ref[...] = proj(C_SGU, 512)
    q = proj(C_NQ, 256)
    qn = q * _group_rms(q, ones256) * qg_ref[...]
    nq_ref[...] = _rope(qn, cos, sin) * QK_SCALE
    kvc_ref[...] = proj(C_KVC, 256)
    kg = kg_ref[...]
    for i in range(2):
        kv = proj(C_KV + i * 256, 256)
        k = kv[:, 0:128]
        kn = k * _group_rms(k, ones128) * kg[i + 1:i + 2, :]
        kv_ref[:, i * 256:i * 256 + 128] = _rope(kn, cos_h, sin_h)
        kv_ref[:, i * 256 + 128:i * 256 + 256] = kv[:, 128:256]
    gate_ref[...] = _dot_split(_sigmoid(proj(C_GATE, 128)), gexp_ref[...])


def _inproj_call(x, norm_g, sc, sh, w, cos, sin, ones256, q_gain, k_gain, gexp, ts):
    B, S, D = x.shape
    row = lambda n: pl.BlockSpec((None, ts, n), lambda b, s: (b, s, 0))
    const = lambda shape: pl.BlockSpec(shape, lambda b, s: (0,) * len(shape),
                                       pipeline_mode=pl.Buffered(1))
    vec = pl.BlockSpec((None, 1, D), lambda b, s: (b, 0, 0))
    return pl.pallas_call(
        _inproj_kernel,
        grid=(B, S // ts),
        in_specs=[row(D), const((1, D)), vec, vec, const((D, N_PROJ)),
                  pl.BlockSpec((ts, 256), lambda b, s: (s, 0)),
                  pl.BlockSpec((ts, 256), lambda b, s: (s, 0)),
                  const((256, 256)), const((1, 256)), const((3, 128)), const((128, 768))],
        out_specs=[row(1024), pl.BlockSpec((ts, 256), lambda b, s: (s, b)), row(512), row(256),
                   row(256), row(512), row(768)],
        out_shape=[jax.ShapeDtypeStruct((B, S, 1024), F32),
                   jax.ShapeDtypeStruct((S, B * 256), F32),
                   jax.ShapeDtypeStruct((B, S, 512), F32),
                   jax.ShapeDtypeStruct((B, S, 256), F32),
                   jax.ShapeDtypeStruct((B, S, 256), F32),
                   jax.ShapeDtypeStruct((B, S, 512), F32),
                   jax.ShapeDtypeStruct((B, S, 768), F32)],
        compiler_params=_cparams(("arbitrary", "arbitrary")),
        name="in_proj",
    )(x, norm_g, sc, sh, w, cos, sin, ones256, q_gain, k_gain, gexp)


def _retention_kernel(blk_ref, dec_ref, zeta_ref, xi_ref, gbd_ref, bd_ref, ones_ref, ng_ref,
                      o_ref, r_ref):
    @pl.when(pl.program_id(1) == 0)
    def _():
        r_ref[...] = jnp.zeros_like(r_ref)

    C = RET_CHUNK
    head = _lane_group((C, GROUP_WIDTH), HEAD_DIM)
    for ci in range(blk_ref.shape[0] // C):
        rows = slice(ci * C, (ci + 1) * C)
        q = blk_ref[rows, 0:256]
        k = blk_ref[rows, 256:512]
        g = blk_ref[rows, 768:1024]
        kb = k.astype(BF16)
        vb = blk_ref[rows, 512:768].astype(BF16)
        n_h = GROUP_WIDTH // HEAD_DIM
        q4 = jnp.concatenate([jnp.where(head == h, q, 0.0) for h in range(n_h)], axis=0)
        o4 = _dot(_dot_nt(q4, kb) * dec_ref[...], vb)
        o = jnp.zeros(q.shape, F32)
        for h in range(n_h):
            o = o + jnp.where(head == h, o4[h * C:(h + 1) * C, :], 0.0)
        r_prev = r_ref[...]
        o = o + _dot(q, r_prev) * xi_ref[...]
        kz_t = (k * zeta_ref[...]).T
        u = _dot(kz_t, vb)
        r_ref[...] = r_prev * gbd_ref[...] + u * bd_ref[...]
        o = o * _group_rms(o, ones_ref[...]) * ng_ref[...]
        o_ref[rows, :] = (_silu(g) * o).astype(o_ref.dtype)


def _retention_call(ret, dec, zeta, xi, gbd, bd, ones256, norm_g, rows):
    B, S, _ = ret.shape
    C = RET_CHUNK
    const = lambda shape: pl.BlockSpec(shape, lambda b, n: (0,) * len(shape))
    return pl.pallas_call(
        _retention_kernel,
        grid=(B, S // rows),
        in_specs=[pl.BlockSpec((None, rows, 1024), lambda b, n: (b, n, 0)),
                  const((4 * C, C)), const((C, 256)), const((C, 256)),
                  const((256, 256)), const((256, 256)), const((256, 256)), const((1, 256))],
        out_specs=pl.BlockSpec((None, rows, 256), lambda b, n: (b, n, 0)),
        out_shape=jax.ShapeDtypeStruct((B, S, 256), MIX_DTYPE),
        scratch_shapes=[pltpu.VMEM((256, 256), F32)],
        compiler_params=_cparams(("arbitrary", "arbitrary")),
        name="retention",
    )(ret, dec, zeta, xi, gbd, bd, ones256, norm_g)


def _retention_consts():
    H, C = GROUP_WIDTH // HEAD_DIM, RET_CHUNK
    log_gamma = np.log1p(-(2.0 ** (-5.0 - np.arange(H, dtype=np.float64))))
    idx = np.arange(C, dtype=np.float64)
    diff = idx[:, None] - idx[None, :]
    dec = np.where(diff >= 0, np.exp(log_gamma[:, None, None] * np.maximum(diff, 0.0)), 0.0)
    zeta = np.exp(log_gamma[:, None] * (C - 1 - idx)[None, :])
    xi = np.exp(log_gamma[:, None] * (idx + 1)[None, :])
    gamma_chunk = np.exp(log_gamma * C)
    zeta256 = np.repeat(zeta.T, HEAD_DIM, axis=1)
    xi256 = np.repeat(xi.T, HEAD_DIM, axis=1)
    hid = np.arange(GROUP_WIDTH) // HEAD_DIM
    bd = (hid[:, None] == hid[None, :]).astype(np.float64)
    gbd = bd * gamma_chunk[hid][:, None]
    f = lambda a: jnp.asarray(a, dtype=F32)
    return f(dec.reshape(H * C, C)), f(zeta256), f(xi256), f(gbd), f(bd)


def _sgu_kernel(blk_ref, w_ref, bias_ref, ones_ref, ng_ref, o_ref):
    C = w_ref.shape[-1]
    grp = _lane_group((C, GROUP_WIDTH), HEAD_DIM)
    tril = (lax.broadcasted_iota(jnp.int32, (C, C), 0) >= lax.broadcasted_iota(jnp.int32, (C, C), 1))
    w_all = jnp.concatenate([jnp.where(tril, w_ref[gi], 0.0).astype(BF16)
                             for gi in range(SGU_GROUPS)], axis=0)
    for ci in range(blk_ref.shape[0] // C):
        rows = slice(ci * C, (ci + 1) * C)
        u = _gelu(blk_ref[rows, 0:256])
        v = _gelu(blk_ref[rows, 256:512])
        vn = v * _group_rms(v, ones_ref[...]) * ng_ref[...]
        r = jnp.dot(w_all, vn.astype(BF16), preferred_element_type=F32)
        s = bias_ref[...]
        for gi in range(SGU_GROUPS):
            s = s + jnp.where(grp == gi, r[gi * C:(gi + 1) * C, :], 0.0)
        o_ref[rows, :] = (u * s).astype(o_ref.dtype)


def _sgu_call(sgu, w_s, bias256, ones256, norm_g, rows):
    B, S, _ = sgu.shape
    C = SGU_CHUNK
    const = lambda shape: pl.BlockSpec(shape, lambda b, n: (0,) * len(shape))
    return pl.pallas_call(
        _sgu_kernel,
        grid=(B, S // rows),
        in_specs=[pl.BlockSpec((None, rows, 512), lambda b, n: (b, n, 0)),
                  const((SGU_GROUPS, C, C)), const((C, 256)), const((256, 256)), const((1, 256))],
        out_specs=pl.BlockSpec((None, rows, 256), lambda b, n: (b, n, 0)),
        out_shape=jax.ShapeDtypeStruct((B, S, 256), MIX_DTYPE),
        compiler_params=_cparams(("arbitrary", "arbitrary")),
        name="spatial_gating",
    )(sgu, w_s, bias256, ones256, norm_g)


def _s5_prep_kernel(lr_ref, li_ref, ldt_ref, bre_ref, bim_ref, a_ref, bx_ref):
    lr = lr_ref[...]
    li = li_ref[...]
    dts = jnp.exp(ldt_ref[...])
    mag = jnp.exp(lr * dts)
    a_re = mag * jnp.cos(li * dts)
    a_im = mag * jnp.sin(li * dts)
    den = lr * lr + li * li
    f_re = ((a_re - 1.0) * lr + a_im * li) / den
    f_im = (a_im * lr - (a_re - 1.0) * li) / den
    a_ref[0:1, :] = a_re
    a_ref[1:2, :] = a_im
    bre = bre_ref[...]
    bim = bim_ref[...]
    bx_ref[:, 0:S5_WIDTH] = (f_re * bre - f_im * bim).astype(BF16)
    bx_ref[:, S5_WIDTH:2 * S5_WIDTH] = (f_re * bim + f_im * bre).astype(BF16)


def _s5_prep_call(lam_re, lam_im, log_dt, b_re, b_im):
    L = lam_re.shape[0]
    W = S5_WIDTH
    lr = lam_re.reshape(L, 1, W)
    li = lam_im.reshape(L, 1, W)
    ldt = jnp.repeat(log_dt, S5_STATE, axis=-1).reshape(L, 1, W)
    eye = jnp.eye(S5_GROUPS, dtype=F32)

    def dense(b):
        return jnp.einsum('lgph,gk->lghkp', b, eye).reshape(L, S5_GROUPS * S5_CH, W)

    vec = pl.BlockSpec((None, 1, W), lambda l: (l, 0, 0))
    mat = pl.BlockSpec((None, GROUP_WIDTH, W), lambda l: (l, 0, 0))
    return pl.pallas_call(
        _s5_prep_kernel,
        grid=(L,),
        in_specs=[vec, vec, vec, mat, mat],
        out_specs=[pl.BlockSpec((None, 2, W), lambda l: (l, 0, 0)),
                   pl.BlockSpec((None, GROUP_WIDTH, 2 * W), lambda l: (l, 0, 0))],
        out_shape=[jax.ShapeDtypeStruct((L, 2, W), F32),
                   jax.ShapeDtypeStruct((L, GROUP_WIDTH, 2 * W), BF16)],
        compiler_params=_cparams(("arbitrary",)),
        name="s5_discretise",
    )(lr, li, ldt, dense(b_re), dense(b_im))


def _split_bf16(x):
    hi = x.astype(BF16)
    return hi, (x - hi.astype(F32)).astype(BF16)


def _s5_kernel(u_ref, a_ref, bx_ref, cre_ref, cim_ref, d_ref, gw_ref, gb_ref, pf_ref, pb_ref, o_ref,
               hre_ref, him_ref, xre_ref, xim_ref, utb_ref, *, nb, tc, sub):
    @pl.when(pl.program_id(0) == 0)
    def _():
        hre_ref[...] = jnp.zeros_like(hre_ref)
        him_ref[...] = jnp.zeros_like(him_ref)

    blk = sub * nb
    for sb in range(tc // sub):
        t_rows = slice(sb * sub, (sb + 1) * sub)
        u_bt = jnp.concatenate([u_ref[t_rows, b * 256:(b + 1) * 256] for b in range(nb)], axis=0)
        hi, lo = _split_bf16(u_bt)
        utb_ref[sb * blk:(sb + 1) * blk, :] = (
            jnp.dot(pf_ref[...], hi, preferred_element_type=F32)
            + jnp.dot(pf_ref[...], lo, preferred_element_type=F32))
    u = utb_ref[...]
    ub = u.astype(BF16)
    xre_ref[...] = jnp.dot(ub, bx_ref[:, 0:S5_WIDTH], preferred_element_type=F32)
    xim_ref[...] = jnp.dot(ub, bx_ref[:, S5_WIDTH:2 * S5_WIDTH], preferred_element_type=F32)
    a_re = jnp.broadcast_to(a_ref[0:1, :], (nb, S5_WIDTH))
    a_im = jnp.broadcast_to(a_ref[1:2, :], (nb, S5_WIDTH))

    def step(t, carry):
        h_re, h_im = carry
        r0 = pl.multiple_of(t * nb, nb)
        n_re = a_re * h_re - a_im * h_im + xre_ref[pl.ds(r0, nb), :]
        n_im = a_re * h_im + a_im * h_re + xim_ref[pl.ds(r0, nb), :]
        xre_ref[pl.ds(r0, nb), :] = n_re
        xim_ref[pl.ds(r0, nb), :] = n_im
        return n_re, n_im

    h_re, h_im = lax.fori_loop(0, tc, step, (hre_ref[...], him_ref[...]))
    hre_ref[...] = h_re
    him_ref[...] = h_im
    y = _dot(xre_ref[...], cre_ref[...]) - _dot(xim_ref[...], cim_ref[...])
    y = _gelu(y + d_ref[...] * u)
    y = y * _sigmoid(_dot(y, gw_ref[...]) + gb_ref[...])
    for sb in range(tc // sub):
        hi, lo = _split_bf16(y[sb * blk:(sb + 1) * blk, :])
        y_bt = (jnp.dot(pb_ref[...], hi, preferred_element_type=F32)
                + jnp.dot(pb_ref[...], lo, preferred_element_type=F32))
        for b in range(nb):
            o_ref[sb * sub:(sb + 1) * sub, b * 256:(b + 1) * 256] = (
                y_bt[b * sub:(b + 1) * sub, :].astype(o_ref.dtype))


def _s5_perms(nb, sub):
    n = nb * sub
    fwd = np.zeros((n, n), dtype=np.float32)
    for t in range(sub):
        for b in range(nb):
            fwd[t * nb + b, b * sub + t] = 1.0
    return jnp.asarray(fwd, dtype=BF16), jnp.asarray(fwd.T, dtype=BF16)


def _s5_call(u, a, bx, cre, cim, d, glu_w, glu_b, nb, tc):
    S = u.shape[0]
    W = S5_WIDTH
    sub = min(tc, max(8, 256 // nb))
    perm_f, perm_b = _s5_perms(nb, sub)
    const = lambda shape: pl.BlockSpec(shape, lambda i: (0,) * len(shape))
    return pl.pallas_call(
        functools.partial(_s5_kernel, nb=nb, tc=tc, sub=sub),
        grid=(S // tc,),
        in_specs=[pl.BlockSpec((tc, nb * 256), lambda i: (i, 0)),
                  const((2, W)), const((256, 2 * W)), const((W, 256)), const((W, 256)),
                  const((1, 256)), const((256, 256)), const((1, 256)),
                  const((nb * sub, nb * sub)), const((nb * sub, nb * sub))],
        out_specs=pl.BlockSpec((tc, nb * 256), lambda i: (i, 0)),
        out_shape=jax.ShapeDtypeStruct((S, nb * 256), MIX_DTYPE),
        scratch_shapes=[pltpu.VMEM((nb, W), F32), pltpu.VMEM((nb, W), F32),
                        pltpu.VMEM((tc * nb, W), F32), pltpu.VMEM((tc * nb, W), F32),
                        pltpu.VMEM((tc * nb, 256), F32)],
        compiler_params=_cparams(("arbitrary",)),
        name="s5_scan",
    )(u, a, bx, cre, cim, d, glu_w, glu_b, perm_f, perm_b)


def _compress_kernel(hm_ref, pet_ref, peb_ref, w1t_ref, w1b_ref, w2_ref, ones_ref, kg_ref,
                     cos_ref, sin_ref, k_ref, v_ref):
    hm = hm_ref[...]
    a = _dot(hm + pet_ref[...], w1t_ref[...])
    b = _dot(hm + peb_ref[...], w1b_ref[...])
    n = a.shape[0]
    pre = a + pltpu.roll(b, n - 1, 0)
    cmp = _dot(_gelu(pre), w2_ref[...])
    k = cmp[:, 0:128]
    kn = k * _group_rms(k, ones_ref[...]) * kg_ref[...]
    k_ref[...] = _rope(kn, cos_ref[...], sin_ref[...])
    v_ref[...] = cmp[:, 128:256]


def _compress_call(hm, pet, peb, w1t, w1b, w2, ones128, k_gain, cos_c, sin_c):
    B, n, width = hm.shape
    const = lambda shape: pl.BlockSpec(shape, lambda b: (0,) * len(shape))
    out = pl.BlockSpec((None, n, 128), lambda b: (b, 0, 0))
    return pl.pallas_call(
        _compress_kernel,
        grid=(B,),
        in_specs=[pl.BlockSpec((None, n, width), lambda b: (b, 0, 0)),
                  const((1, width)), const((1, width)), const((width, 512)), const((width, 512)),
                  const((512, 256)), const((128, 128)), const((1, 128)),
                  const((n, 128)), const((n, 128))],
        out_specs=[out, out],
        out_shape=[jax.ShapeDtypeStruct((B, n, 128), F32)] * 2,
        compiler_params=_cparams(("arbitrary",)),
        name="nsa_compress",
    )(hm, pet, peb, w1t, w1b, w2, ones128, k_gain, cos_c, sin_c)


def _nsa_kernel(nq_ref, gate_ref, kc_ref, vc_ref, kv_ref, ovt_ref, o_ref,
                s_ref, m_ref, acc_ref, *, n_cmp_rows, win_keys):
    T = ATT_TILE
    qi = pl.program_id(1)
    t0 = qi * T
    pos_c = t0 + lax.broadcasted_iota(jnp.int32, (T, 1), 0)
    pos_r = t0 + lax.broadcasted_iota(jnp.int32, (1, T), 1)
    kvh_lane = _lane_group((T, 128), HEAD_DIM)
    combos = [(g, kvh) for g in range(2) for kvh in range(2)]
    rows = lambda ci: slice(ci * T, (ci + 1) * T)
    q_all = jnp.concatenate(
        [jnp.where(kvh_lane == kvh, nq_ref[:, g * 128:(g + 1) * 128] * LOG2E, 0.0).astype(BF16)
         for g, kvh in combos], axis=0)
    ones_cols = jnp.ones((T, 128), BF16)

    def head_lanes(x):
        return [jnp.where(kvh_lane == 0, x[rows(2 * g)], x[rows(2 * g + 1)]) for g in range(2)]

    w0 = pl.multiple_of(jnp.maximum(t0 + T - win_keys, 0), T)
    dist = pos_c - (w0 + lax.broadcasted_iota(jnp.int32, (1, win_keys), 1))
    ok_w = (dist >= 0) & (dist < WIN)
    s_w = _dot_nt(q_all, kv_ref[pl.ds(w0, win_keys), 256:384])
    vw_ext = jnp.concatenate([kv_ref[pl.ds(w0, win_keys), 384:512].astype(BF16),
                              jnp.ones((win_keys, 128), BF16)], axis=1)
    o_w = []
    for ci in range(4):
        s = jnp.where(ok_w, s_w[rows(ci)], NEG)
        e = jnp.exp2(s - jnp.max(s, axis=-1, keepdims=True))
        ow = jnp.dot(e.astype(BF16), vw_ext, preferred_element_type=F32)
        o_w.append(ow[:, 0:128] * (1.0 / ow[:, 128:256]))
    o_win = head_lanes(jnp.concatenate(o_w, axis=0))

    cmp_end = lax.broadcasted_iota(jnp.int32, (1, n_cmp_rows), 1) * CMP_STRIDE + (CMP_BLOCK - 1)
    valid = cmp_end <= pos_c
    s_all = _dot_nt(q_all, kc_ref[...])
    p_list = []
    for ci in range(4):
        s = jnp.where(valid, s_all[rows(ci)], NEG)
        e = jnp.exp2(s - jnp.max(s, axis=-1, keepdims=True))
        inv = 1.0 / jnp.sum(e, axis=-1, keepdims=True)
        p_list.append(jnp.where(valid, e * inv, 0.0))
    o_cmp = head_lanes(_dot(jnp.concatenate(p_list, axis=0), vc_ref[...]))

    n_sel = ovt_ref.shape[0]
    blk = lax.broadcasted_iota(jnp.int32, (n_sel, T), 0)
    cur = pos_r // SEL_BLOCK
    forced = (blk == 0) | (blk == cur) | (blk == cur - 1)
    causal_blk = blk <= cur
    ovt = ovt_ref[...]
    sel_rows = []
    for kvh in range(2):
        p_kvh = p_list[kvh] + p_list[2 + kvh]
        ph = p_kvh.astype(BF16)
        plo = (p_kvh - ph.astype(F32)).astype(BF16)
        imp = (lax.dot_general(ovt, ph, (((1,), (1,)), ((), ())), preferred_element_type=F32)
               + lax.dot_general(ovt, plo, (((1,), (1,)), ((), ())), preferred_element_type=F32))
        imp = jnp.where(forced, imp + FORCE_BONUS, imp)
        imp = jnp.where(causal_blk, imp, NEG)
        cnt = jnp.zeros((n_sel, T), F32)
        for j in range(n_sel):
            row = imp[j:j + 1, :]
            ahead = (row > imp) | ((row == imp) & (blk > j))
            cnt = cnt + jnp.where(ahead, 1.0, 0.0)
        sel_t = jnp.where((cnt < SEL_TOPK) & causal_blk, 1.0, 0.0)
        sel_t = jnp.concatenate([sel_t, jnp.zeros((128 - n_sel, T), F32)], axis=0)
        sel_rows.append(sel_t.T.astype(BF16))

    tri = (lax.broadcasted_iota(jnp.int32, (T, T), 0) >= lax.broadcasted_iota(jnp.int32, (T, T), 1))
    bias = [((s.astype(F32) - 1.0) * SEL_MASK).astype(BF16) for s in sel_rows]
    q_ext = jnp.concatenate([q_all, jnp.concatenate([bias[kvh] for _, kvh in combos], axis=0)], axis=1)
    key_blk = lax.broadcasted_iota(jnp.int32, (T, 128), 0) // SEL_BLOCK
    lane_id = lax.broadcasted_iota(jnp.int32, (T, 128), 1)
    m_ref[...] = jnp.full(m_ref.shape, NEG, F32)

    def scores(j, diagonal):
        r0 = pl.multiple_of(j * T, T)
        onehot = jnp.where(lane_id == (T // SEL_BLOCK) * j + key_blk, 1.0, 0.0).astype(BF16)
        k_ext = jnp.concatenate([kv_ref[pl.ds(r0, T), 0:128].astype(BF16), onehot], axis=1)
        s = _dot_nt(q_ext, k_ext)
        for ci in range(4):
            sm = jnp.where(tri, s[rows(ci)], NEG) if diagonal else s[rows(ci)]
            s_ref[j, rows(ci), :] = sm
            half = sm[:, 0:128]
            for c in range(1, T // 128):
                half = jnp.maximum(half, sm[:, c * 128:(c + 1) * 128])
            m_ref[rows(ci), :] = jnp.maximum(m_ref[rows(ci), :], half)

    def pass1(jp, carry):
        scores(2 * jp, False)
        scores(2 * jp + 1, False)
        return carry

    lax.fori_loop(0, qi // 2, pass1, 0)

    @pl.when(qi % 2 == 1)
    def _():
        scores(qi - 1, False)

    scores(qi, True)
    m_ref[...] = jnp.broadcast_to(jnp.max(m_ref[...], axis=-1, keepdims=True), m_ref.shape)
    acc_ref[...] = jnp.zeros_like(acc_ref)

    def weighted(j):
        r0 = pl.multiple_of(j * T, T)
        m = m_ref[...]
        p = jnp.exp2(s_ref[j] - jnp.concatenate([m] * (T // 128), axis=1))
        v_ext = jnp.concatenate([kv_ref[pl.ds(r0, T), 128:256].astype(BF16), ones_cols], axis=1)
        return jnp.dot(p.astype(BF16), v_ext, preferred_element_type=F32)

    def pass2(jp, carry):
        acc_ref[...] += weighted(2 * jp) + weighted(2 * jp + 1)
        return carry

    lax.fori_loop(0, (qi + 1) // 2, pass2, 0)

    @pl.when(qi % 2 == 0)
    def _():
        acc_ref[...] += weighted(qi)
    o_sel = head_lanes(acc_ref[:, 0:128] * (1.0 / acc_ref[:, 128:256]))

    for g in range(2):
        gc = gate_ref[:, (g * 3 + 0) * 128:(g * 3 + 1) * 128]
        gs = gate_ref[:, (g * 3 + 1) * 128:(g * 3 + 2) * 128]
        gw = gate_ref[:, (g * 3 + 2) * 128:(g * 3 + 3) * 128]
        o_ref[:, g * 128:(g + 1) * 128] = (gc * o_cmp[g] + gs * o_sel[g]
                                            + gw * o_win[g]).astype(o_ref.dtype)


def _nsa_call(nq, gates, kcmp, vcmp, kv, ovt):
    B, S, _ = nq.shape
    T = ATT_TILE
    n = kcmp.shape[1]
    n_sel = ovt.shape[0]
    win_keys = min(WIN + T, S)
    return pl.pallas_call(
        functools.partial(_nsa_kernel, n_cmp_rows=n, win_keys=win_keys),
        grid=(B, S // T),
        in_specs=[pl.BlockSpec((None, T, 256), lambda b, i: (b, i, 0)),
                  pl.BlockSpec((None, T, 768), lambda b, i: (b, i, 0)),
                  pl.BlockSpec((None, n, 128), lambda b, i: (b, 0, 0)),
                  pl.BlockSpec((None, n, 128), lambda b, i: (b, 0, 0)),
                  pl.BlockSpec((None, S, 512), lambda b, i: (b, 0, 0)),
                  pl.BlockSpec((n_sel, n), lambda b, i: (0, 0))],
        out_specs=pl.BlockSpec((None, T, 256), lambda b, i: (b, i, 0)),
        out_shape=jax.ShapeDtypeStruct((B, S, 256), MIX_DTYPE),
        scratch_shapes=[pltpu.VMEM((S // T, 4 * T, T), F32), pltpu.VMEM((4 * T, 128), F32),
                        pltpu.VMEM((4 * T, 256), F32)],
        compiler_params=_cparams(("arbitrary", "arbitrary")),
        name="nsa_attention",
    )(nq, gates, kcmp, vcmp, kv, ovt)


def _mix_ffn_kernel(x_ref, yr_ref, ys_ref, yg_ref, yn_ref, mg_ref, wo_ref, g1_ref,
                    g_ref, sc_ref, sh_ref, g2_ref, wup_ref, cw_ref, cb_ref, wdn_ref, o_ref,
                    tail_ref, z_ref, *, d_ff, tf):
    @pl.when(pl.program_id(1) == 0)
    def _():
        tail_ref[...] = jnp.zeros_like(tail_ref)

    normed = []
    for i, y_ref in enumerate((yr_ref, ys_ref, yg_ref, yn_ref)):
        y = y_ref[...].astype(F32)
        ms = jnp.mean(y * y, axis=-1, keepdims=True)
        normed.append((y * lax.rsqrt(ms + EPS) * mg_ref[i:i + 1, :]).astype(BF16))
    mixed = jnp.dot(jnp.concatenate(normed, axis=1), wo_ref[...], preferred_element_type=F32)
    x = x_ref[...] + g1_ref[...] * mixed

    ts = x.shape[0]
    ms = jnp.mean(x * x, axis=-1, keepdims=True)
    h = x * lax.rsqrt(ms + EPS) * g_ref[...]
    hb = (h * (1.0 + sc_ref[...]) + sh_ref[...]).astype(BF16)
    row8 = lax.broadcasted_iota(jnp.int32, (8, tf), 0)

    def conv(c0):
        a = jnp.dot(hb, wup_ref[:, c0:c0 + tf], preferred_element_type=F32)
        t6 = tail_ref[6:7, c0:c0 + tf]
        t7 = tail_ref[7:8, c0:c0 + tf]
        a1 = pltpu.roll(a, 1, 0)
        a2 = pltpu.roll(a, 2, 0)
        a1 = jnp.concatenate([jnp.where(row8 == 0, t7, a1[0:8]), a1[8:]], axis=0)
        a2 = jnp.concatenate([jnp.where(row8 == 0, t6, jnp.where(row8 == 1, t7, a2[0:8])), a2[8:]],
                             axis=0)
        tail_ref[:, c0:c0 + tf] = a[ts - 8:ts, :]
        return (cw_ref[2:3, c0:c0 + tf] * a + cw_ref[1:2, c0:c0 + tf] * a1
                + cw_ref[0:1, c0:c0 + tf] * a2 + cb_ref[:, c0:c0 + tf])

    n_chunks = d_ff // tf
    split = (n_chunks // 2) * tf
    acc = None
    for c in range(n_chunks):
        gate = conv(c * tf)
        up = conv(d_ff + c * tf)
        z_ref[:, c * tf:(c + 1) * tf] = (_silu(gate) * up).astype(BF16)
        if (c + 1) * tf == split:
            acc = jnp.dot(z_ref[:, 0:split], wdn_ref[0:split, :], preferred_element_type=F32)
    acc = acc + jnp.dot(z_ref[:, split:d_ff], wdn_ref[split:d_ff, :], preferred_element_type=F32)
    o_ref[...] = x + g2_ref[...] * acc


def _mix_ffn_call(x, y_ret, y_s5, y_sgu, y_nsa, mix_g, w_out, g1,
                  norm_g, sc, sh, g2, w_up, conv_w, conv_b, w_down, ts, tf):
    B, S, D = x.shape
    d_ff = w_down.shape[0]
    row = lambda n: pl.BlockSpec((None, ts, n), lambda b, s: (b, s, 0))
    vec = pl.BlockSpec((None, 1, D), lambda b, s: (b, 0, 0))
    const = lambda shape: pl.BlockSpec(shape, lambda b, s: (0,) * len(shape),
                                       pipeline_mode=pl.Buffered(1))
    return pl.pallas_call(
        functools.partial(_mix_ffn_kernel, d_ff=d_ff, tf=tf),
        grid=(B, S // ts),
        in_specs=[row(D), row(256), pl.BlockSpec((ts, 256), lambda b, s: (s, b)), row(256), row(256),
                  const((4, 256)), const((D, D)), vec,
                  const((1, D)), vec, vec, vec, const((D, 2 * d_ff)),
                  const((CONV_W, 2 * d_ff)), const((1, 2 * d_ff)), const((d_ff, D))],
        out_specs=row(D),
        out_shape=jax.ShapeDtypeStruct((B, S, D), F32),
        scratch_shapes=[pltpu.VMEM((8, 2 * d_ff), F32), pltpu.VMEM((ts, d_ff), BF16)],
        compiler_params=_cparams(("arbitrary", "arbitrary")),
        name="mix_ffn",
    )(x, y_ret, y_s5, y_sgu, y_nsa, mix_g, w_out, g1, norm_g, sc, sh, g2, w_up, conv_w, conv_b, w_down)


def _rope_tables(pos):
    inv = ROPE_THETA ** (-jnp.arange(0, HEAD_DIM, 2, dtype=F32) / HEAD_DIM)
    ang = pos.astype(F32)[:, None] * inv[None, :]
    return jnp.cos(ang), jnp.sin(ang)


def _tile_rope(cos, sin, width):
    reps = width // (HEAD_DIM // 2)
    return jnp.tile(cos, (1, reps)), jnp.tile(sin, (1, reps))


def _block_ones(n):
    hid = np.arange(n) // HEAD_DIM
    return jnp.asarray(hid[:, None] == hid[None, :], dtype=BF16)


def _compress_weights(pe, w1, w2):
    half = CMP_STRIDE * HEAD_DIM
    eye2 = jnp.eye(2, dtype=F32)

    def big(w):
        w = w.reshape(2, CMP_STRIDE, HEAD_DIM, CMP_HIDDEN)
        full = jnp.einsum('ktdj,ka,hb->tkhdabj', w, eye2, eye2)
        return full.reshape(CMP_STRIDE * 4 * HEAD_DIM, 4 * CMP_HIDDEN)

    w1t = big(w1[:, :half])
    w1b = big(w1[:, half:])

    def pe_row(p):
        return jnp.broadcast_to(p[:, :, None, :], (2, CMP_STRIDE, 2, HEAD_DIM)).transpose(
            1, 0, 2, 3).reshape(1, CMP_STRIDE * 4 * HEAD_DIM)

    pet = pe_row(pe[:, :CMP_STRIDE])
    peb = pe_row(pe[:, CMP_STRIDE:])
    w2big = jnp.einsum('kjd,ka,hb->khjabd', w2, eye2, eye2).reshape(4 * CMP_HIDDEN, 4 * HEAD_DIM)
    return pet, peb, w1t.astype(BF16), w1b.astype(BF16), w2big.astype(BF16)


def _nsa_perm():
    idx = np.zeros(GROUP_WIDTH, dtype=np.int32)
    for g in range(2):
        for kvh in range(2):
            for d in range(HEAD_DIM):
                idx[g * 128 + kvh * HEAD_DIM + d] = (kvh * 2 + g) * HEAD_DIM + d
    return idx


def kernel(x, c, norm1_g, norm2_g, ada_w, ada_b, w_in, ret_norm_g, s5_lambda_re, s5_lambda_im, s5_log_dt, s5_b_re, s5_b_im, s5_c_re, s5_c_im, s5_d, s5_glu_w, s5_glu_b, sgu_norm_g, sgu_w, sgu_b, nsa_q_norm_g, nsa_k_norm_g, nsa_cmp_pe, nsa_cmp_w1, nsa_cmp_w2, mix_norm_g, w_out, ffn_w_up, ffn_conv_w, ffn_conv_b, ffn_w_down):
    B, S, D = x.shape
    L = w_in.shape[0]
    ts = min(512, S)
    s5_tc = min(64, S)

    cos, sin = _rope_tables(jnp.arange(S))
    cos256, sin256 = _tile_rope(cos, sin, 256)
    n_cmp_rows = S // CMP_STRIDE
    cos_c, sin_c = _rope_tables(jnp.arange(n_cmp_rows) * CMP_STRIDE + CMP_BLOCK - 1)
    cos_c, sin_c = _tile_rope(cos_c, sin_c, 128)
    ones256 = _block_ones(256)
    ones128 = _block_ones(128)
    dec, zeta256, xi256, gbd, bd = _retention_consts()
    n_sel = S // SEL_BLOCK
    ii = np.arange(n_cmp_rows)[None, :]
    jj = np.arange(n_sel)[:, None]
    overlap_t = ((ii * CMP_STRIDE < (jj + 1) * SEL_BLOCK) & (ii * CMP_STRIDE + CMP_BLOCK > jj * SEL_BLOCK)
                 & (ii < n_cmp_rows - 1))
    ovt = jnp.asarray(overlap_t, dtype=BF16)
    proj_cols = _proj_columns()
    gate_expand = _gate_expand()
    nsa_perm = _nsa_perm()

    mod = _mod_call(c, ada_w, ada_b)
    s5_a, s5_bx = _s5_prep_call(s5_lambda_re, s5_lambda_im, s5_log_dt, s5_b_re, s5_b_im)
    eye_g = jnp.eye(S5_GROUPS, dtype=F32)

    for l in range(L):
        sh1, sc1, g1, sh2, sc2, g2 = [mod[l, :, i * D:(i + 1) * D].reshape(B, 1, D) for i in range(6)]
        w = jnp.pad(w_in[l][:, proj_cols], ((0, 0), (0, N_PROJ - proj_cols.size))).astype(BF16)
        q_gain = jnp.tile(nsa_q_norm_g[l], 4).reshape(1, 256)
        k_gain = jnp.tile(nsa_k_norm_g[l], (1, 2))
        ret, s5u, sgu, nq, kvc, kv, gates = _inproj_call(
            x, norm1_g[l].reshape(1, D), sc1, sh1, w, cos256, sin256, ones256, q_gain, k_gain,
            gate_expand, ts)

        y_ret = _retention_call(ret, dec, zeta256, xi256, gbd, bd, ones256,
                                ret_norm_g[l].reshape(1, 256), ts)

        cre = jnp.einsum('ghp,gk->gpkh', s5_c_re[l], eye_g).reshape(S5_WIDTH, GROUP_WIDTH)
        cim = jnp.einsum('ghp,gk->gpkh', s5_c_im[l], eye_g).reshape(S5_WIDTH, GROUP_WIDTH)
        y_s5 = _s5_call(s5u, s5_a[l], s5_bx[l], cre.astype(BF16), cim.astype(BF16),
                        s5_d[l].reshape(1, 256), s5_glu_w[l].astype(BF16), s5_glu_b[l].reshape(1, 256),
                        B, s5_tc)

        bias256 = jnp.repeat(sgu_b[l].T, HEAD_DIM, axis=1)
        y_sgu = _sgu_call(sgu, sgu_w[l], bias256, ones256, sgu_norm_g[l].reshape(1, 256), ts)

        pet, peb, w1t, w1b, w2big = _compress_weights(nsa_cmp_pe[l], nsa_cmp_w1[l], nsa_cmp_w2[l])
        kcmp, vcmp = _compress_call(kvc.reshape(B, n_cmp_rows, CMP_STRIDE * 256), pet, peb, w1t, w1b,
                                    w2big, ones128, k_gain[0:1], cos_c, sin_c)
        y_nsa = _nsa_call(nq, gates, kcmp, vcmp, kv, ovt)

        mix_g = mix_norm_g[l].reshape(4, 256)
        mix_g = mix_g.at[3].set(mix_g[3][nsa_perm])
        wo = w_out[l]
        wo = jnp.concatenate([wo[:768], wo[768:][nsa_perm]], axis=0).astype(BF16)
        x = _mix_ffn_call(x, y_ret, y_s5, y_sgu, y_nsa, mix_g, wo, g1,
                          norm2_g[l].reshape(1, D), sc2, sh2, g2, ffn_w_up[l].astype(BF16),
                          ffn_conv_w[l], ffn_conv_b[l].reshape(1, -1), ffn_w_down[l].astype(BF16),
                          min(1024, S), 256)
    return x
```
